```python
import math
import jax, jax.numpy as jnp
from jax import lax
import numpy as np

D_MODEL = 1024
BATCH = 16
SEQ = 256
DEPTH = 4
DEC_BATCH = 2
DEC_SEQ = 2048
PAST_LEN = 256

GRID_W = 64
N_ATTN_LAYERS = (DEPTH + 1) // 2
N_SSM_LAYERS = DEPTH // 2
MLA_HEADS = 8
Q_LORA = 384
KV_LORA = 256
MLA_NOPE = 64
MLA_ROPE = 32
MLA_QK = MLA_NOPE + MLA_ROPE
MLA_V = 64
GQA_HEADS = 8
GQA_KV_HEADS = 2
GQA_DIM = 64
GQA_REP = GQA_HEADS // GQA_KV_HEADS
IN_DIM = Q_LORA + KV_LORA + MLA_ROPE + GQA_HEADS * GQA_DIM + 2 * GQA_KV_HEADS * GQA_DIM
IN_OFFSETS = (Q_LORA,
              Q_LORA + KV_LORA,
              Q_LORA + KV_LORA + MLA_ROPE,
              Q_LORA + KV_LORA + MLA_ROPE + GQA_HEADS * GQA_DIM,
              Q_LORA + KV_LORA + MLA_ROPE + GQA_HEADS * GQA_DIM + GQA_KV_HEADS * GQA_DIM)
MIX_DIM = MLA_HEADS * MLA_V + GQA_HEADS * GQA_DIM
Q_BLOCK = 128
ROPE_THETA = 10000.0
SSM_GROUP = 16
SSM_GROUPS = D_MODEL // SSM_GROUP
SSM_STATE = 64
N_DIR = 2
DT_MIN = 0.001
DT_MAX = 0.1
D_FF = 2816
CONV_W = 3
EPS = 1e-6

kernel_name = "hybrid_mla_gqa_s5_prefix_diffusion_step"


def rms_norm(x, g):
    xf = x.astype(jnp.float32)
    y = xf * lax.rsqrt(jnp.mean(xf * xf, axis=-1, keepdims=True) + EPS)
    return (y * g.astype(jnp.float32)).astype(x.dtype)


def modulation(cond, w_mod, b_mod):
    m = jax.nn.silu(cond.astype(jnp.float32)) @ w_mod.astype(jnp.float32) + b_mod.astype(jnp.float32)
    return [t[:, None, :].astype(cond.dtype) for t in jnp.split(m, 6, axis=-1)]


def modulate(h, shift, scale):
    return h * (1 + scale) + shift


def grid_angles(rows, rot_dim):
    row = jnp.repeat(jnp.arange(rows, dtype=jnp.float32), GRID_W)
    col = jnp.tile(jnp.arange(GRID_W, dtype=jnp.float32), rows)
    n_freq = rot_dim // 4
    inv_freq = ROPE_THETA ** (-jnp.arange(n_freq, dtype=jnp.float32) / n_freq)
    return row[:, None] * inv_freq, col[:, None] * inv_freq


def rotate(v, ang):
    c = jnp.cos(ang)[None, :, None, :]
    s = jnp.sin(ang)[None, :, None, :]
    v1, v2 = jnp.split(v.astype(jnp.float32), 2, axis=-1)
    return jnp.concatenate([v1 * c - v2 * s, v2 * c + v1 * s], axis=-1)


def axial_rope(x, angles):
    ang_row, ang_col = angles
    x_row, x_col = jnp.split(x, 2, axis=-1)
    return jnp.concatenate([rotate(x_row, ang_row), rotate(x_col, ang_col)], axis=-1).astype(x.dtype)


def rope_tail(x, angles):
    return jnp.concatenate([x[..., :MLA_NOPE], axial_rope(x[..., MLA_NOPE:], angles)], axis=-1)


def attend(q, k, v):
    b, t, g, r, dq = q.shape
    n_blk = t // Q_BLOCK
    qb = q.reshape(b, n_blk, Q_BLOCK, g, r, dq).swapaxes(0, 1)
    kf = k.astype(jnp.float32)
    vf = v.astype(jnp.float32)
    scale = 1.0 / math.sqrt(dq)

    def one_block(qi):
        s = jnp.einsum('bqgrd,bkgd->bgrqk', qi.astype(jnp.float32), kf) * scale
        p = jax.nn.softmax(s, axis=-1)
        return jnp.einsum('bgrqk,bkgd->bqgrd', p, vf)

    out = lax.map(one_block, qb)
    return out.swapaxes(0, 1).reshape(b, t, g * r * v.shape[-1]).astype(v.dtype)


def attn_inputs(h, w_in, g_qa, g_kva, w_uq, g_mq, g_gq, g_gk):
    b, t, _ = h.shape
    q_c, ckv, k_rope, gq, gk, gv = jnp.split(h @ w_in, IN_OFFSETS, axis=-1)
    q_c = rms_norm(q_c, g_qa)
    ckv = rms_norm(ckv, g_kva)
    qm = rms_norm((q_c @ w_uq).reshape(b, t, MLA_HEADS, MLA_QK), g_mq)
    gq = rms_norm(gq.reshape(b, t, GQA_HEADS, GQA_DIM), g_gq)
    gk = rms_norm(gk.reshape(b, t, GQA_KV_HEADS, GQA_DIM), g_gk)
    gv = gv.reshape(b, t, GQA_KV_HEADS, GQA_DIM)
    return qm, ckv, k_rope, gq, gk, gv


def mla_kv(ckv, k_rope, w_ukv, g_mk):
    b, s, _ = ckv.shape
    k_nope, v = jnp.split((ckv @ w_ukv).reshape(b, s, MLA_HEADS, MLA_NOPE + MLA_V), [MLA_NOPE], axis=-1)
    k_pe = jnp.broadcast_to(k_rope[:, :, None, :], (b, s, MLA_HEADS, MLA_ROPE))
    return rms_norm(jnp.concatenate([k_nope, k_pe], axis=-1), g_mk), v


def attn_merge(qm, km, vm, gq, gk, gv, w_out):
    b, t = qm.shape[:2]
    o_m = attend(qm[:, :, :, None, :], km, vm)
    o_g = attend(gq.reshape(b, t, GQA_KV_HEADS, GQA_REP, GQA_DIM), gk, gv)
    return jnp.concatenate([o_m, o_g], axis=-1) @ w_out


def ssm_discretize(a_re, a_im, log_dt, b_re, b_im):
    a_re = a_re.astype(jnp.float32)
    a_im = a_im.astype(jnp.float32)
    dt = jnp.exp(log_dt.astype(jnp.float32))[:, None]
    mag = jnp.exp(a_re * dt)
    ang = a_im * dt
    ab_re, ab_im = mag * jnp.cos(ang), mag * jnp.sin(ang)
    den = a_re * a_re + a_im * a_im
    n_re, n_im = ab_re - 1.0, ab_im
    k_re = ((n_re * a_re + n_im * a_im) / den)[..., None]
    k_im = ((n_im * a_re - n_re * a_im) / den)[..., None]
    b_re = b_re.astype(jnp.float32)
    b_im = b_im.astype(jnp.float32)
    return ab_re, ab_im, k_re * b_re - k_im * b_im, k_re * b_im + k_im * b_re


def scan_combine(e1, e2):
    a1r, a1i, b1r, b1i = e1
    a2r, a2i, b2r, b2i = e2
    return (a2r * a1r - a2i * a1i, a2r * a1i + a2i * a1r,
            a2r * b1r - a2i * b1i + b2r, a2r * b1i + a2i * b1r + b2i)


def ssm_direction(u_g, h0, a_re, a_im, log_dt, b_re, b_im, c_re, c_im, reverse):
    ab_re, ab_im, bb_re, bb_im = ssm_discretize(a_re, a_im, log_dt, b_re, b_im)
    bu_re = jnp.einsum('btgh,gph->btgp', u_g, bb_re)
    bu_im = jnp.einsum('btgh,gph->btgp', u_g, bb_im)
    t_first, t_last = (-1, 0) if reverse else (0, -1)
    if h0 is not None:
        h_re = h0[..., 0].astype(jnp.float32)
        h_im = h0[..., 1].astype(jnp.float32)
        bu_re = bu_re.at[:, t_first].add(ab_re * h_re - ab_im * h_im)
        bu_im = bu_im.at[:, t_first].add(ab_re * h_im + ab_im * h_re)
    a_full_re = jnp.broadcast_to(ab_re, bu_re.shape)
    a_full_im = jnp.broadcast_to(ab_im, bu_im.shape)
    _, _, x_re, x_im = lax.associative_scan(scan_combine, (a_full_re, a_full_im, bu_re, bu_im),
                                            reverse=reverse, axis=1)
    y = (jnp.einsum('btgp,ghp->btgh', x_re, c_re.astype(jnp.float32))
         - jnp.einsum('btgp,ghp->btgh', x_im, c_im.astype(jnp.float32)))
    return y, jnp.stack([x_re[:, t_last], x_im[:, t_last]], axis=-1)


def ssm_mixer(h, h0, a_re, a_im, log_dt, b_re, b_im, c_re, c_im, d_skip, w_glu, b_glu):
    b, t, _ = h.shape
    u_g = h.astype(jnp.float32).reshape(b, t, SSM_GROUPS, SSM_GROUP)
    y = h.astype(jnp.float32) * d_skip.astype(jnp.float32)
    finals = []
    for dr in range(N_DIR):
        y_d, fin = ssm_direction(u_g, None if h0 is None else h0[:, dr], a_re[dr], a_im[dr], log_dt[dr],
                                 b_re[dr], b_im[dr], c_re[dr], c_im[dr], reverse=(dr == 1))
        y = y + y_d.reshape(b, t, D_MODEL)
        finals.append(fin)
    g = jax.nn.gelu(y).astype(h.dtype)
    za, zb = jnp.split(g @ w_glu + b_glu, 2, axis=-1)
    return za * jax.nn.sigmoid(zb), jnp.stack(finals, axis=1)


def conv_ffn(h, w_up, conv_w, conv_b, w_down):
    z = h @ w_up
    z = lax.conv_general_dilated(z, conv_w[:, None, :].astype(z.dtype), window_strides=(1,),
                                 padding=((CONV_W // 2, CONV_W // 2),),
                                 dimension_numbers=('NWC', 'WIO', 'NWC'),
                                 feature_group_count=2 * D_FF) + conv_b
    za, zb = jnp.split(z, 2, axis=-1)
    return (jax.nn.silu(za) * zb) @ w_down


def setup_inputs(seed: int = 0) -> dict:
    key = jax.random.key(seed)
    ks = iter(jax.random.split(key, 48))
    f32 = jnp.float32

    def nrm(shape, fan_in, mult=1.0):
        return jax.random.normal(next(ks), shape, f32) * (mult * fan_in ** -0.5)

    def gain(shape):
        return 1.0 + 0.02 * jax.random.normal(next(ks), shape, f32)

    def small(shape):
        return 0.02 * jax.random.normal(next(ks), shape, f32)

    na, ns = N_ATTN_LAYERS, N_SSM_LAYERS
    ssm_shape = (ns, N_DIR, SSM_GROUPS, SSM_STATE)
    return {
        "x_prompt": jax.random.normal(next(ks), (BATCH, SEQ, D_MODEL), f32),
        "x_sample": jax.random.normal(next(ks), (DEC_BATCH, DEC_SEQ, D_MODEL), f32),
        "c": jax.random.normal(next(ks), (DEC_BATCH, D_MODEL), f32),
        "cache_mla_ckv": jax.random.normal(next(ks), (DEC_BATCH, na, PAST_LEN, KV_LORA), f32),
        "cache_mla_krope": jax.random.normal(next(ks), (DEC_BATCH, na, PAST_LEN, MLA_ROPE), f32),
        "cache_gqa_k": jax.random.normal(next(ks), (DEC_BATCH, na, PAST_LEN, GQA_KV_HEADS, GQA_DIM), f32),
        "cache_gqa_v": jax.random.normal(next(ks), (DEC_BATCH, na, PAST_LEN, GQA_KV_HEADS, GQA_DIM), f32),
        "state_ssm": 0.5 * jax.random.normal(next(ks), (DEC_BATCH, ns, N_DIR, SSM_GROUPS, SSM_STATE, 2), f32),
        "c_ctx": jax.random.normal(next(ks), (D_MODEL,), f32),
        "norm1_g": gain((DEPTH, D_MODEL)),
        "norm2_g": gain((DEPTH, D_MODEL)),
        "w_mod": nrm((DEPTH, D_MODEL, 6 * D_MODEL), D_MODEL, 0.5),
        "b_mod": small((DEPTH, 6 * D_MODEL)),
        "attn_w_in": nrm((na, D_MODEL, IN_DIM), D_MODEL),
        "attn_qa_norm_g": gain((na, Q_LORA)),
        "attn_kva_norm_g": gain((na, KV_LORA)),
        "attn_w_uq": nrm((na, Q_LORA, MLA_HEADS * MLA_QK), Q_LORA),
        "attn_w_ukv": nrm((na, KV_LORA, MLA_HEADS * (MLA_NOPE + MLA_V)), KV_LORA),
        "attn_mla_q_norm_g": gain((na, MLA_QK)),
        "attn_mla_k_norm_g": gain((na, MLA_QK)),
        "attn_gqa_q_norm_g": gain((na, GQA_DIM)),
        "attn_gqa_k_norm_g": gain((na, GQA_DIM)),
        "attn_w_out": nrm((na, MIX_DIM, D_MODEL), MIX_DIM),
        "ssm_a_re": -0.5 + 0.01 * jax.random.normal(next(ks), ssm_shape, f32),
        "ssm_a_im": math.pi * jnp.arange(SSM_STATE, dtype=f32) + 0.01 * jax.random.normal(next(ks), ssm_shape, f32),
        "ssm_log_dt": jax.random.uniform(next(ks), (ns, N_DIR, SSM_GROUPS), f32,
                                         math.log(DT_MIN), math.log(DT_MAX)),
        "ssm_b_re": nrm(ssm_shape + (SSM_GROUP,), 2 * SSM_GROUP),
        "ssm_b_im": nrm(ssm_shape + (SSM_GROUP,), 2 * SSM_GROUP),
        "ssm_c_re": nrm((ns, N_DIR, SSM_GROUPS, SSM_GROUP, SSM_STATE), SSM_STATE),
        "ssm_c_im": nrm((ns, N_DIR, SSM_GROUPS, SSM_GROUP, SSM_STATE), SSM_STATE),
        "ssm_d": jax.random.normal(next(ks), (ns, D_MODEL), f32),
        "ssm_w_glu": nrm((ns, D_MODEL, 2 * D_MODEL), D_MODEL),
        "ssm_b_glu": small((ns, 2 * D_MODEL)),
        "ffn_w_up": nrm((DEPTH, D_MODEL, 2 * D_FF), D_MODEL),
        "ffn_conv_w": nrm((DEPTH, CONV_W, 2 * D_FF), CONV_W),
        "ffn_conv_b": small((DEPTH, 2 * D_FF)),
        "ffn_w_down": nrm((DEPTH, D_FF, D_MODEL), D_FF),
    }


def reference(x_prompt, x_sample, c, cache_mla_ckv, cache_mla_krope, cache_gqa_k, cache_gqa_v, state_ssm, c_ctx,
              norm1_g, norm2_g, w_mod, b_mod,
              attn_w_in, attn_qa_norm_g, attn_kva_norm_g, attn_w_uq, attn_w_ukv,
              attn_mla_q_norm_g, attn_mla_k_norm_g, attn_gqa_q_norm_g, attn_gqa_k_norm_g, attn_w_out,
              ssm_a_re, ssm_a_im, ssm_log_dt, ssm_b_re, ssm_b_im, ssm_c_re, ssm_c_im, ssm_d, ssm_w_glu, ssm_b_glu,
              ffn_w_up, ffn_conv_w, ffn_conv_b, ffn_w_down):
    rows = x_sample.shape[1] // GRID_W
    ang_mla = grid_angles(rows, MLA_ROPE)
    ang_gqa = grid_angles(rows, GQA_DIM)
    xp, xs = x_prompt, x_sample
    new_ckv, new_krope, new_gk, new_gv, new_ssm = [], [], [], [], []

    for i in range(DEPTH):
        j = i // 2
        sh1p, sc1p, gt1p, sh2p, sc2p, gt2p = modulation(c_ctx[None, :], w_mod[i], b_mod[i])
        sh1s, sc1s, gt1s, sh2s, sc2s, gt2s = modulation(c, w_mod[i], b_mod[i])
        hp = modulate(rms_norm(xp, norm1_g[i]), sh1p, sc1p)
        hs = modulate(rms_norm(xs, norm1_g[i]), sh1s, sc1s)

        if i % 2 == 0:
            proj = (attn_w_in[j], attn_qa_norm_g[j], attn_kva_norm_g[j], attn_w_uq[j],
                    attn_mla_q_norm_g[j], attn_gqa_q_norm_g[j], attn_gqa_k_norm_g[j])
            qm, ckv, krope, gq, gk, gv = attn_inputs(hp, *proj)
            km, vm = mla_kv(ckv, krope, attn_w_ukv[j], attn_mla_k_norm_g[j])
            mp = attn_merge(qm, km, vm, gq, gk, gv, attn_w_out[j])
            new_ckv.append(ckv)
            new_krope.append(krope)
            new_gk.append(gk)
            new_gv.append(gv)
            qm, ckv, krope, gq, gk, gv = attn_inputs(hs, *proj)
            km, vm = mla_kv(ckv, krope, attn_w_ukv[j], attn_mla_k_norm_g[j])
            qm, km = rope_tail(qm, ang_mla), rope_tail(km, ang_mla)
            gq, gk = axial_rope(gq, ang_gqa), axial_rope(gk, ang_gqa)
            km_c, vm_c = mla_kv(cache_mla_ckv[:, j], cache_mla_krope[:, j], attn_w_ukv[j], attn_mla_k_norm_g[j])
            ms = attn_merge(qm, jnp.concatenate([km_c, km], axis=1), jnp.concatenate([vm_c, vm], axis=1),
                            gq, jnp.concatenate([cache_gqa_k[:, j], gk], axis=1),
                            jnp.concatenate([cache_gqa_v[:, j], gv], axis=1), attn_w_out[j])
        else:
            sp = (ssm_a_re[j], ssm_a_im[j], ssm_log_dt[j], ssm_b_re[j], ssm_b_im[j], ssm_c_re[j], ssm_c_im[j],
                  ssm_d[j], ssm_w_glu[j], ssm_b_glu[j])
            mp, fin = ssm_mixer(hp, None, *sp)
            new_ssm.append(fin)
            ms, _ = ssm_mixer(hs, state_ssm[:, j], *sp)

        xp = xp + gt1p * mp
        xs = xs + gt1s * ms
        fp = (ffn_w_up[i], ffn_conv_w[i], ffn_conv_b[i], ffn_w_down[i])
        xp = xp + gt2p * conv_ffn(modulate(rms_norm(xp, norm2_g[i]), sh2p, sc2p), *fp)
        xs = xs + gt2s * conv_ffn(modulate(rms_norm(xs, norm2_g[i]), sh2s, sc2s), *fp)

    new_mla_ckv = jnp.stack(new_ckv, axis=1)
    new_mla_krope = jnp.stack(new_krope, axis=1)
    new_gqa_k = jnp.stack(new_gk, axis=1)
    new_gqa_v = jnp.stack(new_gv, axis=1)
    new_state_ssm = jnp.stack(new_ssm, axis=1)
    return (xp, xs, new_mla_ckv, new_mla_krope, new_gqa_k, new_gqa_v, new_state_ssm)
```

```python
import functools
import math

import jax
import jax.numpy as jnp
import numpy as np
from jax import lax
from jax.experimental import pallas as pl
from jax.experimental.pallas import tpu as pltpu

F32 = jnp.float32
BF16 = jnp.bfloat16
HIGHEST = lax.Precision.HIGHEST

LANES = 128
BF16_SUBLANES = 16
VMEM_LIMIT_BYTES = 56 * 1024 * 1024

D_MODEL = 1024
BATCH = 16
SEQ = 256
DEPTH = 4
DEC_BATCH = 2
DEC_SEQ = 2048
PAST_LEN = 256
GRID_W = 64
MLA_HEADS = 8
Q_LORA = 384
KV_LORA = 256
MLA_NOPE = 64
MLA_ROPE = 32
MLA_QK = MLA_NOPE + MLA_ROPE
MLA_V = 64
GQA_HEADS = 8
GQA_KV_HEADS = 2
GQA_DIM = 64
GQA_REP = GQA_HEADS // GQA_KV_HEADS
ROPE_THETA = 10000.0
SSM_GROUP = 16
SSM_GROUPS = D_MODEL // SSM_GROUP
SSM_STATE = 64
D_FF = 2816
EPS = 1e-6

N_PROMPT = BATCH * SEQ
N_SAMPLE = DEC_BATCH * DEC_SEQ
N_TOK = N_PROMPT + N_SAMPLE
N_HEADS = MLA_HEADS + GQA_HEADS
HEAD_PAD = LANES
KV_LEN_SAMPLE = PAST_LEN + DEC_SEQ

IN_Q = 0
IN_CKV = IN_Q + Q_LORA
IN_KPE = IN_CKV + KV_LORA
IN_GQ = IN_KPE + HEAD_PAD
IN_GK = IN_GQ + GQA_HEADS * HEAD_PAD
IN_GV = IN_GK + GQA_KV_HEADS * HEAD_PAD
IN_EXT = IN_GV + GQA_KV_HEADS * GQA_DIM

K_ALL = (MLA_HEADS + GQA_KV_HEADS) * HEAD_PAD
V_ALL = MLA_HEADS * MLA_V + GQA_KV_HEADS * GQA_DIM

SSM_CHUNK = 16
CHUNK_W = SSM_CHUNK * SSM_GROUP
N_CHUNK_ROWS = N_TOK // SSM_CHUNK
N_CHUNK_PROMPT = N_PROMPT // SSM_CHUNK
STATE_W = 2 * SSM_STATE

TM_FFN = 1024
TN_FFN = 256
FFN_HALO = BF16_SUBLANES
TM_PRE = 512
TQ_SAMPLE = 256
TM_SSM = 1024
TN_MOD = 1536
GB_SSM = 8


def _cparams(sem):
    return pltpu.CompilerParams(dimension_semantics=sem, vmem_limit_bytes=VMEM_LIMIT_BYTES)


def _rms(x, g):
    return x * lax.rsqrt(jnp.mean(x * x, axis=-1, keepdims=True) + EPS) * g


def _mod_row(tm):
    n_prompt_tiles = N_PROMPT // tm
    tiles_per_seq = DEC_SEQ // tm

    def row(i):
        return jnp.where(i < n_prompt_tiles, 0, 1 + (i - n_prompt_tiles) // tiles_per_seq)

    return row


def _mod_kernel(cond_ref, w_ref, b_ref, o_ref):
    a = jax.nn.silu(cond_ref[...])
    o_ref[0] = jnp.dot(a, w_ref[0], preferred_element_type=F32, precision=HIGHEST) + b_ref[0]


def _modulation(cond8, w_mod, b_mod):
    n = w_mod.shape[-1]
    return pl.pallas_call(
        _mod_kernel,
        out_shape=jax.ShapeDtypeStruct((DEPTH, 8, n), F32),
        grid=(DEPTH, n // TN_MOD),
        in_specs=[
            pl.BlockSpec((8, D_MODEL), lambda l, j: (0, 0)),
            pl.BlockSpec((1, D_MODEL, TN_MOD), lambda l, j: (l, 0, j)),
            pl.BlockSpec((1, 1, TN_MOD), lambda l, j: (l, 0, j)),
        ],
        out_specs=pl.BlockSpec((1, 8, TN_MOD), lambda l, j: (l, 0, j)),
        compiler_params=_cparams(("arbitrary", "arbitrary")),
        name="modulation",
    )(cond8, w_mod, b_mod.reshape(DEPTH, 1, n))


def _ffn_kernel(xp_ref, x_ref, xn_ref, mod_ref, g_ref, wa_ref, wb_ref, cwa_ref, cwb_ref, cba_ref, cbb_ref,
                wd_ref, o_ref, h_scr, acc_scr):
    i = pl.program_id(0)
    j = pl.program_id(1)
    tm = x_ref.shape[0]
    m = mod_ref[0]
    sh, sc, gt = m[3:4], m[4:5], m[5:6]

    @pl.when(j == 0)
    def _():
        g = g_ref[...]

        def hmod(x):
            return (_rms(x, g) * (1.0 + sc) + sh).astype(BF16)

        h_scr[0:FFN_HALO] = hmod(xp_ref[...])
        h_scr[FFN_HALO:FFN_HALO + tm] = hmod(x_ref[...])
        h_scr[FFN_HALO + tm:] = hmod(xn_ref[...])
        acc_scr[...] = jnp.zeros_like(acc_scr)

    h = h_scr[...]
    za = jnp.dot(h, wa_ref[...].astype(BF16), preferred_element_type=F32)
    zb = jnp.dot(h, wb_ref[...].astype(BF16), preferred_element_type=F32)
    rows = tm + 2 * FFN_HALO
    seq_len = jnp.where(i * tm < N_PROMPT, SEQ, DEC_SEQ)
    pos = (i * tm + lax.broadcasted_iota(jnp.int32, (tm, 1), 0)) & (seq_len - 1)
    first = pos == 0
    last = pos == seq_len - 1

    def conv(z, cw_ref, cb_ref):
        cw = cw_ref[...]
        zp = pltpu.roll(z, 1, 0)[FFN_HALO:FFN_HALO + tm]
        zn = pltpu.roll(z, rows - 1, 0)[FFN_HALO:FFN_HALO + tm]
        zc = z[FFN_HALO:FFN_HALO + tm]
        return (jnp.where(first, 0.0, zp) * cw[0:1] + zc * cw[1:2] + jnp.where(last, 0.0, zn) * cw[2:3]
                + cb_ref[...])

    a = conv(za, cwa_ref, cba_ref)
    b = conv(zb, cwb_ref, cbb_ref)
    act = (jax.nn.silu(a) * b).astype(BF16)
    acc_scr[...] += jnp.dot(act, wd_ref[...].astype(BF16), preferred_element_type=F32)

    @pl.when(j == pl.num_programs(1) - 1)
    def _():
        o_ref[...] = x_ref[...] + gt * acc_scr[...]


def _conv_ffn(x, mod_l, g, w_up, conv_w, conv_b, w_down):
    tm, tn, halo = TM_FFN, TN_FFN, FFN_HALO
    n_ff = D_FF // tn
    row = _mod_row(tm)
    n_halo_blocks = N_TOK // halo
    return pl.pallas_call(
        _ffn_kernel,
        out_shape=jax.ShapeDtypeStruct((N_TOK, D_MODEL), F32),
        grid=(N_TOK // tm, n_ff),
        in_specs=[
            pl.BlockSpec((halo, D_MODEL), lambda i, j: (jnp.maximum(i * (tm // halo) - 1, 0), 0)),
            pl.BlockSpec((tm, D_MODEL), lambda i, j: (i, 0)),
            pl.BlockSpec((halo, D_MODEL), lambda i, j: (jnp.minimum((i + 1) * (tm // halo), n_halo_blocks - 1), 0)),
            pl.BlockSpec((1, 6, D_MODEL), lambda i, j: (row(i), 0, 0)),
            pl.BlockSpec((1, D_MODEL), lambda i, j: (0, 0)),
            pl.BlockSpec((D_MODEL, tn), lambda i, j: (0, j)),
            pl.BlockSpec((D_MODEL, tn), lambda i, j: (0, n_ff + j)),
            pl.BlockSpec((3, tn), lambda i, j: (0, j)),
            pl.BlockSpec((3, tn), lambda i, j: (0, n_ff + j)),
            pl.BlockSpec((1, tn), lambda i, j: (0, j)),
            pl.BlockSpec((1, tn), lambda i, j: (0, n_ff + j)),
            pl.BlockSpec((tn, D_MODEL), lambda i, j: (j, 0)),
        ],
        out_specs=pl.BlockSpec((tm, D_MODEL), lambda i, j: (i, 0)),
        scratch_shapes=[pltpu.VMEM((tm + 2 * halo, D_MODEL), BF16), pltpu.VMEM((tm, D_MODEL), F32)],
        compiler_params=_cparams(("arbitrary", "arbitrary")),
        name="conv_ffn",
    )(x, x, x, mod_l, g.reshape(1, D_MODEL), w_up, w_up, conv_w, conv_w, conv_b.reshape(1, -1),
      conv_b.reshape(1, -1), w_down)


def _rope(x, cos, sin, half):
    lane = lax.broadcasted_iota(jnp.int32, x.shape, 1)
    first = ((lane // half) & 1) == 0
    partner = jnp.where(first, pltpu.roll(x, LANES - half, 1), pltpu.roll(x, half, 1))
    return x * cos + partner * sin


def _head_norm(xh, g, dim):
    return xh * lax.rsqrt(jnp.sum(xh * xh, axis=-1, keepdims=True) * (1.0 / dim) + EPS) * g


def _mla_kv_heads(ckv_bf, kpe, w_ukv_ref, g_mk, cos_m, sin_m, k_ref, v_ref):
    kv = jnp.dot(ckv_bf, w_ukv_ref[...], preferred_element_type=F32)
    for h in range(MLA_HEADS):
        kh = kv[:, h * HEAD_PAD:(h + 1) * HEAD_PAD] + kpe
        kh = _head_norm(kh, g_mk, MLA_QK)
        if cos_m is not None:
            kh = _rope(kh, cos_m, sin_m, MLA_ROPE // 4)
        k_ref[:, h * HEAD_PAD:(h + 1) * HEAD_PAD] = kh.astype(BF16)
    v_ref[:, 0:MLA_HEADS * MLA_V] = kv[:, MLA_HEADS * HEAD_PAD:].astype(BF16)


def _attn_pre_kernel(x_ref, mod_ref, g1_ref, w_in_ref, g_qa_ref, g_kva_ref, w_uq_ref, w_ukv_ref,
                     g_mq_ref, g_mk_ref, g_gq_ref, g_gk_ref, cos_m_ref, sin_m_ref, cos_g_ref, sin_g_ref,
                     q_ref, k_ref, v_ref, ckv_ref, kpe_ref, gk_ref, gv_ref):
    m = mod_ref[0]
    sh, sc = m[0:1], m[1:2]
    h = (_rms(x_ref[...], g1_ref[...]) * (1.0 + sc) + sh).astype(BF16)
    p = jnp.dot(h, w_in_ref[...], preferred_element_type=F32)
    cos_m, sin_m = cos_m_ref[...], sin_m_ref[...]
    cos_g, sin_g = cos_g_ref[...], sin_g_ref[...]

    q_c = _rms(p[:, IN_Q:IN_CKV], g_qa_ref[...]).astype(BF16)
    ckv = _rms(p[:, IN_CKV:IN_KPE], g_kva_ref[...])
    kpe = p[:, IN_KPE:IN_GQ]
    ckv_ref[...] = ckv
    kpe_ref[...] = kpe

    qm = jnp.dot(q_c, w_uq_ref[...], preferred_element_type=F32)
    g_mq = g_mq_ref[...]
    for hd in range(MLA_HEADS):
        qh = _head_norm(qm[:, hd * HEAD_PAD:(hd + 1) * HEAD_PAD], g_mq, MLA_QK)
        qh = _rope(qh, cos_m, sin_m, MLA_ROPE // 4) * (1.0 / math.sqrt(MLA_QK))
        q_ref[:, hd * HEAD_PAD:(hd + 1) * HEAD_PAD] = qh.astype(BF16)

    _mla_kv_heads(ckv.astype(BF16), kpe, w_ukv_ref, g_mk_ref[...], cos_m, sin_m, k_ref, v_ref)

    g_gq = g_gq_ref[...]
    for hd in range(GQA_HEADS):
        qh = _head_norm(p[:, IN_GQ + hd * HEAD_PAD:IN_GQ + (hd + 1) * HEAD_PAD], g_gq, GQA_DIM)
        qh = _rope(qh, cos_g, sin_g, GQA_DIM // 4) * (1.0 / math.sqrt(GQA_DIM))
        q_ref[:, (MLA_HEADS + hd) * HEAD_PAD:(MLA_HEADS + hd + 1) * HEAD_PAD] = qh.astype(BF16)
    g_gk = g_gk_ref[...]
    for hd in range(GQA_KV_HEADS):
        kh = _head_norm(p[:, IN_GK + hd * HEAD_PAD:IN_GK + (hd + 1) * HEAD_PAD], g_gk, GQA_DIM)
        gk_ref[:, hd * HEAD_PAD:(hd + 1) * HEAD_PAD] = kh
        kh = _rope(kh, cos_g, sin_g, GQA_DIM // 4)
        k_ref[:, (MLA_HEADS + hd) * HEAD_PAD:(MLA_HEADS + hd + 1) * HEAD_PAD] = kh.astype(BF16)
    gv = p[:, IN_GV:IN_EXT]
    gv_ref[...] = gv
    v_ref[:, MLA_HEADS * MLA_V:] = gv.astype(BF16)


def _attn_pre(x, mod_l, g1, aw, tabs):
    tm = TM_PRE
    row = _mod_row(tm)
    full = lambda a: pl.BlockSpec(a.shape, lambda i: (0,) * a.ndim)
    tok = lambda w: pl.BlockSpec((tm, w), lambda i: (i, 0))
    consts = [g1.reshape(1, D_MODEL), aw["w_in"], aw["g_qa"], aw["g_kva"], aw["w_uq"], aw["w_ukv"],
              aw["g_mq"], aw["g_mk"], aw["g_gq"], aw["g_gk"]]
    return pl.pallas_call(
        _attn_pre_kernel,
        out_shape=[
            jax.ShapeDtypeStruct((N_TOK, N_HEADS * HEAD_PAD), BF16),
            jax.ShapeDtypeStruct((N_TOK, K_ALL), BF16),
            jax.ShapeDtypeStruct((N_TOK, V_ALL), BF16),
            jax.ShapeDtypeStruct((N_TOK, KV_LORA), F32),
            jax.ShapeDtypeStruct((N_TOK, HEAD_PAD), F32),
            jax.ShapeDtypeStruct((N_TOK, GQA_KV_HEADS * HEAD_PAD), F32),
            jax.ShapeDtypeStruct((N_TOK, GQA_KV_HEADS * GQA_DIM), F32),
        ],
        grid=(N_TOK // tm,),
        in_specs=[tok(D_MODEL), pl.BlockSpec((1, 6, D_MODEL), lambda i: (row(i), 0, 0))]
        + [full(a) for a in consts] + [tok(HEAD_PAD)] * 4,
        out_specs=[tok(N_HEADS * HEAD_PAD), tok(K_ALL), tok(V_ALL), tok(KV_LORA), tok(HEAD_PAD),
                   tok(GQA_KV_HEADS * HEAD_PAD), tok(GQA_KV_HEADS * GQA_DIM)],
        compiler_params=_cparams(("arbitrary",)),
        name="attn_pre",
    )(x, mod_l, *consts, *tabs)


def _cache_kv_kernel(ckv_ref, kpe_ref, w_ukv_ref, g_mk_ref, k_ref, v_ref):
    _mla_kv_heads(ckv_ref[...].astype(BF16), kpe_ref[...], w_ukv_ref, g_mk_ref[...], None, None, k_ref, v_ref)


def _cache_kv(ckv, kpe_pad, aw):
    n = ckv.shape[0]
    return pl.pallas_call(
        _cache_kv_kernel,
        out_shape=[jax.ShapeDtypeStruct((n, MLA_HEADS * HEAD_PAD), BF16),
                   jax.ShapeDtypeStruct((n, MLA_HEADS * MLA_V), BF16)],
        name="cache_kv",
    )(ckv, kpe_pad, aw["w_ukv"], aw["g_mk"])


def _attn_kernel(q_ref, k_ref, v_ref, x_ref, mod_ref, wo_ref, o_ref, oh_scr):
    for hd in range(N_HEADS):
        if hd < MLA_HEADS:
            kc, vc = hd * HEAD_PAD, hd * MLA_V
        else:
            kvh = (hd - MLA_HEADS) // GQA_REP
            kc, vc = (MLA_HEADS + kvh) * HEAD_PAD, MLA_HEADS * MLA_V + kvh * GQA_DIM
        q = q_ref[0, :, hd * HEAD_PAD:(hd + 1) * HEAD_PAD]
        k = k_ref[0, :, kc:kc + HEAD_PAD]
        v = v_ref[0, :, vc:vc + MLA_V]
        s = lax.dot_general(q, k, (((1,), (1,)), ((), ())), preferred_element_type=F32)
        p = jnp.exp(s - jnp.max(s, axis=-1, keepdims=True))
        den = jnp.sum(p, axis=-1, keepdims=True)
        o = jnp.dot(p.astype(BF16), v, preferred_element_type=F32) / den
        oh_scr[:, hd * MLA_V:(hd + 1) * MLA_V] = o.astype(BF16)
    gt = mod_ref[0][2:3]
    o_ref[0] = x_ref[0] + gt * jnp.dot(oh_scr[...], wo_ref[...], preferred_element_type=F32)


def _attention(q, k, v, x, mod_l, w_out, tq, mod_row):
    b, t, _ = q.shape
    s = k.shape[1]
    return pl.pallas_call(
        _attn_kernel,
        out_shape=jax.ShapeDtypeStruct((b, t, D_MODEL), F32),
        grid=(b, t // tq),
        in_specs=[
            pl.BlockSpec((1, tq, N_HEADS * HEAD_PAD), lambda bi, qi: (bi, qi, 0)),
            pl.BlockSpec((1, s, K_ALL), lambda bi, qi: (bi, 0, 0)),
            pl.BlockSpec((1, s, V_ALL), lambda bi, qi: (bi, 0, 0)),
            pl.BlockSpec((1, tq, D_MODEL), lambda bi, qi: (bi, qi, 0)),
            pl.BlockSpec((1, 6, D_MODEL), lambda bi, qi: (mod_row(bi), 0, 0)),
            pl.BlockSpec((D_MODEL, D_MODEL), lambda bi, qi: (0, 0)),
        ],
        out_specs=pl.BlockSpec((1, tq, D_MODEL), lambda bi, qi: (bi, qi, 0)),
        scratch_shapes=[pltpu.VMEM((tq, D_MODEL), BF16)],
        compiler_params=_cparams(("arbitrary", "arbitrary")),
        name="attention",
    )(q, k, v, x, mod_l, w_out)


def _cmul(ar, ai, br, bi):
    return ar * br - ai * bi, ar * bi + ai * br


def _ssm_ops_kernel(a_ref, dt_ref, bt_ref, c_ref, ops_ref, toep_ref, wct_ref, lam_ref,
                    cl_re_scr, cl_im_scr, op_scr, tp_scr, wc_scr):
    L, H, P = SSM_CHUNK, SSM_GROUP, SSM_STATE
    lane = lax.broadcasted_iota(jnp.int32, (H, CHUNK_W), 1)

    def one_group(g, carry):
        tp_scr[...] = jnp.zeros_like(tp_scr)
        for d in range(2):
            a = a_ref[g, d]
            a_re, a_im = a[0:1], a[1:2]
            dt = jnp.exp(dt_ref[g, d])
            mag = jnp.exp(a_re * dt)
            ang = a_im * dt
            ab_re, ab_im = mag * jnp.cos(ang), mag * jnp.sin(ang)
            den = a_re * a_re + a_im * a_im
            n_re, n_im = ab_re - 1.0, ab_im
            k_re = (n_re * a_re + n_im * a_im) / den
            k_im = (n_im * a_re - n_re * a_im) / den
            bt_re, bt_im = bt_ref[g, d, 0], bt_ref[g, d, 1]
            bb_re, bb_im = _cmul(k_re, k_im, bt_re, bt_im)
            c_re, c_im = c_ref[g, d, 0], c_ref[g, d, 1]

            pw = [(jnp.ones_like(ab_re), jnp.zeros_like(ab_im))]
            for _ in range(L):
                pw.append(_cmul(pw[-1][0], pw[-1][1], ab_re, ab_im))

            for e in range(L + 1):
                pr, pi = pw[e] if d == 0 else pw[L - e]
                cr, ci = _cmul(c_re, c_im, pr, pi)
                cl_re_scr[e * H:(e + 1) * H, :] = cr
                cl_im_scr[e * H:(e + 1) * H, :] = ci
            lo = 0 if d == 0 else H
            r = (lax.dot_general(bb_re, cl_re_scr[lo:lo + CHUNK_W, :], (((1,), (1,)), ((), ())),
                                 preferred_element_type=F32, precision=HIGHEST)
                 - lax.dot_general(bb_im, cl_im_scr[lo:lo + CHUNK_W, :], (((1,), (1,)), ((), ())),
                                   preferred_element_type=F32, precision=HIGHEST))
            wlo = H if d == 0 else 0
            wc_scr[:, d * STATE_W:d * STATE_W + P] = cl_re_scr[wlo:wlo + CHUNK_W, :]
            wc_scr[:, d * STATE_W + P:(d + 1) * STATE_W] = -cl_im_scr[wlo:wlo + CHUNK_W, :]
            for j in range(L):
                if d == 0:
                    blk = r if j == 0 else jnp.where(lane >= H * j, pltpu.roll(r, H * j, 1), 0.0)
                    sr, si = _cmul(pw[L - 1 - j][0], pw[L - 1 - j][1], bb_re, bb_im)
                else:
                    sft = (L - 1 - j) * H
                    blk = r if sft == 0 else jnp.where(lane < H * (j + 1), pltpu.roll(r, CHUNK_W - sft, 1), 0.0)
                    sr, si = _cmul(pw[j][0], pw[j][1], bb_re, bb_im)
                tp_scr[j * H:(j + 1) * H, :] += blk
                op_scr[j * H:(j + 1) * H, d * STATE_W:d * STATE_W + P] = sr
                op_scr[j * H:(j + 1) * H, d * STATE_W + P:(d + 1) * STATE_W] = si
            pr, pi = pw[L]
            lam_ref[g, 2 * d:2 * d + 1, 0:P] = pr
            lam_ref[g, 2 * d:2 * d + 1, P:STATE_W] = pr
            lam_ref[g, 2 * d + 1:2 * d + 2, 0:P] = -pi
            lam_ref[g, 2 * d + 1:2 * d + 2, P:STATE_W] = pi
        ops_ref[g] = op_scr[...].astype(BF16)
        toep_ref[g] = tp_scr[...].astype(BF16)
        wct_ref[g] = wc_scr[...].astype(BF16)
        return carry

    lax.fori_loop(0, a_ref.shape[0], one_group, 0)


def _ssm_ops(a, dt, bt, c):
    gb = GB_SSM
    G = SSM_GROUPS
    blk = lambda shape: pl.BlockSpec((gb,) + shape, lambda i: (i,) + (0,) * len(shape))
    return pl.pallas_call(
        _ssm_ops_kernel,
        out_shape=[jax.ShapeDtypeStruct((G, CHUNK_W, 2 * STATE_W), BF16),
                   jax.ShapeDtypeStruct((G, CHUNK_W, CHUNK_W), BF16),
                   jax.ShapeDtypeStruct((G, CHUNK_W, 2 * STATE_W), BF16),
                   jax.ShapeDtypeStruct((G, 4, STATE_W), F32)],
        grid=(G // gb,),
        in_specs=[blk((2, 2, SSM_STATE)), blk((2, 1, SSM_STATE)), blk((2, 2, SSM_GROUP, SSM_STATE)),
                  blk((2, 2, SSM_GROUP, SSM_STATE))],
        out_specs=[blk((CHUNK_W, 2 * STATE_W)), blk((CHUNK_W, CHUNK_W)), blk((CHUNK_W, 2 * STATE_W)),
                   blk((4, STATE_W))],
        scratch_shapes=[pltpu.VMEM(((SSM_CHUNK + 1) * SSM_GROUP, SSM_STATE), F32),
                        pltpu.VMEM(((SSM_CHUNK + 1) * SSM_GROUP, SSM_STATE), F32),
                        pltpu.VMEM((CHUNK_W, 2 * STATE_W), F32),
                        pltpu.VMEM((CHUNK_W, CHUNK_W), F32),
                        pltpu.VMEM((CHUNK_W, 2 * STATE_W), F32)],
        compiler_params=_cparams(("arbitrary",)),
        name="ssm_ops",
    )(a, dt, bt, c)


def _ssm_in_kernel(x_ref, mod_ref, g_ref, h_ref):
    m = mod_ref[0]
    h_ref[...] = (_rms(x_ref[...], g_ref[...]) * (1.0 + m[1:2]) + m[0:1]).astype(BF16)


def _ssm_in(x, mod_l, g1):
    tm = TM_SSM
    row = _mod_row(tm)
    return pl.pallas_call(
        _ssm_in_kernel,
        out_shape=jax.ShapeDtypeStruct((N_TOK, D_MODEL), BF16),
        grid=(N_TOK // tm,),
        in_specs=[pl.BlockSpec((tm, D_MODEL), lambda i: (i, 0)),
                  pl.BlockSpec((1, 6, D_MODEL), lambda i: (row(i), 0, 0)),
                  pl.BlockSpec((1, D_MODEL), lambda i: (0, 0))],
        out_specs=pl.BlockSpec((tm, D_MODEL), lambda i: (i, 0)),
        compiler_params=_cparams(("arbitrary",)),
        name="ssm_in",
    )(x, mod_l, g1.reshape(1, D_MODEL))


def _ssm_ends_kernel(u_ref, ops_ref, sf_ref, sb_ref):
    for g in range(u_ref.shape[0]):
        s = jnp.dot(u_ref[g], ops_ref[g], preferred_element_type=F32)
        sf_ref[:, g * STATE_W:(g + 1) * STATE_W] = s[:, 0:STATE_W]
        sb_ref[:, g * STATE_W:(g + 1) * STATE_W] = s[:, STATE_W:]


def _ssm_ends(u, ops):
    gb = GB_SSM
    G = SSM_GROUPS
    return pl.pallas_call(
        _ssm_ends_kernel,
        out_shape=[jax.ShapeDtypeStruct((N_CHUNK_ROWS, G * STATE_W), F32)] * 2,
        grid=(G // gb,),
        in_specs=[pl.BlockSpec((gb, N_CHUNK_ROWS, CHUNK_W), lambda i: (i, 0, 0)),
                  pl.BlockSpec((gb, CHUNK_W, 2 * STATE_W), lambda i: (i, 0, 0))],
        out_specs=[pl.BlockSpec((N_CHUNK_ROWS, gb * STATE_W), lambda i: (0, i))] * 2,
        compiler_params=_cparams(("arbitrary",)),
        name="ssm_chunk_ends",
    )(u, ops)


def _ssm_carry_kernel(sf_ref, sb_ref, lamf_ref, lamb_ref, hf_ref, hb_ref, xf_ref, xb_ref, ff_ref, fb_ref,
                      *, n_steps, rows):
    half = SSM_STATE

    def swap(t):
        lane = lax.broadcasted_iota(jnp.int32, t.shape, 1)
        return jnp.where((lane & half) == 0, pltpu.roll(t, t.shape[1] - half, 1), pltpu.roll(t, half, 1))

    def lam(ref):
        a1, a2 = ref[0], ref[1]
        return a1, a2

    a1f, a2f = lam(lamf_ref)
    a1b, a2b = lam(lamb_ref)
    x0f, x0b = hf_ref[...], hb_ref[...]

    def body(c, carry):
        xf, xfs, xb, xbs = carry
        rf = pl.multiple_of(c * rows, rows)
        rb = pl.multiple_of((n_steps - 1 - c) * rows, rows)
        sf = sf_ref[pl.ds(rf, rows), :]
        sb = sb_ref[pl.ds(rb, rows), :]
        xf_ref[pl.ds(rf, rows), :] = xf
        xb_ref[pl.ds(rb, rows), :] = xb
        nxf = a1f * xf + a2f * xfs + sf
        nxfs = a1f * xfs - a2f * xf + swap(sf)
        nxb = a1b * xb + a2b * xbs + sb
        nxbs = a1b * xbs - a2b * xb + swap(sb)
        return nxf, nxfs, nxb, nxbs

    xf, _, xb, _ = lax.fori_loop(0, n_steps, body, (x0f, swap(x0f), x0b, swap(x0b)))
    ff_ref[...] = xf
    fb_ref[...] = xb


def _ssm_carry(sf, sb, lamf, lamb, h0f, h0b, n_steps, rows, lane_block):
    width = sf.shape[1]
    st = pl.BlockSpec((n_steps * rows, lane_block), lambda i: (0, i))
    rw = pl.BlockSpec((rows, lane_block), lambda i: (0, i))
    lm = pl.BlockSpec((2, rows, lane_block), lambda i: (0, 0, i))
    return pl.pallas_call(
        functools.partial(_ssm_carry_kernel, n_steps=n_steps, rows=rows),
        out_shape=[jax.ShapeDtypeStruct(sf.shape, F32)] * 2 + [jax.ShapeDtypeStruct((rows, width), F32)] * 2,
        grid=(width // lane_block,),
        in_specs=[st, st, lm, lm, rw, rw],
        out_specs=[st, st, rw, rw],
        compiler_params=_cparams(("arbitrary",)),
        name="ssm_carry",
    )(sf, sb, lamf, lamb, h0f, h0b)


def _ssm_out_kernel(u_ref, toep_ref, xf_ref, xb_ref, wct_ref, y_ref):
    for g in range(u_ref.shape[0]):
        y = jnp.dot(u_ref[g], toep_ref[g], preferred_element_type=F32)
        for d, x_ref in enumerate((xf_ref, xb_ref)):
            xin = x_ref[:, g * STATE_W:(g + 1) * STATE_W].astype(BF16)
            w = wct_ref[g, :, d * STATE_W:(d + 1) * STATE_W]
            y += lax.dot_general(xin, w, (((1,), (1,)), ((), ())), preferred_element_type=F32)
        y_ref[g] = y


def _ssm_out(u, toep, xf, xb, wct):
    gb = GB_SSM
    G = SSM_GROUPS
    return pl.pallas_call(
        _ssm_out_kernel,
        out_shape=jax.ShapeDtypeStruct((G, N_CHUNK_ROWS, CHUNK_W), F32),
        grid=(G // gb,),
        in_specs=[pl.BlockSpec((gb, N_CHUNK_ROWS, CHUNK_W), lambda i: (i, 0, 0)),
                  pl.BlockSpec((gb, CHUNK_W, CHUNK_W), lambda i: (i, 0, 0)),
                  pl.BlockSpec((N_CHUNK_ROWS, gb * STATE_W), lambda i: (0, i)),
                  pl.BlockSpec((N_CHUNK_ROWS, gb * STATE_W), lambda i: (0, i)),
                  pl.BlockSpec((gb, CHUNK_W, 2 * STATE_W), lambda i: (i, 0, 0))],
        out_specs=pl.BlockSpec((gb, N_CHUNK_ROWS, CHUNK_W), lambda i: (i, 0, 0)),
        compiler_params=_cparams(("arbitrary",)),
        name="ssm_chunk_out",
    )(u, toep, xf, xb, wct)


def _ssm_post_kernel(x_ref, mod_ref, g_ref, y_ref, d_ref, w_ref, b_ref, o_ref):
    m = mod_ref[0]
    x = x_ref[...]
    h = _rms(x, g_ref[...]) * (1.0 + m[1:2]) + m[0:1]
    y = h * d_ref[...] + y_ref[...]
    gl = jax.nn.gelu(y).astype(BF16)
    z = jnp.dot(gl, w_ref[...], preferred_element_type=F32) + b_ref[...]
    o_ref[...] = x + m[2:3] * (z[:, :D_MODEL] * jax.nn.sigmoid(z[:, D_MODEL:]))


def _ssm_post(x, mod_l, g1, y, d_skip, w_glu, b_glu):
    tm = TM_SSM
    row = _mod_row(tm)
    full = lambda a: pl.BlockSpec(a.shape, lambda i: (0,) * a.ndim)
    d2, b2 = d_skip.reshape(1, D_MODEL), b_glu.reshape(1, 2 * D_MODEL)
    g2 = g1.reshape(1, D_MODEL)
    return pl.pallas_call(
        _ssm_post_kernel,
        out_shape=jax.ShapeDtypeStruct((N_TOK, D_MODEL), F32),
        grid=(N_TOK // tm,),
        in_specs=[pl.BlockSpec((tm, D_MODEL), lambda i: (i, 0)),
                  pl.BlockSpec((1, 6, D_MODEL), lambda i: (row(i), 0, 0)),
                  full(g2), pl.BlockSpec((tm, D_MODEL), lambda i: (i, 0)), full(d2), full(w_glu), full(b2)],
        out_specs=pl.BlockSpec((tm, D_MODEL), lambda i: (i, 0)),
        compiler_params=_cparams(("arbitrary",)),
        name="ssm_post",
    )(x, mod_l, g2, y, d2, w_glu, b2)


def _to_chunks(h):
    L, G, H = SSM_CHUNK, SSM_GROUPS, SSM_GROUP
    hp = h[:N_PROMPT].reshape(BATCH, SEQ // L, L, G, H).transpose(3, 1, 0, 2, 4).reshape(G, -1, CHUNK_W)
    hs = h[N_PROMPT:].reshape(DEC_BATCH, DEC_SEQ // L, L, G, H).transpose(3, 1, 0, 2, 4).reshape(G, -1, CHUNK_W)
    return jnp.concatenate([hp, hs], axis=1)


def _from_chunks(y):
    L, G, H = SSM_CHUNK, SSM_GROUPS, SSM_GROUP
    yp = y[:, :N_CHUNK_PROMPT].reshape(G, SEQ // L, BATCH, L, H).transpose(2, 1, 3, 0, 4).reshape(N_PROMPT, D_MODEL)
    ys = y[:, N_CHUNK_PROMPT:].reshape(G, DEC_SEQ // L, DEC_BATCH, L, H).transpose(2, 1, 3, 0, 4)
    return jnp.concatenate([yp, ys.reshape(N_SAMPLE, D_MODEL)], axis=0)


def _rope_tables():
    rows = DEC_SEQ // GRID_W
    t = np.arange(DEC_SEQ)
    row, col = (t // GRID_W).astype(np.float32), (t % GRID_W).astype(np.float32)

    def table(rot_dim, lane0):
        n_freq = rot_dim // 4
        inv_freq = jnp.asarray(ROPE_THETA, F32) ** (-jnp.arange(n_freq, dtype=F32) / n_freq)
        ang_row = jnp.asarray(row)[:, None] * inv_freq
        ang_col = jnp.asarray(col)[:, None] * inv_freq
        ang = jnp.concatenate([ang_row, ang_row, ang_col, ang_col], axis=1)
        sign = jnp.tile(jnp.concatenate([-jnp.ones(n_freq, F32), jnp.ones(n_freq, F32)]), 2)
        cos = jnp.ones((DEC_SEQ, HEAD_PAD), F32).at[:, lane0:lane0 + rot_dim].set(jnp.cos(ang))
        sin = jnp.zeros((DEC_SEQ, HEAD_PAD), F32).at[:, lane0:lane0 + rot_dim].set(jnp.sin(ang) * sign)
        cos = jnp.concatenate([jnp.ones((N_PROMPT, HEAD_PAD), F32), jnp.tile(cos, (DEC_BATCH, 1))], axis=0)
        sin = jnp.concatenate([jnp.zeros((N_PROMPT, HEAD_PAD), F32), jnp.tile(sin, (DEC_BATCH, 1))], axis=0)
        return cos, sin

    assert rows * GRID_W == DEC_SEQ
    return table(MLA_ROPE, MLA_NOPE) + table(GQA_DIM, 0)


def _pad_heads(w, n_heads, dim):
    lead = w.shape[:-1]
    w = w.reshape(lead + (n_heads, dim))
    w = jnp.pad(w, [(0, 0)] * len(lead) + [(0, 0), (0, HEAD_PAD - dim)])
    return w.reshape(lead + (n_heads * HEAD_PAD,))


def _attn_weights(w_in, g_qa, g_kva, w_uq, w_ukv, g_mq, g_mk, g_gq, g_gk, w_out):
    o1, o2, o3 = Q_LORA, Q_LORA + KV_LORA, Q_LORA + KV_LORA + MLA_ROPE
    o4 = o3 + GQA_HEADS * GQA_DIM
    o5 = o4 + GQA_KV_HEADS * GQA_DIM
    kpe = jnp.pad(w_in[:, o2:o3], ((0, 0), (MLA_NOPE, HEAD_PAD - MLA_QK)))
    w_in_ext = jnp.concatenate([w_in[:, :o2], kpe, _pad_heads(w_in[:, o3:o4], GQA_HEADS, GQA_DIM),
                                _pad_heads(w_in[:, o4:o5], GQA_KV_HEADS, GQA_DIM), w_in[:, o5:]], axis=1)
    ukv = w_ukv.reshape(KV_LORA, MLA_HEADS, MLA_NOPE + MLA_V)
    w_ukv_perm = jnp.concatenate([_pad_heads(ukv[:, :, :MLA_NOPE].reshape(KV_LORA, -1), MLA_HEADS, MLA_NOPE),
                                  ukv[:, :, MLA_NOPE:].reshape(KV_LORA, -1)], axis=1)
    pad_g = lambda g: jnp.pad(g, (0, HEAD_PAD - g.shape[0])).reshape(1, HEAD_PAD)
    return dict(w_in=w_in_ext.astype(BF16), g_qa=g_qa.reshape(1, -1), g_kva=g_kva.reshape(1, -1),
                w_uq=_pad_heads(w_uq, MLA_HEADS, MLA_QK).astype(BF16), w_ukv=w_ukv_perm.astype(BF16),
                g_mq=pad_g(g_mq), g_mk=pad_g(g_mk), g_gq=pad_g(g_gq), g_gk=pad_g(g_gk),
                w_out=w_out.astype(BF16))


def _attn_layer(x, mod_l, g1, aw, tabs, cache):
    q, k, v, ckv, kpe, gk, gv = _attn_pre(x, mod_l, g1, aw, tabs)
    c_ckv, c_krope, c_gk, c_gv = cache
    n_c = DEC_BATCH * PAST_LEN
    kpe_c = jnp.pad(c_krope.reshape(n_c, MLA_ROPE), ((0, 0), (MLA_NOPE, HEAD_PAD - MLA_QK)))
    km_c, vm_c = _cache_kv(c_ckv.reshape(n_c, KV_LORA), kpe_c, aw)
    k_c = jnp.concatenate([km_c, _pad_heads(c_gk.reshape(n_c, -1), GQA_KV_HEADS, GQA_DIM).astype(BF16)], axis=1)
    v_c = jnp.concatenate([vm_c, c_gv.reshape(n_c, -1).astype(BF16)], axis=1)
    k_s = jnp.concatenate([k_c.reshape(DEC_BATCH, PAST_LEN, K_ALL),
                           k[N_PROMPT:].reshape(DEC_BATCH, DEC_SEQ, K_ALL)], axis=1)
    v_s = jnp.concatenate([v_c.reshape(DEC_BATCH, PAST_LEN, V_ALL),
                           v[N_PROMPT:].reshape(DEC_BATCH, DEC_SEQ, V_ALL)], axis=1)
    qw = N_HEADS * HEAD_PAD
    xp = _attention(q[:N_PROMPT].reshape(BATCH, SEQ, qw), k[:N_PROMPT].reshape(BATCH, SEQ, K_ALL),
                    v[:N_PROMPT].reshape(BATCH, SEQ, V_ALL), x[:N_PROMPT].reshape(BATCH, SEQ, D_MODEL),
                    mod_l, aw["w_out"], SEQ, lambda b: 0)
    xs = _attention(q[N_PROMPT:].reshape(DEC_BATCH, DEC_SEQ, qw), k_s, v_s,
                    x[N_PROMPT:].reshape(DEC_BATCH, DEC_SEQ, D_MODEL), mod_l, aw["w_out"], TQ_SAMPLE,
                    lambda b: 1 + b)
    x = jnp.concatenate([xp.reshape(N_PROMPT, D_MODEL), xs.reshape(N_SAMPLE, D_MODEL)], axis=0)
    new = (ckv[:N_PROMPT].reshape(BATCH, SEQ, KV_LORA),
           kpe[:N_PROMPT, MLA_NOPE:MLA_QK].reshape(BATCH, SEQ, MLA_ROPE),
           gk[:N_PROMPT].reshape(BATCH, SEQ, GQA_KV_HEADS, HEAD_PAD)[..., :GQA_DIM],
           gv[:N_PROMPT].reshape(BATCH, SEQ, GQA_KV_HEADS, GQA_DIM))
    return x, new


def _ssm_layer(x, mod_l, g1, a_re, a_im, log_dt, b_re, b_im, c_re, c_im, d_skip, w_glu, b_glu, state0):
    G, P, H = SSM_GROUPS, SSM_STATE, SSM_GROUP
    a = jnp.stack([a_re, a_im], axis=2).transpose(1, 0, 2, 3)
    dt = jnp.broadcast_to(log_dt.transpose(1, 0)[:, :, None, None], (G, 2, 1, P))
    bt = jnp.stack([b_re, b_im], axis=2).transpose(1, 0, 2, 4, 3)
    c = jnp.stack([c_re, c_im], axis=2).transpose(1, 0, 2, 3, 4)
    ops, toep, wct, lam = _ssm_ops(a, dt, bt, c)

    u = _to_chunks(_ssm_in(x, mod_l, g1))
    sf, sb = _ssm_ends(u, ops)

    lam_row = lam.transpose(1, 0, 2).reshape(4, 1, G * STATE_W)
    rp = BATCH
    lam_p = jnp.broadcast_to(lam_row, (4, rp, G * STATE_W))
    zero = jnp.zeros((rp, G * STATE_W), F32)
    xf_p, xb_p, fin_f, fin_b = _ssm_carry(sf[:N_CHUNK_PROMPT], sb[:N_CHUNK_PROMPT], lam_p[0:2], lam_p[2:4],
                                          zero, zero, SEQ // SSM_CHUNK, rp, 1024)
    nq = 8 // DEC_BATCH
    ws = G * STATE_W // nq
    n_s = DEC_SEQ // SSM_CHUNK
    lam_s = jnp.broadcast_to(lam_row.reshape(4, 1, nq, ws), (4, DEC_BATCH, nq, ws)).reshape(4, 8, ws)
    h0 = state0.transpose(1, 0, 2, 4, 3).reshape(2, DEC_BATCH, nq, ws).reshape(2, 8, ws)
    xf_s, xb_s, _, _ = _ssm_carry(sf[N_CHUNK_PROMPT:].reshape(n_s * 8, ws), sb[N_CHUNK_PROMPT:].reshape(n_s * 8, ws),
                                  lam_s[0:2], lam_s[2:4], h0[0], h0[1], n_s, 8, 1024)
    xf = jnp.concatenate([xf_p, xf_s.reshape(-1, G * STATE_W)], axis=0)
    xb = jnp.concatenate([xb_p, xb_s.reshape(-1, G * STATE_W)], axis=0)

    y = _from_chunks(_ssm_out(u, toep, xf, xb, wct))
    x = _ssm_post(x, mod_l, g1, y, d_skip, w_glu.astype(BF16), b_glu)
    fin = jnp.stack([fin_f, fin_b], axis=1).reshape(BATCH, 2, G, 2, P).transpose(0, 1, 2, 4, 3)
    return x, fin


def kernel(x_prompt, x_sample, c, cache_mla_ckv, cache_mla_krope, cache_gqa_k, cache_gqa_v, state_ssm, c_ctx,
           norm1_g, norm2_g, w_mod, b_mod,
           attn_w_in, attn_qa_norm_g, attn_kva_norm_g, attn_w_uq, attn_w_ukv,
           attn_mla_q_norm_g, attn_mla_k_norm_g, attn_gqa_q_norm_g, attn_gqa_k_norm_g, attn_w_out,
           ssm_a_re, ssm_a_im, ssm_log_dt, ssm_b_re, ssm_b_im, ssm_c_re, ssm_c_im, ssm_d, ssm_w_glu, ssm_b_glu,
           ffn_w_up, ffn_conv_w, ffn_conv_b, ffn_w_down):
    x = jnp.concatenate([x_prompt.reshape(N_PROMPT, D_MODEL), x_sample.reshape(N_SAMPLE, D_MODEL)], axis=0)
    cond8 = jnp.concatenate([c_ctx[None, :], c, jnp.zeros((8 - 1 - DEC_BATCH, D_MODEL), F32)], axis=0)
    mod = _modulation(cond8, w_mod, b_mod).reshape(DEPTH, 8, 6, D_MODEL)
    tabs = _rope_tables()

    new_attn, new_ssm = [], []
    for i in range(DEPTH):
        j = i // 2
        if i % 2 == 0:
            aw = _attn_weights(attn_w_in[j], attn_qa_norm_g[j], attn_kva_norm_g[j], attn_w_uq[j], attn_w_ukv[j],
                               attn_mla_q_norm_g[j], attn_mla_k_norm_g[j], attn_gqa_q_norm_g[j],
                               attn_gqa_k_norm_g[j], attn_w_out[j])
            cache = (cache_mla_ckv[:, j], cache_mla_krope[:, j], cache_gqa_k[:, j], cache_gqa_v[:, j])
            x, new = _attn_layer(x, mod[i], norm1_g[i], aw, tabs, cache)
            new_attn.append(new)
        else:
            x, fin = _ssm_layer(x, mod[i], norm1_g[i], ssm_a_re[j], ssm_a_im[j], ssm_log_dt[j], ssm_b_re[j],
                                ssm_b_im[j], ssm_c_re[j], ssm_c_im[j], ssm_d[j], ssm_w_glu[j], ssm_b_glu[j],
                                state_ssm[:, j])
            new_ssm.append(fin)
        x = _conv_ffn(x, mod[i], norm2_g[i], ffn_w_up[i], ffn_conv_w[i], ffn_conv_b[i], ffn_w_down[i])

    outs = [jnp.stack([n[k] for n in new_attn], axis=1) for k in range(4)]
    return (x[:N_PROMPT].reshape(BATCH, SEQ, D_MODEL), x[N_PROMPT:].reshape(DEC_BATCH, DEC_SEQ, D_MODEL),
            outs[0], outs[1], outs[2], outs[3], jnp.stack(new_ssm, axis=1))
```

```python
import functools
import math

import jax
import jax.numpy as jnp
import numpy as np
from jax import lax
from jax.experimental import pallas as pl
from jax.experimental.pallas import tpu as pltpu

F32 = jnp.float32
BF16 = jnp.bfloat16
HIGHEST = lax.Precision.HIGHEST

LANES = 128
BF16_SUBLANES = 16
VMEM_LIMIT_BYTES = 56 * 1024 * 1024

D_MODEL = 1024
BATCH = 16
SEQ = 256
DEPTH = 4
DEC_BATCH = 2
DEC_SEQ = 2048
PAST_LEN = 256
GRID_W = 64
MLA_HEADS = 8
Q_LORA = 384
KV_LORA = 256
MLA_NOPE = 64
MLA_ROPE = 32
MLA_QK = MLA_NOPE + MLA_ROPE
MLA_V = 64
GQA_HEADS = 8
GQA_KV_HEADS = 2
GQA_DIM = 64
GQA_REP = GQA_HEADS // GQA_KV_HEADS
ROPE_THETA = 10000.0
SSM_GROUP = 16
SSM_GROUPS = D_MODEL // SSM_GROUP
SSM_STATE = 64
D_FF = 2816
EPS = 1e-6

N_PROMPT = BATCH * SEQ
N_SAMPLE = DEC_BATCH * DEC_SEQ
N_TOK = N_PROMPT + N_SAMPLE
N_HEADS = MLA_HEADS + GQA_HEADS
HEAD_PAD = LANES
KV_LEN_SAMPLE = PAST_LEN + DEC_SEQ

IN_Q = 0
IN_CKV = IN_Q + Q_LORA
IN_KPE = IN_CKV + KV_LORA
IN_GQ = IN_KPE + HEAD_PAD
IN_GK = IN_GQ + GQA_HEADS * HEAD_PAD
IN_GV = IN_GK + GQA_KV_HEADS * HEAD_PAD
IN_EXT = IN_GV + GQA_KV_HEADS * GQA_DIM

K_ALL = (MLA_HEADS + GQA_KV_HEADS) * HEAD_PAD
V_ALL = MLA_HEADS * MLA_V + GQA_KV_HEADS * GQA_DIM

SSM_CHUNK = 16
CHUNK_W = SSM_CHUNK * SSM_GROUP
N_CHUNK_ROWS = N_TOK // SSM_CHUNK
N_CHUNK_PROMPT = N_PROMPT // SSM_CHUNK
STATE_W = 2 * SSM_STATE

TM_FFN = 512
TN_FFN = 256
FFN_HALO = BF16_SUBLANES
TM_PRE = 512
TQ_SAMPLE = 256
TM_SSM = 1024
TM_SSM_IN = 512
OUT_ROW_BLOCK = 32
GB_CARRY = 4
TN_MOD = 1536
GB_SSM = 8


def _cparams(sem):
    return pltpu.CompilerParams(dimension_semantics=sem, vmem_limit_bytes=VMEM_LIMIT_BYTES)


def _rms(x, g):
    return x * lax.rsqrt(jnp.mean(x * x, axis=-1, keepdims=True) + EPS) * g


def _mod_row(tm):
    n_prompt_tiles = N_PROMPT // tm
    tiles_per_seq = DEC_SEQ // tm

    def row(i):
        return jnp.where(i < n_prompt_tiles, 0, 1 + (i - n_prompt_tiles) // tiles_per_seq)

    return row


def _mod_kernel(cond_ref, w_ref, b_ref, o_ref):
    a = jax.nn.silu(cond_ref[...])
    o_ref[0] = jnp.dot(a, w_ref[0], preferred_element_type=F32, precision=HIGHEST) + b_ref[0]


def _modulation(cond8, w_mod, b_mod):
    n = w_mod.shape[-1]
    return pl.pallas_call(
        _mod_kernel,
        out_shape=jax.ShapeDtypeStruct((DEPTH, 8, n), F32),
        grid=(DEPTH, n // TN_MOD),
        in_specs=[
            pl.BlockSpec((8, D_MODEL), lambda l, j: (0, 0)),
            pl.BlockSpec((1, D_MODEL, TN_MOD), lambda l, j: (l, 0, j)),
            pl.BlockSpec((1, 1, TN_MOD), lambda l, j: (l, 0, j)),
        ],
        out_specs=pl.BlockSpec((1, 8, TN_MOD), lambda l, j: (l, 0, j)),
        compiler_params=_cparams(("arbitrary", "arbitrary")),
        name="modulation",
    )(cond8, w_mod, b_mod.reshape(DEPTH, 1, n))


def _ffn_kernel(xp_ref, x_ref, xn_ref, mod_ref, g_ref, wup_ref, cw_ref, cb_ref, wd_ref, o_ref, h_scr, act_scr):
    i = pl.program_id(0)
    tm = x_ref.shape[0]
    tn = TN_FFN
    m = mod_ref[0]
    sh, sc, gt = m[3:4], m[4:5], m[5:6]
    g = g_ref[...]

    def hmod(x):
        return (_rms(x, g) * (1.0 + sc) + sh).astype(BF16)

    h_scr[0:FFN_HALO] = hmod(xp_ref[...])
    h_scr[FFN_HALO:FFN_HALO + tm] = hmod(x_ref[...])
    h_scr[FFN_HALO + tm:] = hmod(xn_ref[...])
    rows = tm + 2 * FFN_HALO
    seq_len = jnp.where(i * tm < N_PROMPT, SEQ, DEC_SEQ)
    pos = (i * tm + lax.broadcasted_iota(jnp.int32, (tm, 1), 0)) & (seq_len - 1)
    first = pos == 0
    last = pos == seq_len - 1

    def conv(z, col):
        cw = cw_ref[:, col:col + tn]
        zp = pltpu.roll(z, 1, 0)[FFN_HALO:FFN_HALO + tm]
        zn = pltpu.roll(z, rows - 1, 0)[FFN_HALO:FFN_HALO + tm]
        zc = z[FFN_HALO:FFN_HALO + tm]
        return (jnp.where(first, 0.0, zp) * cw[0:1] + zc * cw[1:2] + jnp.where(last, 0.0, zn) * cw[2:3]
                + cb_ref[:, col:col + tn])

    h = h_scr[...]
    for jc in range(D_FF // tn):
        ca, cb = jc * tn, D_FF + jc * tn
        za = jnp.dot(h, wup_ref[:, ca:ca + tn], preferred_element_type=F32)
        zb = jnp.dot(h, wup_ref[:, cb:cb + tn], preferred_element_type=F32)
        act_scr[:, ca:ca + tn] = (jax.nn.silu(conv(za, ca)) * conv(zb, cb)).astype(BF16)
    o_ref[...] = x_ref[...] + gt * jnp.dot(act_scr[...], wd_ref[...], preferred_element_type=F32)


def _conv_ffn(x, mod_l, g, w_up, conv_w, conv_b, w_down):
    tm, halo = TM_FFN, FFN_HALO
    row = _mod_row(tm)
    n_halo_blocks = N_TOK // halo
    resident = lambda a: pl.BlockSpec(a.shape, lambda i: (0,) * a.ndim, pipeline_mode=pl.Buffered(1))
    cb2 = conv_b.reshape(1, -1)
    return pl.pallas_call(
        _ffn_kernel,
        out_shape=jax.ShapeDtypeStruct((N_TOK, D_MODEL), F32),
        grid=(N_TOK // tm,),
        in_specs=[
            pl.BlockSpec((halo, D_MODEL), lambda i: (jnp.maximum(i * (tm // halo) - 1, 0), 0)),
            pl.BlockSpec((tm, D_MODEL), lambda i: (i, 0)),
            pl.BlockSpec((halo, D_MODEL), lambda i: (jnp.minimum((i + 1) * (tm // halo), n_halo_blocks - 1), 0)),
            pl.BlockSpec((1, 6, D_MODEL), lambda i: (row(i), 0, 0)),
            pl.BlockSpec((1, D_MODEL), lambda i: (0, 0)),
            resident(w_up), resident(conv_w), resident(cb2), resident(w_down),
        ],
        out_specs=pl.BlockSpec((tm, D_MODEL), lambda i: (i, 0)),
        scratch_shapes=[pltpu.VMEM((tm + 2 * halo, D_MODEL), BF16), pltpu.VMEM((tm, D_FF), BF16)],
        compiler_params=_cparams(("arbitrary",)),
        name="conv_ffn",
    )(x, x, x, mod_l, g.reshape(1, D_MODEL), w_up, conv_w, cb2, w_down)


def _rope(x, cos, sin, half):
    lane = lax.broadcasted_iota(jnp.int32, x.shape, 1)
    first = ((lane // half) & 1) == 0
    partner = jnp.where(first, pltpu.roll(x, LANES - half, 1), pltpu.roll(x, half, 1))
    return x * cos + partner * sin


def _head_norm(xh, g, dim):
    return xh * lax.rsqrt(jnp.sum(xh * xh, axis=-1, keepdims=True) * (1.0 / dim) + EPS) * g


def _mla_kv_heads(ckv_bf, kpe, w_ukv_ref, g_mk, cos_m, sin_m, k_ref, v_ref):
    kv = jnp.dot(ckv_bf, w_ukv_ref[...], preferred_element_type=F32)
    for h in range(MLA_HEADS):
        kh = kv[:, h * HEAD_PAD:(h + 1) * HEAD_PAD] + kpe
        kh = _head_norm(kh, g_mk, MLA_QK)
        if cos_m is not None:
            kh = _rope(kh, cos_m, sin_m, MLA_ROPE // 4)
        k_ref[:, h * HEAD_PAD:(h + 1) * HEAD_PAD] = kh.astype(BF16)
    v_ref[:, 0:MLA_HEADS * MLA_V] = kv[:, MLA_HEADS * HEAD_PAD:].astype(BF16)


def _attn_pre_kernel(x_ref, mod_ref, g1_ref, w_in_ref, g_qa_ref, g_kva_ref, w_uq_ref, w_ukv_ref,
                     g_mq_ref, g_mk_ref, g_gq_ref, g_gk_ref, cos_m_ref, sin_m_ref, cos_g_ref, sin_g_ref,
                     q_ref, k_ref, v_ref, ckv_ref, kpe_ref, gk_ref, gv_ref):
    m = mod_ref[0]
    sh, sc = m[0:1], m[1:2]
    h = (_rms(x_ref[...], g1_ref[...]) * (1.0 + sc) + sh).astype(BF16)
    p = jnp.dot(h, w_in_ref[...], preferred_element_type=F32)
    cos_m, sin_m = cos_m_ref[...], sin_m_ref[...]
    cos_g, sin_g = cos_g_ref[...], sin_g_ref[...]

    q_c = _rms(p[:, IN_Q:IN_CKV], g_qa_ref[...]).astype(BF16)
    ckv = _rms(p[:, IN_CKV:IN_KPE], g_kva_ref[...])
    kpe = p[:, IN_KPE:IN_GQ]
    ckv_ref[...] = ckv
    kpe_ref[...] = kpe

    qm = jnp.dot(q_c, w_uq_ref[...], preferred_element_type=F32)
    g_mq = g_mq_ref[...]
    for hd in range(MLA_HEADS):
        qh = _head_norm(qm[:, hd * HEAD_PAD:(hd + 1) * HEAD_PAD], g_mq, MLA_QK)
        qh = _rope(qh, cos_m, sin_m, MLA_ROPE // 4) * (1.0 / math.sqrt(MLA_QK))
        q_ref[:, hd * HEAD_PAD:(hd + 1) * HEAD_PAD] = qh.astype(BF16)

    _mla_kv_heads(ckv.astype(BF16), kpe, w_ukv_ref, g_mk_ref[...], cos_m, sin_m, k_ref, v_ref)

    g_gq = g_gq_ref[...]
    for hd in range(GQA_HEADS):
        qh = _head_norm(p[:, IN_GQ + hd * HEAD_PAD:IN_GQ + (hd + 1) * HEAD_PAD], g_gq, GQA_DIM)
        qh = _rope(qh, cos_g, sin_g, GQA_DIM // 4) * (1.0 / math.sqrt(GQA_DIM))
        q_ref[:, (MLA_HEADS + hd) * HEAD_PAD:(MLA_HEADS + hd + 1) * HEAD_PAD] = qh.astype(BF16)
    g_gk = g_gk_ref[...]
    for hd in range(GQA_KV_HEADS):
        kh = _head_norm(p[:, IN_GK + hd * HEAD_PAD:IN_GK + (hd + 1) * HEAD_PAD], g_gk, GQA_DIM)
        gk_ref[:, hd * HEAD_PAD:(hd + 1) * HEAD_PAD] = kh
        kh = _rope(kh, cos_g, sin_g, GQA_DIM // 4)
        k_ref[:, (MLA_HEADS + hd) * HEAD_PAD:(MLA_HEADS + hd + 1) * HEAD_PAD] = kh.astype(BF16)
    gv = p[:, IN_GV:IN_EXT]
    gv_ref[...] = gv
    v_ref[:, MLA_HEADS * MLA_V:] = gv.astype(BF16)


def _attn_pre(x, mod_l, g1, aw, tabs):
    tm = TM_PRE
    row = _mod_row(tm)
    full = lambda a: pl.BlockSpec(a.shape, lambda i: (0,) * a.ndim)
    tok = lambda w: pl.BlockSpec((tm, w), lambda i: (i, 0))
    consts = [g1.reshape(1, D_MODEL), aw["w_in"], aw["g_qa"], aw["g_kva"], aw["w_uq"], aw["w_ukv"],
              aw["g_mq"], aw["g_mk"], aw["g_gq"], aw["g_gk"]]
    return pl.pallas_call(
        _attn_pre_kernel,
        out_shape=[
            jax.ShapeDtypeStruct((N_TOK, N_HEADS * HEAD_PAD), BF16),
            jax.ShapeDtypeStruct((N_TOK, K_ALL), BF16),
            jax.ShapeDtypeStruct((N_TOK, V_ALL), BF16),
            jax.ShapeDtypeStruct((N_TOK, KV_LORA), F32),
            jax.ShapeDtypeStruct((N_TOK, HEAD_PAD), F32),
            jax.ShapeDtypeStruct((N_TOK, GQA_KV_HEADS * HEAD_PAD), F32),
            jax.ShapeDtypeStruct((N_TOK, GQA_KV_HEADS * GQA_DIM), F32),
        ],
        grid=(N_TOK // tm,),
        in_specs=[tok(D_MODEL), pl.BlockSpec((1, 6, D_MODEL), lambda i: (row(i), 0, 0))]
        + [full(a) for a in consts] + [tok(HEAD_PAD)] * 4,
        out_specs=[tok(N_HEADS * HEAD_PAD), tok(K_ALL), tok(V_ALL), tok(KV_LORA), tok(HEAD_PAD),
                   tok(GQA_KV_HEADS * HEAD_PAD), tok(GQA_KV_HEADS * GQA_DIM)],
        compiler_params=_cparams(("arbitrary",)),
        name="attn_pre",
    )(x, mod_l, *consts, *tabs)


def _cache_kv_kernel(ckv_ref, kpe_ref, w_ukv_ref, g_mk_ref, k_ref, v_ref):
    _mla_kv_heads(ckv_ref[...].astype(BF16), kpe_ref[...], w_ukv_ref, g_mk_ref[...], None, None, k_ref, v_ref)


def _cache_kv(ckv, kpe_pad, aw):
    n = ckv.shape[0]
    return pl.pallas_call(
        _cache_kv_kernel,
        out_shape=[jax.ShapeDtypeStruct((n, MLA_HEADS * HEAD_PAD), BF16),
                   jax.ShapeDtypeStruct((n, MLA_HEADS * MLA_V), BF16)],
        name="cache_kv",
    )(ckv, kpe_pad, aw["w_ukv"], aw["g_mk"])


def _attn_kernel(q_ref, k_ref, v_ref, x_ref, mod_ref, wo_ref, o_ref, oh_scr):
    for hd in range(N_HEADS):
        if hd < MLA_HEADS:
            kc, vc = hd * HEAD_PAD, hd * MLA_V
        else:
            kvh = (hd - MLA_HEADS) // GQA_REP
            kc, vc = (MLA_HEADS + kvh) * HEAD_PAD, MLA_HEADS * MLA_V + kvh * GQA_DIM
        q = q_ref[0, :, hd * HEAD_PAD:(hd + 1) * HEAD_PAD]
        k = k_ref[0, :, kc:kc + HEAD_PAD]
        v = v_ref[0, :, vc:vc + MLA_V]
        s = lax.dot_general(q, k, (((1,), (1,)), ((), ())), preferred_element_type=F32)
        p = jnp.exp(s - jnp.max(s, axis=-1, keepdims=True))
        den = jnp.sum(p, axis=-1, keepdims=True)
        o = jnp.dot(p.astype(BF16), v, preferred_element_type=F32) / den
        oh_scr[:, hd * MLA_V:(hd + 1) * MLA_V] = o.astype(BF16)
    gt = mod_ref[0][2:3]
    o_ref[0] = x_ref[0] + gt * jnp.dot(oh_scr[...], wo_ref[...], preferred_element_type=F32)


def _attention(q, k, v, x, mod_l, w_out, tq, mod_row):
    b, t, _ = q.shape
    s = k.shape[1]
    return pl.pallas_call(
        _attn_kernel,
        out_shape=jax.ShapeDtypeStruct((b, t, D_MODEL), F32),
        grid=(b, t // tq),
        in_specs=[
            pl.BlockSpec((1, tq, N_HEADS * HEAD_PAD), lambda bi, qi: (bi, qi, 0)),
            pl.BlockSpec((1, s, K_ALL), lambda bi, qi: (bi, 0, 0)),
            pl.BlockSpec((1, s, V_ALL), lambda bi, qi: (bi, 0, 0)),
            pl.BlockSpec((1, tq, D_MODEL), lambda bi, qi: (bi, qi, 0)),
            pl.BlockSpec((1, 6, D_MODEL), lambda bi, qi: (mod_row(bi), 0, 0)),
            pl.BlockSpec((D_MODEL, D_MODEL), lambda bi, qi: (0, 0)),
        ],
        out_specs=pl.BlockSpec((1, tq, D_MODEL), lambda bi, qi: (bi, qi, 0)),
        scratch_shapes=[pltpu.VMEM((tq, D_MODEL), BF16)],
        compiler_params=_cparams(("arbitrary", "arbitrary")),
        name="attention",
    )(q, k, v, x, mod_l, w_out)


def _cmul(ar, ai, br, bi):
    return ar * br - ai * bi, ar * bi + ai * br


def _ssm_ops_kernel(a_ref, dt_ref, bt_ref, c_ref, ops_ref, toep_ref, wct_ref, lam_ref,
                    cl_re_scr, cl_im_scr, op_scr, tp_scr, wc_scr):
    L, H, P = SSM_CHUNK, SSM_GROUP, SSM_STATE
    lane = lax.broadcasted_iota(jnp.int32, (H, CHUNK_W), 1)

    def one_group(g, carry):
        tp_scr[...] = jnp.zeros_like(tp_scr)
        for d in range(2):
            a = a_ref[g, d]
            a_re, a_im = a[0:1], a[1:2]
            dt = jnp.exp(dt_ref[g, d])
            mag = jnp.exp(a_re * dt)
            ang = a_im * dt
            ab_re, ab_im = mag * jnp.cos(ang), mag * jnp.sin(ang)
            den = a_re * a_re + a_im * a_im
            n_re, n_im = ab_re - 1.0, ab_im
            k_re = (n_re * a_re + n_im * a_im) / den
            k_im = (n_im * a_re - n_re * a_im) / den
            bt_re, bt_im = bt_ref[g, d, 0], bt_ref[g, d, 1]
            bb_re, bb_im = _cmul(k_re, k_im, bt_re, bt_im)
            c_re, c_im = c_ref[g, d, 0], c_ref[g, d, 1]

            pw = [(jnp.ones_like(ab_re), jnp.zeros_like(ab_im))]
            for _ in range(L):
                pw.append(_cmul(pw[-1][0], pw[-1][1], ab_re, ab_im))

            for e in range(L + 1):
                pr, pi = pw[e] if d == 0 else pw[L - e]
                cr, ci = _cmul(c_re, c_im, pr, pi)
                cl_re_scr[e * H:(e + 1) * H, :] = cr
                cl_im_scr[e * H:(e + 1) * H, :] = ci
            lo = 0 if d == 0 else H
            r = (lax.dot_general(bb_re, cl_re_scr[lo:lo + CHUNK_W, :], (((1,), (1,)), ((), ())),
                                 preferred_element_type=F32, precision=HIGHEST)
                 - lax.dot_general(bb_im, cl_im_scr[lo:lo + CHUNK_W, :], (((1,), (1,)), ((), ())),
                                   preferred_element_type=F32, precision=HIGHEST))
            wlo = H if d == 0 else 0
            wc_scr[:, d * STATE_W:d * STATE_W + P] = cl_re_scr[wlo:wlo + CHUNK_W, :]
            wc_scr[:, d * STATE_W + P:(d + 1) * STATE_W] = -cl_im_scr[wlo:wlo + CHUNK_W, :]
            for j in range(L):
                if d == 0:
                    blk = r if j == 0 else jnp.where(lane >= H * j, pltpu.roll(r, H * j, 1), 0.0)
                    sr, si = _cmul(pw[L - 1 - j][0], pw[L - 1 - j][1], bb_re, bb_im)
                else:
                    sft = (L - 1 - j) * H
                    blk = r if sft == 0 else jnp.where(lane < H * (j + 1), pltpu.roll(r, CHUNK_W - sft, 1), 0.0)
                    sr, si = _cmul(pw[j][0], pw[j][1], bb_re, bb_im)
                tp_scr[j * H:(j + 1) * H, :] += blk
                op_scr[j * H:(j + 1) * H, d * STATE_W:d * STATE_W + P] = sr
                op_scr[j * H:(j + 1) * H, d * STATE_W + P:(d + 1) * STATE_W] = si
            pr, pi = pw[L]
            lam_ref[g, 2 * d:2 * d + 1, 0:P] = pr
            lam_ref[g, 2 * d:2 * d + 1, P:STATE_W] = pr
            lam_ref[g, 2 * d + 1:2 * d + 2, 0:P] = -pi
            lam_ref[g, 2 * d + 1:2 * d + 2, P:STATE_W] = pi
        ops_ref[g] = op_scr[...].astype(BF16)
        toep_ref[g] = tp_scr[...].astype(BF16)
        wct_ref[g] = wc_scr[...].astype(BF16)
        return carry

    lax.fori_loop(0, a_ref.shape[0], one_group, 0)


def _ssm_ops(a, dt, bt, c):
    gb = GB_SSM
    G = SSM_GROUPS
    blk = lambda shape: pl.BlockSpec((gb,) + shape, lambda i: (i,) + (0,) * len(shape))
    return pl.pallas_call(
        _ssm_ops_kernel,
        out_shape=[jax.ShapeDtypeStruct((G, CHUNK_W, 2 * STATE_W), BF16),
                   jax.ShapeDtypeStruct((G, CHUNK_W, CHUNK_W), BF16),
                   jax.ShapeDtypeStruct((G, CHUNK_W, 2 * STATE_W), BF16),
                   jax.ShapeDtypeStruct((G, 4, STATE_W), F32)],
        grid=(G // gb,),
        in_specs=[blk((2, 2, SSM_STATE)), blk((2, 1, SSM_STATE)), blk((2, 2, SSM_GROUP, SSM_STATE)),
                  blk((2, 2, SSM_GROUP, SSM_STATE))],
        out_specs=[blk((CHUNK_W, 2 * STATE_W)), blk((CHUNK_W, CHUNK_W)), blk((CHUNK_W, 2 * STATE_W)),
                   blk((4, STATE_W))],
        scratch_shapes=[pltpu.VMEM(((SSM_CHUNK + 1) * SSM_GROUP, SSM_STATE), F32),
                        pltpu.VMEM(((SSM_CHUNK + 1) * SSM_GROUP, SSM_STATE), F32),
                        pltpu.VMEM((CHUNK_W, 2 * STATE_W), F32),
                        pltpu.VMEM((CHUNK_W, CHUNK_W), F32),
                        pltpu.VMEM((CHUNK_W, 2 * STATE_W), F32)],
        compiler_params=_cparams(("arbitrary",)),
        name="ssm_ops",
    )(a, dt, bt, c)


GRANULES = LANES // SSM_GROUP


def _granule_transpose(arrs):
    a = list(arrs)
    lane = lax.broadcasted_iota(jnp.int32, a[0].shape, 1)
    for d in (4, 2, 1):
        upper = (lane & (d * SSM_GROUP)) != 0
        for s in range(GRANULES):
            if s & d:
                continue
            lo, hi = a[s], a[s + d]
            a[s] = jnp.where(upper, pltpu.roll(hi, d * SSM_GROUP, 1), lo)
            a[s + d] = jnp.where(upper, hi, pltpu.roll(lo, LANES - d * SSM_GROUP, 1))
    return a


def _ssm_in_kernel(x_ref, mod_ref, g_ref, u_ref, h_scr):
    m = mod_ref[0]
    h = _rms(x_ref[...], g_ref[...]) * (1.0 + m[1:2]) + m[0:1]
    mb = h_scr.shape[1] // SSM_CHUNK
    for g8 in range(SSM_GROUPS // GRANULES):
        h_scr[g8] = h[:, g8 * LANES:(g8 + 1) * LANES]
    for g8 in range(SSM_GROUPS // GRANULES):
        for half in range(SSM_CHUNK // GRANULES):
            a = [h_scr[g8, pl.ds(GRANULES * half + s, mb, stride=SSM_CHUNK), :] for s in range(GRANULES)]
            b = _granule_transpose(a)
            for g in range(GRANULES):
                u_ref[g8 * GRANULES + g, :, half * LANES:(half + 1) * LANES] = b[g].astype(BF16)


def _ssm_in(x, mod_l, g1):
    tm = TM_SSM_IN
    row = _mod_row(tm)
    return pl.pallas_call(
        _ssm_in_kernel,
        out_shape=jax.ShapeDtypeStruct((SSM_GROUPS, N_CHUNK_ROWS, CHUNK_W), BF16),
        grid=(N_TOK // tm,),
        in_specs=[pl.BlockSpec((tm, D_MODEL), lambda i: (i, 0)),
                  pl.BlockSpec((1, 6, D_MODEL), lambda i: (row(i), 0, 0)),
                  pl.BlockSpec((1, D_MODEL), lambda i: (0, 0))],
        out_specs=pl.BlockSpec((SSM_GROUPS, tm // SSM_CHUNK, CHUNK_W), lambda i: (0, i, 0)),
        scratch_shapes=[pltpu.VMEM((D_MODEL // LANES, tm, LANES), F32)],
        compiler_params=_cparams(("arbitrary",)),
        name="ssm_in",
    )(x, mod_l, g1.reshape(1, D_MODEL))


def _ssm_ends_kernel(u_ref, ops_ref, sf_ref, sb_ref):
    for g in range(u_ref.shape[0]):
        s = jnp.dot(u_ref[g], ops_ref[g], preferred_element_type=F32)
        sf_ref[g] = s[:, 0:STATE_W]
        sb_ref[g] = s[:, STATE_W:]


def _ssm_ends(u, ops):
    gb = GB_SSM
    G = SSM_GROUPS
    return pl.pallas_call(
        _ssm_ends_kernel,
        out_shape=[jax.ShapeDtypeStruct((G, N_CHUNK_ROWS, STATE_W), F32)] * 2,
        grid=(G // gb,),
        in_specs=[pl.BlockSpec((gb, N_CHUNK_ROWS, CHUNK_W), lambda i: (i, 0, 0)),
                  pl.BlockSpec((gb, CHUNK_W, 2 * STATE_W), lambda i: (i, 0, 0))],
        out_specs=[pl.BlockSpec((gb, N_CHUNK_ROWS, STATE_W), lambda i: (i, 0, 0))] * 2,
        compiler_params=_cparams(("arbitrary",)),
        name="ssm_chunk_ends",
    )(u, ops)


def _ssm_carry_kernel(sf_ref, sb_ref, lam_ref, h0_ref, xf_ref, xb_ref, fin_ref):
    n_g = sf_ref.shape[0]

    def swap(t):
        return pltpu.roll(t, SSM_STATE, 1)

    def run(n_steps, row0, rows, stride, x0):
        lam = [[jnp.broadcast_to(lam_ref[g, k:k + 1, :], (rows, STATE_W)) for k in range(4)] for g in range(n_g)]

        def body(c, carry):
            rf = pl.ds(row0 + c, rows, stride=stride)
            rb = pl.ds(row0 + (n_steps - 1 - c), rows, stride=stride)
            out = []
            for g in range(n_g):
                xf, xfs, xb, xbs = carry[4 * g:4 * g + 4]
                a1f, a2f, a1b, a2b = lam[g]
                sf = sf_ref[g, rf, :]
                sb = sb_ref[g, rb, :]
                xf_ref[g, rf, :] = xf
                xb_ref[g, rb, :] = xb
                out += [a1f * xf + a2f * xfs + sf, a1f * xfs - a2f * xf + swap(sf),
                        a1b * xb + a2b * xbs + sb, a1b * xbs - a2b * xb + swap(sb)]
            return tuple(out)

        init = []
        for g in range(n_g):
            init += [x0[g][0], swap(x0[g][0]), x0[g][1], swap(x0[g][1])]
        return lax.fori_loop(0, n_steps, body, tuple(init))

    zero = jnp.zeros((BATCH, STATE_W), F32)
    n_p = SEQ // SSM_CHUNK
    fin = run(n_p, 0, BATCH, n_p, [(zero, zero)] * n_g)
    for g in range(n_g):
        fin_ref[g, 0] = fin[4 * g]
        fin_ref[g, 1] = fin[4 * g + 2]
    n_s = DEC_SEQ // SSM_CHUNK
    run(n_s, N_CHUNK_PROMPT, DEC_BATCH, n_s, [(h0_ref[g, 0], h0_ref[g, 1]) for g in range(n_g)])


def _ssm_carry(sf, sb, lam, h0):
    gb = GB_CARRY
    G = SSM_GROUPS
    st = pl.BlockSpec((gb, N_CHUNK_ROWS, STATE_W), lambda i: (i, 0, 0))
    return pl.pallas_call(
        _ssm_carry_kernel,
        out_shape=[jax.ShapeDtypeStruct(sf.shape, F32)] * 2 + [jax.ShapeDtypeStruct((G, 2, BATCH, STATE_W), F32)],
        grid=(G // gb,),
        in_specs=[st, st, pl.BlockSpec((gb, 4, STATE_W), lambda i: (i, 0, 0)),
                  pl.BlockSpec((gb, 2, DEC_BATCH, STATE_W), lambda i: (i, 0, 0, 0))],
        out_specs=[st, st, pl.BlockSpec((gb, 2, BATCH, STATE_W), lambda i: (i, 0, 0, 0))],
        compiler_params=_cparams(("arbitrary",)),
        name="ssm_carry",
    )(sf, sb, lam, h0)


def _ssm_out_kernel(u_ref, toep_ref, xf_ref, xb_ref, wct_ref, y_ref, y_scr):
    for g in range(GRANULES):
        y = jnp.dot(u_ref[g], toep_ref[g], preferred_element_type=F32)
        for d, x_ref in enumerate((xf_ref, xb_ref)):
            xin = x_ref[g].astype(BF16)
            w = wct_ref[g, :, d * STATE_W:(d + 1) * STATE_W]
            y += lax.dot_general(xin, w, (((1,), (1,)), ((), ())), preferred_element_type=F32)
        y_scr[g] = y
    rb = OUT_ROW_BLOCK
    for r in range(N_CHUNK_ROWS // rb):
        for half in range(SSM_CHUNK // GRANULES):
            a = [y_scr[g, r * rb:(r + 1) * rb, half * LANES:(half + 1) * LANES] for g in range(GRANULES)]
            b = _granule_transpose(a)
            for s in range(GRANULES):
                y_ref[pl.ds(r * rb * SSM_CHUNK + GRANULES * half + s, rb, stride=SSM_CHUNK), :] = b[s]


def _ssm_out(u, toep, xf, xb, wct):
    gb = GRANULES
    G = SSM_GROUPS
    return pl.pallas_call(
        _ssm_out_kernel,
        out_shape=jax.ShapeDtypeStruct((N_TOK, D_MODEL), F32),
        grid=(G // gb,),
        in_specs=[pl.BlockSpec((gb, N_CHUNK_ROWS, CHUNK_W), lambda i: (i, 0, 0)),
                  pl.BlockSpec((gb, CHUNK_W, CHUNK_W), lambda i: (i, 0, 0)),
                  pl.BlockSpec((gb, N_CHUNK_ROWS, STATE_W), lambda i: (i, 0, 0)),
                  pl.BlockSpec((gb, N_CHUNK_ROWS, STATE_W), lambda i: (i, 0, 0)),
                  pl.BlockSpec((gb, CHUNK_W, 2 * STATE_W), lambda i: (i, 0, 0))],
        out_specs=pl.BlockSpec((N_TOK, LANES), lambda i: (0, i)),
        scratch_shapes=[pltpu.VMEM((gb, N_CHUNK_ROWS, CHUNK_W), F32)],
        compiler_params=_cparams(("arbitrary",)),
        name="ssm_chunk_out",
    )(u, toep, xf, xb, wct)


def _ssm_post_kernel(x_ref, mod_ref, g_ref, y_ref, d_ref, w_ref, b_ref, o_ref):
    m = mod_ref[0]
    x = x_ref[...]
    h = _rms(x, g_ref[...]) * (1.0 + m[1:2]) + m[0:1]
    y = h * d_ref[...] + y_ref[...]
    gl = jax.nn.gelu(y).astype(BF16)
    z = jnp.dot(gl, w_ref[...], preferred_element_type=F32) + b_ref[...]
    o_ref[...] = x + m[2:3] * (z[:, :D_MODEL] * jax.nn.sigmoid(z[:, D_MODEL:]))


def _ssm_post(x, mod_l, g1, y, d_skip, w_glu, b_glu):
    tm = TM_SSM
    row = _mod_row(tm)
    full = lambda a: pl.BlockSpec(a.shape, lambda i: (0,) * a.ndim)
    d2, b2 = d_skip.reshape(1, D_MODEL), b_glu.reshape(1, 2 * D_MODEL)
    g2 = g1.reshape(1, D_MODEL)
    return pl.pallas_call(
        _ssm_post_kernel,
        out_shape=jax.ShapeDtypeStruct((N_TOK, D_MODEL), F32),
        grid=(N_TOK // tm,),
        in_specs=[pl.BlockSpec((tm, D_MODEL), lambda i: (i, 0)),
                  pl.BlockSpec((1, 6, D_MODEL), lambda i: (row(i), 0, 0)),
                  full(g2), pl.BlockSpec((tm, D_MODEL), lambda i: (i, 0)), full(d2), full(w_glu), full(b2)],
        out_specs=pl.BlockSpec((tm, D_MODEL), lambda i: (i, 0)),
        compiler_params=_cparams(("arbitrary",)),
        name="ssm_post",
    )(x, mod_l, g2, y, d2, w_glu, b2)


def _rope_tables():
    rows = DEC_SEQ // GRID_W
    t = np.arange(DEC_SEQ)
    row, col = (t // GRID_W).astype(np.float32), (t % GRID_W).astype(np.float32)

    def table(rot_dim, lane0):
        n_freq = rot_dim // 4
        inv_freq = jnp.asarray(ROPE_THETA, F32) ** (-jnp.arange(n_freq, dtype=F32) / n_freq)
        ang_row = jnp.asarray(row)[:, None] * inv_freq
        ang_col = jnp.asarray(col)[:, None] * inv_freq
        ang = jnp.concatenate([ang_row, ang_row, ang_col, ang_col], axis=1)
        sign = jnp.tile(jnp.concatenate([-jnp.ones(n_freq, F32), jnp.ones(n_freq, F32)]), 2)
        cos = jnp.ones((DEC_SEQ, HEAD_PAD), F32).at[:, lane0:lane0 + rot_dim].set(jnp.cos(ang))
        sin = jnp.zeros((DEC_SEQ, HEAD_PAD), F32).at[:, lane0:lane0 + rot_dim].set(jnp.sin(ang) * sign)
        cos = jnp.concatenate([jnp.ones((N_PROMPT, HEAD_PAD), F32), jnp.tile(cos, (DEC_BATCH, 1))], axis=0)
        sin = jnp.concatenate([jnp.zeros((N_PROMPT, HEAD_PAD), F32), jnp.tile(sin, (DEC_BATCH, 1))], axis=0)
        return cos, sin

    assert rows * GRID_W == DEC_SEQ
    return table(MLA_ROPE, MLA_NOPE) + table(GQA_DIM, 0)


def _pad_heads(w, n_heads, dim):
    lead = w.shape[:-1]
    w = w.reshape(lead + (n_heads, dim))
    w = jnp.pad(w, [(0, 0)] * len(lead) + [(0, 0), (0, HEAD_PAD - dim)])
    return w.reshape(lead + (n_heads * HEAD_PAD,))


def _attn_weights(w_in, g_qa, g_kva, w_uq, w_ukv, g_mq, g_mk, g_gq, g_gk, w_out):
    o1, o2, o3 = Q_LORA, Q_LORA + KV_LORA, Q_LORA + KV_LORA + MLA_ROPE
    o4 = o3 + GQA_HEADS * GQA_DIM
    o5 = o4 + GQA_KV_HEADS * GQA_DIM
    kpe = jnp.pad(w_in[:, o2:o3], ((0, 0), (MLA_NOPE, HEAD_PAD - MLA_QK)))
    w_in_ext = jnp.concatenate([w_in[:, :o2], kpe, _pad_heads(w_in[:, o3:o4], GQA_HEADS, GQA_DIM),
                                _pad_heads(w_in[:, o4:o5], GQA_KV_HEADS, GQA_DIM), w_in[:, o5:]], axis=1)
    ukv = w_ukv.reshape(KV_LORA, MLA_HEADS, MLA_NOPE + MLA_V)
    w_ukv_perm = jnp.concatenate([_pad_heads(ukv[:, :, :MLA_NOPE].reshape(KV_LORA, -1), MLA_HEADS, MLA_NOPE),
                                  ukv[:, :, MLA_NOPE:].reshape(KV_LORA, -1)], axis=1)
    pad_g = lambda g: jnp.pad(g, (0, HEAD_PAD - g.shape[0])).reshape(1, HEAD_PAD)
    return dict(w_in=w_in_ext.astype(BF16), g_qa=g_qa.reshape(1, -1), g_kva=g_kva.reshape(1, -1),
                w_uq=_pad_heads(w_uq, MLA_HEADS, MLA_QK).astype(BF16), w_ukv=w_ukv_perm.astype(BF16),
                g_mq=pad_g(g_mq), g_mk=pad_g(g_mk), g_gq=pad_g(g_gq), g_gk=pad_g(g_gk),
                w_out=w_out.astype(BF16))


def _attn_layer(x, mod_l, g1, aw, tabs, cache):
    q, k, v, ckv, kpe, gk, gv = _attn_pre(x, mod_l, g1, aw, tabs)
    c_ckv, c_krope, c_gk, c_gv = cache
    n_c = DEC_BATCH * PAST_LEN
    kpe_c = jnp.pad(c_krope.reshape(n_c, MLA_ROPE), ((0, 0), (MLA_NOPE, HEAD_PAD - MLA_QK)))
    km_c, vm_c = _cache_kv(c_ckv.reshape(n_c, KV_LORA), kpe_c, aw)
    k_c = jnp.concatenate([km_c, _pad_heads(c_gk.reshape(n_c, -1), GQA_KV_HEADS, GQA_DIM).astype(BF16)], axis=1)
    v_c = jnp.concatenate([vm_c, c_gv.reshape(n_c, -1).astype(BF16)], axis=1)
    k_s = jnp.concatenate([k_c.reshape(DEC_BATCH, PAST_LEN, K_ALL),
                           k[N_PROMPT:].reshape(DEC_BATCH, DEC_SEQ, K_ALL)], axis=1)
    v_s = jnp.concatenate([v_c.reshape(DEC_BATCH, PAST_LEN, V_ALL),
                           v[N_PROMPT:].reshape(DEC_BATCH, DEC_SEQ, V_ALL)], axis=1)
    qw = N_HEADS * HEAD_PAD
    xp = _attention(q[:N_PROMPT].reshape(BATCH, SEQ, qw), k[:N_PROMPT].reshape(BATCH, SEQ, K_ALL),
                    v[:N_PROMPT].reshape(BATCH, SEQ, V_ALL), x[:N_PROMPT].reshape(BATCH, SEQ, D_MODEL),
                    mod_l, aw["w_out"], SEQ, lambda b: 0)
    xs = _attention(q[N_PROMPT:].reshape(DEC_BATCH, DEC_SEQ, qw), k_s, v_s,
                    x[N_PROMPT:].reshape(DEC_BATCH, DEC_SEQ, D_MODEL), mod_l, aw["w_out"], TQ_SAMPLE,
                    lambda b: 1 + b)
    x = jnp.concatenate([xp.reshape(N_PROMPT, D_MODEL), xs.reshape(N_SAMPLE, D_MODEL)], axis=0)
    new = (ckv[:N_PROMPT].reshape(BATCH, SEQ, KV_LORA),
           kpe[:N_PROMPT, MLA_NOPE:MLA_QK].reshape(BATCH, SEQ, MLA_ROPE),
           gk[:N_PROMPT].reshape(BATCH, SEQ, GQA_KV_HEADS, HEAD_PAD)[..., :GQA_DIM],
           gv[:N_PROMPT].reshape(BATCH, SEQ, GQA_KV_HEADS, GQA_DIM))
    return x, new


def _ssm_layer(x, mod_l, g1, a_re, a_im, log_dt, b_re, b_im, c_re, c_im, d_skip, w_glu, b_glu, state0):
    G, P, H = SSM_GROUPS, SSM_STATE, SSM_GROUP
    a = jnp.stack([a_re, a_im], axis=2).transpose(1, 0, 2, 3)
    dt = jnp.broadcast_to(log_dt.transpose(1, 0)[:, :, None, None], (G, 2, 1, P))
    bt = jnp.stack([b_re, b_im], axis=2).transpose(1, 0, 2, 4, 3)
    c = jnp.stack([c_re, c_im], axis=2).transpose(1, 0, 2, 3, 4)
    ops, toep, wct, lam = _ssm_ops(a, dt, bt, c)

    u = _ssm_in(x, mod_l, g1)
    sf, sb = _ssm_ends(u, ops)
    h0 = state0.transpose(2, 1, 0, 4, 3).reshape(G, 2, DEC_BATCH, STATE_W)
    xf, xb, fin = _ssm_carry(sf, sb, lam, h0)
    y = _ssm_out(u, toep, xf, xb, wct)
    x = _ssm_post(x, mod_l, g1, y, d_skip, w_glu.astype(BF16), b_glu)
    fin = fin.reshape(G, 2, BATCH, 2, P).transpose(2, 1, 0, 4, 3)
    return x, fin


def kernel(x_prompt, x_sample, c, cache_mla_ckv, cache_mla_krope, cache_gqa_k, cache_gqa_v, state_ssm, c_ctx,
           norm1_g, norm2_g, w_mod, b_mod,
           attn_w_in, attn_qa_norm_g, attn_kva_norm_g, attn_w_uq, attn_w_ukv,
           attn_mla_q_norm_g, attn_mla_k_norm_g, attn_gqa_q_norm_g, attn_gqa_k_norm_g, attn_w_out,
           ssm_a_re, ssm_a_im, ssm_log_dt, ssm_b_re, ssm_b_im, ssm_c_re, ssm_c_im, ssm_d, ssm_w_glu, ssm_b_glu,
           ffn_w_up, ffn_conv_w, ffn_conv_b, ffn_w_down):
    x = jnp.concatenate([x_prompt.reshape(N_PROMPT, D_MODEL), x_sample.reshape(N_SAMPLE, D_MODEL)], axis=0)
    cond8 = jnp.concatenate([c_ctx[None, :], c, jnp.zeros((8 - 1 - DEC_BATCH, D_MODEL), F32)], axis=0)
    mod = _modulation(cond8, w_mod, b_mod).reshape(DEPTH, 8, 6, D_MODEL)
    tabs = _rope_tables()

    new_attn, new_ssm = [], []
    for i in range(DEPTH):
        j = i // 2
        if i % 2 == 0:
            aw = _attn_weights(attn_w_in[j], attn_qa_norm_g[j], attn_kva_norm_g[j], attn_w_uq[j], attn_w_ukv[j],
                               attn_mla_q_norm_g[j], attn_mla_k_norm_g[j], attn_gqa_q_norm_g[j],
                               attn_gqa_k_norm_g[j], attn_w_out[j])
            cache = (cache_mla_ckv[:, j], cache_mla_krope[:, j], cache_gqa_k[:, j], cache_gqa_v[:, j])
            x, new = _attn_layer(x, mod[i], norm1_g[i], aw, tabs, cache)
            new_attn.append(new)
        else:
            x, fin = _ssm_layer(x, mod[i], norm1_g[i], ssm_a_re[j], ssm_a_im[j], ssm_log_dt[j], ssm_b_re[j],
                                ssm_b_im[j], ssm_c_re[j], ssm_c_im[j], ssm_d[j], ssm_w_glu[j], ssm_b_glu[j],
                                state_ssm[:, j])
            new_ssm.append(fin)
        x = _conv_ffn(x, mod[i], norm2_g[i], ffn_w_up[i].astype(BF16), ffn_conv_w[i], ffn_conv_b[i],
                      ffn_w_down[i].astype(BF16))

    outs = [jnp.stack([n[k] for n in new_attn], axis=1) for k in range(4)]
    return (x[:N_PROMPT].reshape(BATCH, SEQ, D_MODEL), x[N_PROMPT:].reshape(DEC_BATCH, DEC_SEQ, D_MODEL),
            outs[0], outs[1], outs[2], outs[3], jnp.stack(new_ssm, axis=1))
```

```python
import functools
import math

import jax
import jax.numpy as jnp
import numpy as np
from jax import lax
from jax.experimental import pallas as pl
from jax.experimental.pallas import tpu as pltpu

F32 = jnp.float32
BF16 = jnp.bfloat16
HIGHEST = lax.Precision.HIGHEST

LANES = 128
BF16_SUBLANES = 16
VMEM_LIMIT_BYTES = 56 * 1024 * 1024

D_MODEL = 1024
BATCH = 16
SEQ = 256
DEPTH = 4
DEC_BATCH = 2
DEC_SEQ = 2048
PAST_LEN = 256
GRID_W = 64
MLA_HEADS = 8
Q_LORA = 384
KV_LORA = 256
MLA_NOPE = 64
MLA_ROPE = 32
MLA_QK = MLA_NOPE + MLA_ROPE
MLA_V = 64
GQA_HEADS = 8
GQA_KV_HEADS = 2
GQA_DIM = 64
GQA_REP = GQA_HEADS // GQA_KV_HEADS
ROPE_THETA = 10000.0
SSM_GROUP = 16
SSM_GROUPS = D_MODEL // SSM_GROUP
SSM_STATE = 64
D_FF = 2816
EPS = 1e-6

N_PROMPT = BATCH * SEQ
N_SAMPLE = DEC_BATCH * DEC_SEQ
N_TOK = N_PROMPT + N_SAMPLE
N_HEADS = MLA_HEADS + GQA_HEADS
HEAD_PAD = LANES
KV_LEN_SAMPLE = PAST_LEN + DEC_SEQ

IN_Q = 0
IN_CKV = IN_Q + Q_LORA
IN_KPE = IN_CKV + KV_LORA
IN_GQ = IN_KPE + HEAD_PAD
IN_GK = IN_GQ + GQA_HEADS * HEAD_PAD
IN_GV = IN_GK + GQA_KV_HEADS * HEAD_PAD
IN_EXT = IN_GV + GQA_KV_HEADS * GQA_DIM

K_ALL = (MLA_HEADS + GQA_KV_HEADS) * HEAD_PAD
V_ALL = MLA_HEADS * MLA_V + GQA_KV_HEADS * GQA_DIM

SSM_CHUNK = 16
CHUNK_W = SSM_CHUNK * SSM_GROUP
N_CHUNK_ROWS = N_TOK // SSM_CHUNK
N_CHUNK_PROMPT = N_PROMPT // SSM_CHUNK
STATE_W = 2 * SSM_STATE

TM_FFN = 512
TN_FFN = 256
FFN_HALO = BF16_SUBLANES
TM_PRE = 512
TQ_SAMPLE = 256
TM_SSM = 1024
TM_SSM_IN = 512
OUT_ROW_BLOCK = 32
GB_CARRY = 4
TN_MOD = 1536
GB_SSM = 8


def _cparams(sem):
    return pltpu.CompilerParams(dimension_semantics=sem, vmem_limit_bytes=VMEM_LIMIT_BYTES)


def _rms(x, g):
    return x * lax.rsqrt(jnp.mean(x * x, axis=-1, keepdims=True) + EPS) * g


def _mod_row(tm):
    n_prompt_tiles = N_PROMPT // tm
    tiles_per_seq = DEC_SEQ // tm

    def row(i):
        return jnp.where(i < n_prompt_tiles, 0, 1 + (i - n_prompt_tiles) // tiles_per_seq)

    return row


def _mod_kernel(cond_ref, w_ref, b_ref, o_ref):
    a = jax.nn.silu(cond_ref[...])
    o_ref[0] = jnp.dot(a, w_ref[0], preferred_element_type=F32, precision=HIGHEST) + b_ref[0]


def _modulation(cond8, w_mod, b_mod):
    n = w_mod.shape[-1]
    return pl.pallas_call(
        _mod_kernel,
        out_shape=jax.ShapeDtypeStruct((DEPTH, 8, n), F32),
        grid=(DEPTH, n // TN_MOD),
        in_specs=[
            pl.BlockSpec((8, D_MODEL), lambda l, j: (0, 0)),
            pl.BlockSpec((1, D_MODEL, TN_MOD), lambda l, j: (l, 0, j)),
            pl.BlockSpec((1, 1, TN_MOD), lambda l, j: (l, 0, j)),
        ],
        out_specs=pl.BlockSpec((1, 8, TN_MOD), lambda l, j: (l, 0, j)),
        compiler_params=_cparams(("arbitrary", "arbitrary")),
        name="modulation",
    )(cond8, w_mod, b_mod.reshape(DEPTH, 1, n))


def _ffn_kernel(xp_ref, x_ref, xn_ref, mod_ref, g_ref, wup_ref, cw_ref, cb_ref, wd_ref, o_ref,
                h_scr, act_scr, z_scr):
    i = pl.program_id(0)
    tm = x_ref.shape[0]
    tn = TN_FFN
    m = mod_ref[0]
    sh, sc, gt = m[3:4], m[4:5], m[5:6]
    g = g_ref[...]

    def hmod(x):
        return (_rms(x, g) * (1.0 + sc) + sh).astype(BF16)

    h_scr[0:FFN_HALO] = hmod(xp_ref[...])
    h_scr[FFN_HALO:FFN_HALO + tm] = hmod(x_ref[...])
    h_scr[FFN_HALO + tm:] = hmod(xn_ref[...])
    seq_len = jnp.where(i * tm < N_PROMPT, SEQ, DEC_SEQ)
    pos = (i * tm + lax.broadcasted_iota(jnp.int32, (tm, 1), 0)) & (seq_len - 1)
    eg = BF16_SUBLANES
    bounds = range(0, tm + 1, SEQ)
    cuts = sorted({0, tm} | {b + o for b in bounds for o in (-eg, eg) if 0 < b + o < tm})
    segments = [(r0, r1, any(r0 <= b < r1 or r0 < b <= r1 for b in bounds)) for r0, r1 in zip(cuts[:-1], cuts[1:])]

    def conv(slot, k, col, r0, r1, masked):
        cw = cw_ref[:, col:col + tn]
        zp = z_scr[slot, k, FFN_HALO - 1 + r0:FFN_HALO - 1 + r1, :]
        zc = z_scr[slot, k, FFN_HALO + r0:FFN_HALO + r1, :]
        zn = z_scr[slot, k, FFN_HALO + 1 + r0:FFN_HALO + 1 + r1, :]
        if masked:
            zp = jnp.where(pos[r0:r1] == 0, 0.0, zp)
            zn = jnp.where(pos[r0:r1] == seq_len - 1, 0.0, zn)
        return zp * cw[0:1] + zc * cw[1:2] + zn * cw[2:3] + cb_ref[:, col:col + tn]

    h = h_scr[...]
    for jc in range(D_FF // tn):
        ca, cb = jc * tn, D_FF + jc * tn
        slot = jc % 2
        z_scr[slot, 0] = jnp.dot(h, wup_ref[:, ca:ca + tn], preferred_element_type=F32)
        z_scr[slot, 1] = jnp.dot(h, wup_ref[:, cb:cb + tn], preferred_element_type=F32)
        for r0, r1, masked in segments:
            act_scr[r0:r1, ca:ca + tn] = (jax.nn.silu(conv(slot, 0, ca, r0, r1, masked))
                                          * conv(slot, 1, cb, r0, r1, masked)).astype(BF16)
    o_ref[...] = x_ref[...] + gt * jnp.dot(act_scr[...], wd_ref[...], preferred_element_type=F32)


def _conv_ffn(x, mod_l, g, w_up, conv_w, conv_b, w_down):
    tm, halo = TM_FFN, FFN_HALO
    row = _mod_row(tm)
    n_halo_blocks = N_TOK // halo
    resident = lambda a: pl.BlockSpec(a.shape, lambda i: (0,) * a.ndim, pipeline_mode=pl.Buffered(1))
    cb2 = conv_b.reshape(1, -1)
    return pl.pallas_call(
        _ffn_kernel,
        out_shape=jax.ShapeDtypeStruct((N_TOK, D_MODEL), F32),
        grid=(N_TOK // tm,),
        in_specs=[
            pl.BlockSpec((halo, D_MODEL), lambda i: (jnp.maximum(i * (tm // halo) - 1, 0), 0)),
            pl.BlockSpec((tm, D_MODEL), lambda i: (i, 0)),
            pl.BlockSpec((halo, D_MODEL), lambda i: (jnp.minimum((i + 1) * (tm // halo), n_halo_blocks - 1), 0)),
            pl.BlockSpec((1, 6, D_MODEL), lambda i: (row(i), 0, 0)),
            pl.BlockSpec((1, D_MODEL), lambda i: (0, 0)),
            resident(w_up), resident(conv_w), resident(cb2), resident(w_down),
        ],
        out_specs=pl.BlockSpec((tm, D_MODEL), lambda i: (i, 0)),
        scratch_shapes=[pltpu.VMEM((tm + 2 * halo, D_MODEL), BF16), pltpu.VMEM((tm, D_FF), BF16),
                        pltpu.VMEM((2, 2, tm + 2 * halo, TN_FFN), F32)],
        compiler_params=_cparams(("arbitrary",)),
        name="conv_ffn",
    )(x, x, x, mod_l, g.reshape(1, D_MODEL), w_up, conv_w, cb2, w_down)


def _rope(x, cos, sin, half):
    lane = lax.broadcasted_iota(jnp.int32, x.shape, 1)
    first = ((lane // half) & 1) == 0
    partner = jnp.where(first, pltpu.roll(x, LANES - half, 1), pltpu.roll(x, half, 1))
    return x * cos + partner * sin


def _head_norm(xh, g, dim):
    return xh * lax.rsqrt(jnp.sum(xh * xh, axis=-1, keepdims=True) * (1.0 / dim) + EPS) * g


def _mla_kv_heads(ckv_bf, kpe, w_ukv_ref, g_mk, cos_m, sin_m, k_ref, v_ref):
    kv = jnp.dot(ckv_bf, w_ukv_ref[...], preferred_element_type=F32)
    for h in range(MLA_HEADS):
        kh = kv[:, h * HEAD_PAD:(h + 1) * HEAD_PAD] + kpe
        kh = _head_norm(kh, g_mk, MLA_QK)
        if cos_m is not None:
            kh = _rope(kh, cos_m, sin_m, MLA_ROPE // 4)
        k_ref[:, h * HEAD_PAD:(h + 1) * HEAD_PAD] = kh.astype(BF16)
    v_ref[:, 0:MLA_HEADS * MLA_V] = kv[:, MLA_HEADS * HEAD_PAD:].astype(BF16)


def _attn_pre_kernel(x_ref, mod_ref, g1_ref, w_in_ref, g_qa_ref, g_kva_ref, w_uq_ref, w_ukv_ref,
                     g_mq_ref, g_mk_ref, g_gq_ref, g_gk_ref, cos_m_ref, sin_m_ref, cos_g_ref, sin_g_ref,
                     q_ref, k_ref, v_ref, ckv_ref, kpe_ref, gk_ref, gv_ref):
    m = mod_ref[0]
    sh, sc = m[0:1], m[1:2]
    h = (_rms(x_ref[...], g1_ref[...]) * (1.0 + sc) + sh).astype(BF16)
    p = jnp.dot(h, w_in_ref[...], preferred_element_type=F32)
    q_c = _rms(p[:, IN_Q:IN_CKV], g_qa_ref[...]).astype(BF16)
    ckv = _rms(p[:, IN_CKV:IN_KPE], g_kva_ref[...])
    kpe = p[:, IN_KPE:IN_GQ]
    ckv_ref[...] = ckv
    kpe_ref[...] = kpe
    gv = p[:, IN_GV:IN_EXT]
    gv_ref[...] = gv
    v_ref[:, MLA_HEADS * MLA_V:] = gv.astype(BF16)
    qm = jnp.dot(q_c, w_uq_ref[...], preferred_element_type=F32)

    def heads(rope):
        cos_m, sin_m = (cos_m_ref[...], sin_m_ref[...]) if rope else (None, None)
        cos_g, sin_g = (cos_g_ref[...], sin_g_ref[...]) if rope else (None, None)
        g_mq = g_mq_ref[...]
        for hd in range(MLA_HEADS):
            qh = _head_norm(qm[:, hd * HEAD_PAD:(hd + 1) * HEAD_PAD], g_mq, MLA_QK)
            if rope:
                qh = _rope(qh, cos_m, sin_m, MLA_ROPE // 4)
            q_ref[:, hd * HEAD_PAD:(hd + 1) * HEAD_PAD] = (qh * (1.0 / math.sqrt(MLA_QK))).astype(BF16)
        _mla_kv_heads(ckv.astype(BF16), kpe, w_ukv_ref, g_mk_ref[...], cos_m, sin_m, k_ref, v_ref)
        g_gq = g_gq_ref[...]
        for hd in range(GQA_HEADS):
            qh = _head_norm(p[:, IN_GQ + hd * HEAD_PAD:IN_GQ + (hd + 1) * HEAD_PAD], g_gq, GQA_DIM)
            if rope:
                qh = _rope(qh, cos_g, sin_g, GQA_DIM // 4)
            q_ref[:, (MLA_HEADS + hd) * HEAD_PAD:(MLA_HEADS + hd + 1) * HEAD_PAD] = (
                qh * (1.0 / math.sqrt(GQA_DIM))).astype(BF16)
        g_gk = g_gk_ref[...]
        for hd in range(GQA_KV_HEADS):
            kh = _head_norm(p[:, IN_GK + hd * HEAD_PAD:IN_GK + (hd + 1) * HEAD_PAD], g_gk, GQA_DIM)
            gk_ref[:, hd * HEAD_PAD:(hd + 1) * HEAD_PAD] = kh
            if rope:
                kh = _rope(kh, cos_g, sin_g, GQA_DIM // 4)
            k_ref[:, (MLA_HEADS + hd) * HEAD_PAD:(MLA_HEADS + hd + 1) * HEAD_PAD] = kh.astype(BF16)

    is_latent = pl.program_id(0) >= N_PROMPT // x_ref.shape[0]
    pl.when(is_latent)(lambda: heads(True))
    pl.when(jnp.logical_not(is_latent))(lambda: heads(False))


def _attn_pre(x, mod_l, g1, aw, tabs):
    tm = TM_PRE
    row = _mod_row(tm)
    full = lambda a: pl.BlockSpec(a.shape, lambda i: (0,) * a.ndim)
    tok = lambda w: pl.BlockSpec((tm, w), lambda i: (i, 0))
    pos_block = lambda i: (jnp.maximum(i - N_PROMPT // tm, 0) % (DEC_SEQ // tm), 0)
    consts = [g1.reshape(1, D_MODEL), aw["w_in"], aw["g_qa"], aw["g_kva"], aw["w_uq"], aw["w_ukv"],
              aw["g_mq"], aw["g_mk"], aw["g_gq"], aw["g_gk"]]
    return pl.pallas_call(
        _attn_pre_kernel,
        out_shape=[
            jax.ShapeDtypeStruct((N_TOK, N_HEADS * HEAD_PAD), BF16),
            jax.ShapeDtypeStruct((N_TOK, K_ALL), BF16),
            jax.ShapeDtypeStruct((N_TOK, V_ALL), BF16),
            jax.ShapeDtypeStruct((N_TOK, KV_LORA), F32),
            jax.ShapeDtypeStruct((N_TOK, HEAD_PAD), F32),
            jax.ShapeDtypeStruct((N_TOK, GQA_KV_HEADS * HEAD_PAD), F32),
            jax.ShapeDtypeStruct((N_TOK, GQA_KV_HEADS * GQA_DIM), F32),
        ],
        grid=(N_TOK // tm,),
        in_specs=[tok(D_MODEL), pl.BlockSpec((1, 6, D_MODEL), lambda i: (row(i), 0, 0))]
        + [full(a) for a in consts] + [pl.BlockSpec((tm, HEAD_PAD), pos_block)] * 4,
        out_specs=[tok(N_HEADS * HEAD_PAD), tok(K_ALL), tok(V_ALL), tok(KV_LORA), tok(HEAD_PAD),
                   tok(GQA_KV_HEADS * HEAD_PAD), tok(GQA_KV_HEADS * GQA_DIM)],
        compiler_params=_cparams(("arbitrary",)),
        name="attn_pre",
    )(x, mod_l, *consts, *tabs)


def _cache_kv_kernel(ckv_ref, kpe_ref, w_ukv_ref, g_mk_ref, k_ref, v_ref):
    _mla_kv_heads(ckv_ref[...].astype(BF16), kpe_ref[...], w_ukv_ref, g_mk_ref[...], None, None, k_ref, v_ref)


def _cache_kv(ckv, kpe_pad, aw):
    n = ckv.shape[0]
    return pl.pallas_call(
        _cache_kv_kernel,
        out_shape=[jax.ShapeDtypeStruct((n, MLA_HEADS * HEAD_PAD), BF16),
                   jax.ShapeDtypeStruct((n, MLA_HEADS * MLA_V), BF16)],
        name="cache_kv",
    )(ckv, kpe_pad, aw["w_ukv"], aw["g_mk"])


def _attn_kernel(*refs, n_seg):
    q_ref, kv_refs = refs[0], refs[1:1 + 2 * n_seg]
    x_ref, mod_ref, wo_ref, o_ref, oh_scr = refs[1 + 2 * n_seg:]
    for hd in range(N_HEADS):
        if hd < MLA_HEADS:
            kc, vc = hd * HEAD_PAD, hd * MLA_V
        else:
            kvh = (hd - MLA_HEADS) // GQA_REP
            kc, vc = (MLA_HEADS + kvh) * HEAD_PAD, MLA_HEADS * MLA_V + kvh * GQA_DIM
        q = q_ref[:, hd * HEAD_PAD:(hd + 1) * HEAD_PAD]
        ss = [lax.dot_general(q, kv_refs[2 * j][:, kc:kc + HEAD_PAD], (((1,), (1,)), ((), ())),
                              preferred_element_type=F32) for j in range(n_seg)]
        m = functools.reduce(jnp.maximum, [jnp.max(s, axis=-1, keepdims=True) for s in ss])
        ps = [jnp.exp(s - m) for s in ss]
        den = sum(jnp.sum(p, axis=-1, keepdims=True) for p in ps)
        o = sum(jnp.dot(p.astype(BF16), kv_refs[2 * j + 1][:, vc:vc + MLA_V], preferred_element_type=F32)
                for j, p in enumerate(ps))
        oh_scr[:, hd * MLA_V:(hd + 1) * MLA_V] = (o / den).astype(BF16)
    gt = mod_ref[0][2:3]
    o_ref[...] = x_ref[...] + gt * jnp.dot(oh_scr[...], wo_ref[...], preferred_element_type=F32)


def _attention(q, segs, x, mod_l, w_out, tq, n_batch, t_len, tok0, mod_row):
    tiles = t_len // tq
    qmap = lambda bi, qi: (tok0 // tq + bi * tiles + qi, 0)
    kv_specs, kv_args = [], []
    for k, v, rows, blk0 in segs:
        kv_specs += [pl.BlockSpec((rows, K_ALL), lambda bi, qi, blk0=blk0: (blk0 + bi, 0)),
                     pl.BlockSpec((rows, V_ALL), lambda bi, qi, blk0=blk0: (blk0 + bi, 0))]
        kv_args += [k, v]
    return pl.pallas_call(
        functools.partial(_attn_kernel, n_seg=len(segs)),
        out_shape=jax.ShapeDtypeStruct((N_TOK, D_MODEL), F32),
        grid=(n_batch, tiles),
        in_specs=[pl.BlockSpec((tq, N_HEADS * HEAD_PAD), qmap)] + kv_specs + [
            pl.BlockSpec((tq, D_MODEL), qmap),
            pl.BlockSpec((1, 6, D_MODEL), lambda bi, qi: (mod_row(bi), 0, 0)),
            pl.BlockSpec((D_MODEL, D_MODEL), lambda bi, qi: (0, 0)),
        ],
        out_specs=pl.BlockSpec((tq, D_MODEL), qmap),
        scratch_shapes=[pltpu.VMEM((tq, D_MODEL), BF16)],
        input_output_aliases={1 + len(kv_args): 0},
        compiler_params=_cparams(("arbitrary", "arbitrary")),
        name="attention",
    )(q, *kv_args, x, mod_l, w_out)


def _cmul(ar, ai, br, bi):
    return ar * br - ai * bi, ar * bi + ai * br


def _ssm_ops_kernel(a_ref, dt_ref, bt_ref, c_ref, ops_ref, toep_ref, wct_ref, lam_ref,
                    cl_re_scr, cl_im_scr, op_scr, tp_scr, wc_scr):
    L, H, P = SSM_CHUNK, SSM_GROUP, SSM_STATE
    lane = lax.broadcasted_iota(jnp.int32, (H, CHUNK_W), 1)

    def one_group(g, carry):
        tp_scr[...] = jnp.zeros_like(tp_scr)
        for d in range(2):
            a = a_ref[g, d]
            a_re, a_im = a[0:1], a[1:2]
            dt = jnp.exp(dt_ref[g, d])
            mag = jnp.exp(a_re * dt)
            ang = a_im * dt
            ab_re, ab_im = mag * jnp.cos(ang), mag * jnp.sin(ang)
            den = a_re * a_re + a_im * a_im
            n_re, n_im = ab_re - 1.0, ab_im
            k_re = (n_re * a_re + n_im * a_im) / den
            k_im = (n_im * a_re - n_re * a_im) / den
            bt_re, bt_im = bt_ref[g, d, 0], bt_ref[g, d, 1]
            bb_re, bb_im = _cmul(k_re, k_im, bt_re, bt_im)
            c_re, c_im = c_ref[g, d, 0], c_ref[g, d, 1]

            pw = [(jnp.ones_like(ab_re), jnp.zeros_like(ab_im))]
            for _ in range(L):
                pw.append(_cmul(pw[-1][0], pw[-1][1], ab_re, ab_im))

            for e in range(L + 1):
                pr, pi = pw[e] if d == 0 else pw[L - e]
                cr, ci = _cmul(c_re, c_im, pr, pi)
                cl_re_scr[e * H:(e + 1) * H, :] = cr
                cl_im_scr[e * H:(e + 1) * H, :] = ci
            lo = 0 if d == 0 else H
            r = (lax.dot_general(bb_re, cl_re_scr[lo:lo + CHUNK_W, :], (((1,), (1,)), ((), ())),
                                 preferred_element_type=F32, precision=HIGHEST)
                 - lax.dot_general(bb_im, cl_im_scr[lo:lo + CHUNK_W, :], (((1,), (1,)), ((), ())),
                                   preferred_element_type=F32, precision=HIGHEST))
            wlo = H if d == 0 else 0
            wc_scr[:, d * STATE_W:d * STATE_W + P] = cl_re_scr[wlo:wlo + CHUNK_W, :]
            wc_scr[:, d * STATE_W + P:(d + 1) * STATE_W] = -cl_im_scr[wlo:wlo + CHUNK_W, :]
            for j in range(L):
                if d == 0:
                    blk = r if j == 0 else jnp.where(lane >= H * j, pltpu.roll(r, H * j, 1), 0.0)
                    sr, si = _cmul(pw[L - 1 - j][0], pw[L - 1 - j][1], bb_re, bb_im)
                else:
                    sft = (L - 1 - j) * H
                    blk = r if sft == 0 else jnp.where(lane < H * (j + 1), pltpu.roll(r, CHUNK_W - sft, 1), 0.0)
                    sr, si = _cmul(pw[j][0], pw[j][1], bb_re, bb_im)
                tp_scr[j * H:(j + 1) * H, :] += blk
                op_scr[j * H:(j + 1) * H, d * STATE_W:d * STATE_W + P] = sr
                op_scr[j * H:(j + 1) * H, d * STATE_W + P:(d + 1) * STATE_W] = si
                op_scr[j * H:(j + 1) * H, (2 + d) * STATE_W:(2 + d) * STATE_W + P] = si
                op_scr[j * H:(j + 1) * H, (2 + d) * STATE_W + P:(3 + d) * STATE_W] = sr
            pr, pi = pw[L]
            lam_ref[g, 2 * d:2 * d + 1, 0:P] = pr
            lam_ref[g, 2 * d:2 * d + 1, P:STATE_W] = pr
            lam_ref[g, 2 * d + 1:2 * d + 2, 0:P] = -pi
            lam_ref[g, 2 * d + 1:2 * d + 2, P:STATE_W] = pi
        ops_ref[g] = op_scr[...].astype(BF16)
        toep_ref[g] = tp_scr[...].astype(BF16)
        wct_ref[g] = wc_scr[...].astype(BF16)
        return carry

    lax.fori_loop(0, a_ref.shape[0], one_group, 0)


def _ssm_ops(a, dt, bt, c):
    gb = GB_SSM
    G = SSM_GROUPS
    blk = lambda shape: pl.BlockSpec((gb,) + shape, lambda i: (i,) + (0,) * len(shape))
    return pl.pallas_call(
        _ssm_ops_kernel,
        out_shape=[jax.ShapeDtypeStruct((G, CHUNK_W, 4 * STATE_W), BF16),
                   jax.ShapeDtypeStruct((G, CHUNK_W, CHUNK_W), BF16),
                   jax.ShapeDtypeStruct((G, CHUNK_W, 2 * STATE_W), BF16),
                   jax.ShapeDtypeStruct((G, 4, STATE_W), F32)],
        grid=(G // gb,),
        in_specs=[blk((2, 2, SSM_STATE)), blk((2, 1, SSM_STATE)), blk((2, 2, SSM_GROUP, SSM_STATE)),
                  blk((2, 2, SSM_GROUP, SSM_STATE))],
        out_specs=[blk((CHUNK_W, 4 * STATE_W)), blk((CHUNK_W, CHUNK_W)), blk((CHUNK_W, 2 * STATE_W)),
                   blk((4, STATE_W))],
        scratch_shapes=[pltpu.VMEM(((SSM_CHUNK + 1) * SSM_GROUP, SSM_STATE), F32),
                        pltpu.VMEM(((SSM_CHUNK + 1) * SSM_GROUP, SSM_STATE), F32),
                        pltpu.VMEM((CHUNK_W, 4 * STATE_W), F32),
                        pltpu.VMEM((CHUNK_W, CHUNK_W), F32),
                        pltpu.VMEM((CHUNK_W, 2 * STATE_W), F32)],
        compiler_params=_cparams(("arbitrary",)),
        name="ssm_ops",
    )(a, dt, bt, c)


GRANULES = LANES // SSM_GROUP


def _granule_transpose(arrs):
    a = list(arrs)
    lane = lax.broadcasted_iota(jnp.int32, a[0].shape, 1)
    for d in (4, 2, 1):
        upper = (lane & (d * SSM_GROUP)) != 0
        for s in range(GRANULES):
            if s & d:
                continue
            lo, hi = a[s], a[s + d]
            a[s] = jnp.where(upper, pltpu.roll(hi, d * SSM_GROUP, 1), lo)
            a[s + d] = jnp.where(upper, hi, pltpu.roll(lo, LANES - d * SSM_GROUP, 1))
    return a


def _ssm_in_kernel(x_ref, mod_ref, g_ref, u_ref, h_scr):
    m = mod_ref[0]
    h = _rms(x_ref[...], g_ref[...]) * (1.0 + m[1:2]) + m[0:1]
    mb = h_scr.shape[1] // SSM_CHUNK
    for g8 in range(SSM_GROUPS // GRANULES):
        h_scr[g8] = h[:, g8 * LANES:(g8 + 1) * LANES]
    for g8 in range(SSM_GROUPS // GRANULES):
        for half in range(SSM_CHUNK // GRANULES):
            a = [h_scr[g8, pl.ds(GRANULES * half + s, mb, stride=SSM_CHUNK), :] for s in range(GRANULES)]
            b = _granule_transpose(a)
            for g in range(GRANULES):
                u_ref[g8 * GRANULES + g, :, half * LANES:(half + 1) * LANES] = b[g].astype(BF16)


def _ssm_in(x, mod_l, g1):
    tm = TM_SSM_IN
    row = _mod_row(tm)
    return pl.pallas_call(
        _ssm_in_kernel,
        out_shape=jax.ShapeDtypeStruct((SSM_GROUPS, N_CHUNK_ROWS, CHUNK_W), BF16),
        grid=(N_TOK // tm,),
        in_specs=[pl.BlockSpec((tm, D_MODEL), lambda i: (i, 0)),
                  pl.BlockSpec((1, 6, D_MODEL), lambda i: (row(i), 0, 0)),
                  pl.BlockSpec((1, D_MODEL), lambda i: (0, 0))],
        out_specs=pl.BlockSpec((SSM_GROUPS, tm // SSM_CHUNK, CHUNK_W), lambda i: (0, i, 0)),
        scratch_shapes=[pltpu.VMEM((D_MODEL // LANES, tm, LANES), F32)],
        compiler_params=_cparams(("arbitrary",)),
        name="ssm_in",
    )(x, mod_l, g1.reshape(1, D_MODEL))


def _ssm_ends_kernel(u_ref, ops_ref, *s_refs):
    for g in range(u_ref.shape[0]):
        s = jnp.dot(u_ref[g], ops_ref[g], preferred_element_type=F32)
        for k, s_ref in enumerate(s_refs):
            s_ref[g] = s[:, k * STATE_W:(k + 1) * STATE_W]


def _ssm_ends(u, ops):
    gb = GB_SSM
    G = SSM_GROUPS
    return pl.pallas_call(
        _ssm_ends_kernel,
        out_shape=[jax.ShapeDtypeStruct((G, N_CHUNK_ROWS, STATE_W), F32)] * 4,
        grid=(G // gb,),
        in_specs=[pl.BlockSpec((gb, N_CHUNK_ROWS, CHUNK_W), lambda i: (i, 0, 0)),
                  pl.BlockSpec((gb, CHUNK_W, 4 * STATE_W), lambda i: (i, 0, 0))],
        out_specs=[pl.BlockSpec((gb, N_CHUNK_ROWS, STATE_W), lambda i: (i, 0, 0))] * 4,
        compiler_params=_cparams(("arbitrary",)),
        name="ssm_chunk_ends",
    )(u, ops)


def _ssm_carry_kernel(sf_ref, sb_ref, sfs_ref, sbs_ref, lam_ref, h0_ref, xf_ref, xb_ref, fin_ref):
    n_g = sf_ref.shape[0]

    def swap(t):
        return pltpu.roll(t, SSM_STATE, 1)

    def run(n_steps, row0, rows, stride, x0):
        lam = [[jnp.broadcast_to(lam_ref[g, k:k + 1, :], (rows, STATE_W)) for k in range(4)] for g in range(n_g)]

        def body(c, carry):
            rf = pl.ds(row0 + c, rows, stride=stride)
            rb = pl.ds(row0 + (n_steps - 1 - c), rows, stride=stride)
            out = []
            for g in range(n_g):
                xf, xfs, xb, xbs = carry[4 * g:4 * g + 4]
                a1f, a2f, a1b, a2b = lam[g]
                xf_ref[g, rf, :] = xf
                xb_ref[g, rb, :] = xb
                out += [a1f * xf + a2f * xfs + sf_ref[g, rf, :], a1f * xfs - a2f * xf + sfs_ref[g, rf, :],
                        a1b * xb + a2b * xbs + sb_ref[g, rb, :], a1b * xbs - a2b * xb + sbs_ref[g, rb, :]]
            return tuple(out)

        init = []
        for g in range(n_g):
            init += [x0[g][0], swap(x0[g][0]), x0[g][1], swap(x0[g][1])]
        return lax.fori_loop(0, n_steps, body, tuple(init), unroll=2)

    zero = jnp.zeros((BATCH, STATE_W), F32)
    n_p = SEQ // SSM_CHUNK
    fin = run(n_p, 0, BATCH, n_p, [(zero, zero)] * n_g)
    for g in range(n_g):
        fin_ref[g, 0] = fin[4 * g]
        fin_ref[g, 1] = fin[4 * g + 2]
    n_s = DEC_SEQ // SSM_CHUNK
    run(n_s, N_CHUNK_PROMPT, DEC_BATCH, n_s, [(h0_ref[g, 0], h0_ref[g, 1]) for g in range(n_g)])


def _ssm_carry(sf, sb, sfs, sbs, lam, h0):
    gb = GB_CARRY
    G = SSM_GROUPS
    st = pl.BlockSpec((gb, N_CHUNK_ROWS, STATE_W), lambda i: (i, 0, 0))
    return pl.pallas_call(
        _ssm_carry_kernel,
        out_shape=[jax.ShapeDtypeStruct(sf.shape, F32)] * 2 + [jax.ShapeDtypeStruct((G, 2, BATCH, STATE_W), F32)],
        grid=(G // gb,),
        in_specs=[st, st, st, st, pl.BlockSpec((gb, 4, STATE_W), lambda i: (i, 0, 0)),
                  pl.BlockSpec((gb, 2, DEC_BATCH, STATE_W), lambda i: (i, 0, 0, 0))],
        out_specs=[st, st, pl.BlockSpec((gb, 2, BATCH, STATE_W), lambda i: (i, 0, 0, 0))],
        compiler_params=_cparams(("arbitrary",)),
        name="ssm_carry",
    )(sf, sb, sfs, sbs, lam, h0)


def _ssm_out_kernel(u_ref, toep_ref, xf_ref, xb_ref, wct_ref, y_ref, y_scr):
    for g in range(GRANULES):
        y = jnp.dot(u_ref[g], toep_ref[g], preferred_element_type=F32)
        for d, x_ref in enumerate((xf_ref, xb_ref)):
            xin = x_ref[g].astype(BF16)
            w = wct_ref[g, :, d * STATE_W:(d + 1) * STATE_W]
            y += lax.dot_general(xin, w, (((1,), (1,)), ((), ())), preferred_element_type=F32)
        y_scr[g] = y
    rb = OUT_ROW_BLOCK
    for r in range(N_CHUNK_ROWS // rb):
        for half in range(SSM_CHUNK // GRANULES):
            a = [y_scr[g, r * rb:(r + 1) * rb, half * LANES:(half + 1) * LANES] for g in range(GRANULES)]
            b = _granule_transpose(a)
            for s in range(GRANULES):
                y_ref[pl.ds(r * rb * SSM_CHUNK + GRANULES * half + s, rb, stride=SSM_CHUNK), :] = b[s]


def _ssm_out(u, toep, xf, xb, wct):
    gb = GRANULES
    G = SSM_GROUPS
    return pl.pallas_call(
        _ssm_out_kernel,
        out_shape=jax.ShapeDtypeStruct((N_TOK, D_MODEL), F32),
        grid=(G // gb,),
        in_specs=[pl.BlockSpec((gb, N_CHUNK_ROWS, CHUNK_W), lambda i: (i, 0, 0)),
                  pl.BlockSpec((gb, CHUNK_W, CHUNK_W), lambda i: (i, 0, 0)),
                  pl.BlockSpec((gb, N_CHUNK_ROWS, STATE_W), lambda i: (i, 0, 0)),
                  pl.BlockSpec((gb, N_CHUNK_ROWS, STATE_W), lambda i: (i, 0, 0)),
                  pl.BlockSpec((gb, CHUNK_W, 2 * STATE_W), lambda i: (i, 0, 0))],
        out_specs=pl.BlockSpec((N_TOK, LANES), lambda i: (0, i)),
        scratch_shapes=[pltpu.VMEM((gb, N_CHUNK_ROWS, CHUNK_W), F32)],
        compiler_params=_cparams(("arbitrary",)),
        name="ssm_chunk_out",
    )(u, toep, xf, xb, wct)


def _ssm_post_kernel(x_ref, mod_ref, g_ref, y_ref, d_ref, w_ref, b_ref, o_ref):
    m = mod_ref[0]
    x = x_ref[...]
    h = _rms(x, g_ref[...]) * (1.0 + m[1:2]) + m[0:1]
    y = h * d_ref[...] + y_ref[...]
    gl = jax.nn.gelu(y).astype(BF16)
    z = jnp.dot(gl, w_ref[...], preferred_element_type=F32) + b_ref[...]
    o_ref[...] = x + m[2:3] * (z[:, :D_MODEL] * jax.nn.sigmoid(z[:, D_MODEL:]))


def _ssm_post(x, mod_l, g1, y, d_skip, w_glu, b_glu):
    tm = TM_SSM
    row = _mod_row(tm)
    full = lambda a: pl.BlockSpec(a.shape, lambda i: (0,) * a.ndim)
    d2, b2 = d_skip.reshape(1, D_MODEL), b_glu.reshape(1, 2 * D_MODEL)
    g2 = g1.reshape(1, D_MODEL)
    return pl.pallas_call(
        _ssm_post_kernel,
        out_shape=jax.ShapeDtypeStruct((N_TOK, D_MODEL), F32),
        grid=(N_TOK // tm,),
        in_specs=[pl.BlockSpec((tm, D_MODEL), lambda i: (i, 0)),
                  pl.BlockSpec((1, 6, D_MODEL), lambda i: (row(i), 0, 0)),
                  full(g2), pl.BlockSpec((tm, D_MODEL), lambda i: (i, 0)), full(d2), full(w_glu), full(b2)],
        out_specs=pl.BlockSpec((tm, D_MODEL), lambda i: (i, 0)),
        compiler_params=_cparams(("arbitrary",)),
        name="ssm_post",
    )(x, mod_l, g2, y, d2, w_glu, b2)


def _rope_tables():
    rows = DEC_SEQ // GRID_W
    t = np.arange(DEC_SEQ)
    row, col = (t // GRID_W).astype(np.float32), (t % GRID_W).astype(np.float32)

    def table(rot_dim, lane0):
        n_freq = rot_dim // 4
        inv_freq = jnp.asarray(ROPE_THETA, F32) ** (-jnp.arange(n_freq, dtype=F32) / n_freq)
        ang_row = jnp.asarray(row)[:, None] * inv_freq
        ang_col = jnp.asarray(col)[:, None] * inv_freq
        ang = jnp.concatenate([ang_row, ang_row, ang_col, ang_col], axis=1)
        sign = jnp.tile(jnp.concatenate([-jnp.ones(n_freq, F32), jnp.ones(n_freq, F32)]), 2)
        cos = jnp.ones((DEC_SEQ, HEAD_PAD), F32).at[:, lane0:lane0 + rot_dim].set(jnp.cos(ang))
        sin = jnp.zeros((DEC_SEQ, HEAD_PAD), F32).at[:, lane0:lane0 + rot_dim].set(jnp.sin(ang) * sign)
        return cos, sin

    assert rows * GRID_W == DEC_SEQ
    return table(MLA_ROPE, MLA_NOPE) + table(GQA_DIM, 0)


def _pad_heads(w, n_heads, dim):
    lead = w.shape[:-1]
    w = w.reshape(lead + (n_heads, dim))
    w = jnp.pad(w, [(0, 0)] * len(lead) + [(0, 0), (0, HEAD_PAD - dim)])
    return w.reshape(lead + (n_heads * HEAD_PAD,))


def _attn_weights(w_in, g_qa, g_kva, w_uq, w_ukv, g_mq, g_mk, g_gq, g_gk, w_out):
    o1, o2, o3 = Q_LORA, Q_LORA + KV_LORA, Q_LORA + KV_LORA + MLA_ROPE
    o4 = o3 + GQA_HEADS * GQA_DIM
    o5 = o4 + GQA_KV_HEADS * GQA_DIM
    kpe = jnp.pad(w_in[:, o2:o3], ((0, 0), (MLA_NOPE, HEAD_PAD - MLA_QK)))
    w_in_ext = jnp.concatenate([w_in[:, :o2], kpe, _pad_heads(w_in[:, o3:o4], GQA_HEADS, GQA_DIM),
                                _pad_heads(w_in[:, o4:o5], GQA_KV_HEADS, GQA_DIM), w_in[:, o5:]], axis=1)
    ukv = w_ukv.reshape(KV_LORA, MLA_HEADS, MLA_NOPE + MLA_V)
    w_ukv_perm = jnp.concatenate([_pad_heads(ukv[:, :, :MLA_NOPE].reshape(KV_LORA, -1), MLA_HEADS, MLA_NOPE),
                                  ukv[:, :, MLA_NOPE:].reshape(KV_LORA, -1)], axis=1)
    pad_g = lambda g: jnp.pad(g, (0, HEAD_PAD - g.shape[0])).reshape(1, HEAD_PAD)
    return dict(w_in=w_in_ext.astype(BF16), g_qa=g_qa.reshape(1, -1), g_kva=g_kva.reshape(1, -1),
                w_uq=_pad_heads(w_uq, MLA_HEADS, MLA_QK).astype(BF16), w_ukv=w_ukv_perm.astype(BF16),
                g_mq=pad_g(g_mq), g_mk=pad_g(g_mk), g_gq=pad_g(g_gq), g_gk=pad_g(g_gk),
                w_out=w_out.astype(BF16))


def _attn_layer(x, mod_l, g1, aw, tabs, cache):
    q, k, v, ckv, kpe, gk, gv = _attn_pre(x, mod_l, g1, aw, tabs)
    c_ckv, c_krope, c_gk, c_gv = cache
    n_c = DEC_BATCH * PAST_LEN
    kpe_c = jnp.pad(c_krope.reshape(n_c, MLA_ROPE), ((0, 0), (MLA_NOPE, HEAD_PAD - MLA_QK)))
    km_c, vm_c = _cache_kv(c_ckv.reshape(n_c, KV_LORA), kpe_c, aw)
    k_c = jnp.concatenate([km_c, _pad_heads(c_gk.reshape(n_c, -1), GQA_KV_HEADS, GQA_DIM).astype(BF16)], axis=1)
    v_c = jnp.concatenate([vm_c, c_gv.reshape(n_c, -1).astype(BF16)], axis=1)
    x = _attention(q, [(k, v, SEQ, 0)], x, mod_l, aw["w_out"], SEQ, BATCH, SEQ, 0, lambda b: 0)
    x = _attention(q, [(k_c, v_c, PAST_LEN, 0), (k, v, DEC_SEQ, N_PROMPT // DEC_SEQ)], x, mod_l, aw["w_out"],
                   TQ_SAMPLE, DEC_BATCH, DEC_SEQ, N_PROMPT, lambda b: 1 + b)
    new = (ckv[:N_PROMPT].reshape(BATCH, SEQ, KV_LORA),
           kpe[:N_PROMPT, MLA_NOPE:MLA_QK].reshape(BATCH, SEQ, MLA_ROPE),
           gk[:N_PROMPT].reshape(BATCH, SEQ, GQA_KV_HEADS, HEAD_PAD)[..., :GQA_DIM],
           gv[:N_PROMPT].reshape(BATCH, SEQ, GQA_KV_HEADS, GQA_DIM))
    return x, new


def _ssm_layer(x, mod_l, g1, a_re, a_im, log_dt, b_re, b_im, c_re, c_im, d_skip, w_glu, b_glu, state0):
    G, P, H = SSM_GROUPS, SSM_STATE, SSM_GROUP
    a = jnp.stack([a_re, a_im], axis=2).transpose(1, 0, 2, 3)
    dt = jnp.broadcast_to(log_dt.transpose(1, 0)[:, :, None, None], (G, 2, 1, P))
    bt = jnp.stack([b_re, b_im], axis=2).transpose(1, 0, 2, 4, 3)
    c = jnp.stack([c_re, c_im], axis=2).transpose(1, 0, 2, 3, 4)
    ops, toep, wct, lam = _ssm_ops(a, dt, bt, c)

    u = _ssm_in(x, mod_l, g1)
    sf, sb, sfs, sbs = _ssm_ends(u, ops)
    h0 = state0.transpose(2, 1, 0, 4, 3).reshape(G, 2, DEC_BATCH, STATE_W)
    xf, xb, fin = _ssm_carry(sf, sb, sfs, sbs, lam, h0)
    y = _ssm_out(u, toep, xf, xb, wct)
    x = _ssm_post(x, mod_l, g1, y, d_skip, w_glu.astype(BF16), b_glu)
    fin = fin.reshape(G, 2, BATCH, 2, P).transpose(2, 1, 0, 4, 3)
    return x, fin


def kernel(x_prompt, x_sample, c, cache_mla_ckv, cache_mla_krope, cache_gqa_k, cache_gqa_v, state_ssm, c_ctx,
           norm1_g, norm2_g, w_mod, b_mod,
           attn_w_in, attn_qa_norm_g, attn_kva_norm_g, attn_w_uq, attn_w_ukv,
           attn_mla_q_norm_g, attn_mla_k_norm_g, attn_gqa_q_norm_g, attn_gqa_k_norm_g, attn_w_out,
           ssm_a_re, ssm_a_im, ssm_log_dt, ssm_b_re, ssm_b_im, ssm_c_re, ssm_c_im, ssm_d, ssm_w_glu, ssm_b_glu,
           ffn_w_up, ffn_conv_w, ffn_conv_b, ffn_w_down):
    x = jnp.concatenate([x_prompt.reshape(N_PROMPT, D_MODEL), x_sample.reshape(N_SAMPLE, D_MODEL)], axis=0)
    cond8 = jnp.concatenate([c_ctx[None, :], c, jnp.zeros((8 - 1 - DEC_BATCH, D_MODEL), F32)], axis=0)
    mod = _modulation(cond8, w_mod, b_mod).reshape(DEPTH, 8, 6, D_MODEL)
    tabs = _rope_tables()

    new_attn, new_ssm = [], []
    for i in range(DEPTH):
        j = i // 2
        if i % 2 == 0:
            aw = _attn_weights(attn_w_in[j], attn_qa_norm_g[j], attn_kva_norm_g[j], attn_w_uq[j], attn_w_ukv[j],
                               attn_mla_q_norm_g[j], attn_mla_k_norm_g[j], attn_gqa_q_norm_g[j],
                               attn_gqa_k_norm_g[j], attn_w_out[j])
            cache = (cache_mla_ckv[:, j], cache_mla_krope[:, j], cache_gqa_k[:, j], cache_gqa_v[:, j])
            x, new = _attn_layer(x, mod[i], norm1_g[i], aw, tabs, cache)
            new_attn.append(new)
        else:
            x, fin = _ssm_layer(x, mod[i], norm1_g[i], ssm_a_re[j], ssm_a_im[j], ssm_log_dt[j], ssm_b_re[j],
                                ssm_b_im[j], ssm_c_re[j], ssm_c_im[j], ssm_d[j], ssm_w_glu[j], ssm_b_glu[j],
                                state_ssm[:, j])
            new_ssm.append(fin)
        x = _conv_ffn(x, mod[i], norm2_g[i], ffn_w_up[i].astype(BF16), ffn_conv_w[i], ffn_conv_b[i],
                      ffn_w_down[i].astype(BF16))

    outs = [jnp.stack([n[k] for n in new_attn], axis=1) for k in range(4)]
    return (x[:N_PROMPT].reshape(BATCH, SEQ, D_MODEL), x[N_PROMPT:].reshape(DEC_BATCH, DEC_SEQ, D_MODEL),
            outs[0], outs[1], outs[2], outs[3], jnp.stack(new_ssm, axis=1))
```

```python
import functools
import math

import jax
import jax.numpy as jnp
import numpy as np
from jax import lax
from jax.experimental import pallas as pl
from jax.experimental.pallas import tpu as pltpu

F32 = jnp.float32
BF16 = jnp.bfloat16
HIGHEST = lax.Precision.HIGHEST

LANES = 128
BF16_SUBLANES = 16
VMEM_LIMIT_BYTES = 56 * 1024 * 1024

D_MODEL = 1024
BATCH = 16
SEQ = 256
DEPTH = 4
DEC_BATCH = 2
DEC_SEQ = 2048
PAST_LEN = 256
GRID_W = 64
MLA_HEADS = 8
Q_LORA = 384
KV_LORA = 256
MLA_NOPE = 64
MLA_ROPE = 32
MLA_QK = MLA_NOPE + MLA_ROPE
MLA_V = 64
GQA_HEADS = 8
GQA_KV_HEADS = 2
GQA_DIM = 64
GQA_REP = GQA_HEADS // GQA_KV_HEADS
ROPE_THETA = 10000.0
SSM_GROUP = 16
SSM_GROUPS = D_MODEL // SSM_GROUP
SSM_STATE = 64
D_FF = 2816
EPS = 1e-6

N_PROMPT = BATCH * SEQ
N_SAMPLE = DEC_BATCH * DEC_SEQ
N_TOK = N_PROMPT + N_SAMPLE
N_HEADS = MLA_HEADS + GQA_HEADS
HEAD_PAD = LANES
KV_LEN_SAMPLE = PAST_LEN + DEC_SEQ

IN_Q = 0
IN_CKV = IN_Q + Q_LORA
IN_KPE = IN_CKV + KV_LORA
IN_GQ = IN_KPE + HEAD_PAD
IN_GK = IN_GQ + GQA_HEADS * HEAD_PAD
IN_GV = IN_GK + GQA_KV_HEADS * HEAD_PAD
IN_EXT = IN_GV + GQA_KV_HEADS * GQA_DIM

K_ALL = (MLA_HEADS + GQA_KV_HEADS) * HEAD_PAD
V_ALL = MLA_HEADS * MLA_V + GQA_KV_HEADS * GQA_DIM

SSM_CHUNK = 16
CHUNK_W = SSM_CHUNK * SSM_GROUP
N_CHUNK_ROWS = N_TOK // SSM_CHUNK
N_CHUNK_PROMPT = N_PROMPT // SSM_CHUNK
STATE_W = 2 * SSM_STATE

TM_FFN = 512
TN_FFN = 256
FFN_HALO = BF16_SUBLANES
TM_PRE = 512
TQ_SAMPLE = 256
TM_SSM = 1024
TM_SSM_IN = 512
OUT_ROW_BLOCK = 32
GB_CARRY = 4
GB_CARRY_BLOCK = 16
TN_MOD = 1536
GB_SSM = 8


def _cparams(sem):
    return pltpu.CompilerParams(dimension_semantics=sem, vmem_limit_bytes=VMEM_LIMIT_BYTES)


def _rms(x, g):
    return x * lax.rsqrt(jnp.mean(x * x, axis=-1, keepdims=True) + EPS) * g


def _mod_row(tm):
    n_prompt_tiles = N_PROMPT // tm
    tiles_per_seq = DEC_SEQ // tm

    def row(i):
        return jnp.where(i < n_prompt_tiles, 0, 1 + (i - n_prompt_tiles) // tiles_per_seq)

    return row


def _mod_kernel(cond_ref, w_ref, b_ref, o_ref):
    a = jax.nn.silu(cond_ref[...])
    o_ref[0] = jnp.dot(a, w_ref[0], preferred_element_type=F32, precision=HIGHEST) + b_ref[0]


def _modulation(cond8, w_mod, b_mod):
    n = w_mod.shape[-1]
    return pl.pallas_call(
        _mod_kernel,
        out_shape=jax.ShapeDtypeStruct((DEPTH, 8, n), F32),
        grid=(DEPTH, n // TN_MOD),
        in_specs=[
            pl.BlockSpec((8, D_MODEL), lambda l, j: (0, 0)),
            pl.BlockSpec((1, D_MODEL, TN_MOD), lambda l, j: (l, 0, j)),
            pl.BlockSpec((1, 1, TN_MOD), lambda l, j: (l, 0, j)),
        ],
        out_specs=pl.BlockSpec((1, 8, TN_MOD), lambda l, j: (l, 0, j)),
        compiler_params=_cparams(("arbitrary", "arbitrary")),
        name="modulation",
    )(cond8, w_mod, b_mod.reshape(DEPTH, 1, n))


def _ffn_kernel(xp_ref, x_ref, xn_ref, mod_ref, g_ref, wup_ref, cw_ref, cb_ref, wd_ref, o_ref,
                h_scr, act_scr, z_scr, *, tile0):
    i = pl.program_id(0) + tile0
    tm = x_ref.shape[0]
    tn = TN_FFN
    m = mod_ref[0]
    sh, sc, gt = m[3:4], m[4:5], m[5:6]
    g = g_ref[...]

    def hmod(x):
        return (_rms(x, g) * (1.0 + sc) + sh).astype(BF16)

    h_scr[0:FFN_HALO] = hmod(xp_ref[...])
    h_scr[FFN_HALO:FFN_HALO + tm] = hmod(x_ref[...])
    h_scr[FFN_HALO + tm:] = hmod(xn_ref[...])
    seq_len = jnp.where(i * tm < N_PROMPT, SEQ, DEC_SEQ)
    pos = (i * tm + lax.broadcasted_iota(jnp.int32, (tm, 1), 0)) & (seq_len - 1)
    eg = BF16_SUBLANES
    bounds = range(0, tm + 1, SEQ)
    cuts = sorted({0, tm} | {b + o for b in bounds for o in (-eg, eg) if 0 < b + o < tm})
    segments = [(r0, r1, any(r0 <= b < r1 or r0 < b <= r1 for b in bounds)) for r0, r1 in zip(cuts[:-1], cuts[1:])]

    def conv(slot, k, col, r0, r1, masked):
        cw = cw_ref[:, col:col + tn]
        zp = z_scr[slot, k, FFN_HALO - 1 + r0:FFN_HALO - 1 + r1, :]
        zc = z_scr[slot, k, FFN_HALO + r0:FFN_HALO + r1, :]
        zn = z_scr[slot, k, FFN_HALO + 1 + r0:FFN_HALO + 1 + r1, :]
        if masked:
            zp = jnp.where(pos[r0:r1] == 0, 0.0, zp)
            zn = jnp.where(pos[r0:r1] == seq_len - 1, 0.0, zn)
        return zp * cw[0:1] + zc * cw[1:2] + zn * cw[2:3] + cb_ref[:, col:col + tn]

    h = h_scr[...]
    for jc in range(D_FF // tn):
        ca, cb = jc * tn, D_FF + jc * tn
        slot = jc % 2
        z_scr[slot, 0] = jnp.dot(h, wup_ref[:, ca:ca + tn], preferred_element_type=F32)
        z_scr[slot, 1] = jnp.dot(h, wup_ref[:, cb:cb + tn], preferred_element_type=F32)
        for r0, r1, masked in segments:
            act_scr[r0:r1, ca:ca + tn] = (jax.nn.silu(conv(slot, 0, ca, r0, r1, masked))
                                          * conv(slot, 1, cb, r0, r1, masked)).astype(BF16)
    o_ref[...] = x_ref[...] + gt * jnp.dot(act_scr[...], wd_ref[...], preferred_element_type=F32)


def _conv_ffn(x, mod_l, g, layer, w_up, conv_w, conv_b, w_down, tok0=0, n_tok=N_TOK):
    tm, halo = TM_FFN, FFN_HALO
    row = _mod_row(tm)
    n_halo_blocks = N_TOK // halo
    tile0 = tok0 // tm
    resident = lambda a: pl.BlockSpec((None,) + a.shape[1:], lambda i: (layer,) + (0,) * (a.ndim - 1),
                                      pipeline_mode=pl.Buffered(1))
    cb3 = conv_b.reshape(DEPTH, 1, -1)
    return pl.pallas_call(
        functools.partial(_ffn_kernel, tile0=tile0),
        out_shape=jax.ShapeDtypeStruct((n_tok, D_MODEL), F32),
        grid=(n_tok // tm,),
        in_specs=[
            pl.BlockSpec((halo, D_MODEL), lambda i: (jnp.maximum((i + tile0) * (tm // halo) - 1, 0), 0)),
            pl.BlockSpec((tm, D_MODEL), lambda i: (i + tile0, 0)),
            pl.BlockSpec((halo, D_MODEL),
                         lambda i: (jnp.minimum((i + tile0 + 1) * (tm // halo), n_halo_blocks - 1), 0)),
            pl.BlockSpec((1, 6, D_MODEL), lambda i: (row(i + tile0), 0, 0)),
            pl.BlockSpec((1, D_MODEL), lambda i: (0, 0)),
            resident(w_up), resident(conv_w), resident(cb3), resident(w_down),
        ],
        out_specs=pl.BlockSpec((tm, D_MODEL), lambda i: (i, 0)),
        scratch_shapes=[pltpu.VMEM((tm + 2 * halo, D_MODEL), BF16), pltpu.VMEM((tm, D_FF), BF16),
                        pltpu.VMEM((2, 2, tm + 2 * halo, TN_FFN), F32)],
        compiler_params=_cparams(("arbitrary",)),
        name="conv_ffn",
    )(x, x, x, mod_l, g.reshape(1, D_MODEL), w_up, conv_w, cb3, w_down)


def _rope(x, cos, sin, half):
    lane = lax.broadcasted_iota(jnp.int32, x.shape, 1)
    first = ((lane // half) & 1) == 0
    partner = jnp.where(first, pltpu.roll(x, LANES - half, 1), pltpu.roll(x, half, 1))
    return x * cos + partner * sin


def _head_norm(xh, g, dim):
    return xh * lax.rsqrt(jnp.sum(xh * xh, axis=-1, keepdims=True) * (1.0 / dim) + EPS) * g


def _mla_kv_heads(ckv_bf, kpe, w_ukv_ref, g_mk, cos_m, sin_m, k_ref, v_ref):
    kv = jnp.dot(ckv_bf, w_ukv_ref[...], preferred_element_type=F32)
    for h in range(MLA_HEADS):
        kh = kv[:, h * HEAD_PAD:(h + 1) * HEAD_PAD] + kpe
        kh = _head_norm(kh, g_mk, MLA_QK)
        if cos_m is not None:
            kh = _rope(kh, cos_m, sin_m, MLA_ROPE // 4)
        k_ref[:, h * HEAD_PAD:(h + 1) * HEAD_PAD] = kh.astype(BF16)
    v_ref[:, 0:MLA_HEADS * MLA_V] = kv[:, MLA_HEADS * HEAD_PAD:].astype(BF16)


def _attn_pre_kernel(x_ref, mod_ref, g1_ref, w_in_ref, g_qa_ref, g_kva_ref, w_uq_ref, w_ukv_ref,
                     g_mq_ref, g_mk_ref, g_gq_ref, g_gk_ref, cos_m_ref, sin_m_ref, cos_g_ref, sin_g_ref,
                     q_ref, k_ref, v_ref, ckv_ref, kpe_ref, gk_ref, gv_ref):
    m = mod_ref[0]
    sh, sc = m[0:1], m[1:2]
    h = (_rms(x_ref[...], g1_ref[...]) * (1.0 + sc) + sh).astype(BF16)
    p = jnp.dot(h, w_in_ref[...], preferred_element_type=F32)
    q_c = _rms(p[:, IN_Q:IN_CKV], g_qa_ref[...]).astype(BF16)
    ckv = _rms(p[:, IN_CKV:IN_KPE], g_kva_ref[...])
    kpe = p[:, IN_KPE:IN_GQ]
    ckv_ref[...] = ckv
    kpe_ref[...] = kpe
    gv = p[:, IN_GV:IN_EXT]
    gv_ref[...] = gv
    v_ref[:, MLA_HEADS * MLA_V:] = gv.astype(BF16)
    qm = jnp.dot(q_c, w_uq_ref[...], preferred_element_type=F32)

    def heads(rope):
        cos_m, sin_m = (cos_m_ref[...], sin_m_ref[...]) if rope else (None, None)
        cos_g, sin_g = (cos_g_ref[...], sin_g_ref[...]) if rope else (None, None)
        g_mq = g_mq_ref[...]
        for hd in range(MLA_HEADS):
            qh = _head_norm(qm[:, hd * HEAD_PAD:(hd + 1) * HEAD_PAD], g_mq, MLA_QK)
            if rope:
                qh = _rope(qh, cos_m, sin_m, MLA_ROPE // 4)
            q_ref[:, hd * HEAD_PAD:(hd + 1) * HEAD_PAD] = (qh * (1.0 / math.sqrt(MLA_QK))).astype(BF16)
        _mla_kv_heads(ckv.astype(BF16), kpe, w_ukv_ref, g_mk_ref[...], cos_m, sin_m, k_ref, v_ref)
        g_gq = g_gq_ref[...]
        for hd in range(GQA_HEADS):
            qh = _head_norm(p[:, IN_GQ + hd * HEAD_PAD:IN_GQ + (hd + 1) * HEAD_PAD], g_gq, GQA_DIM)
            if rope:
                qh = _rope(qh, cos_g, sin_g, GQA_DIM // 4)
            q_ref[:, (MLA_HEADS + hd) * HEAD_PAD:(MLA_HEADS + hd + 1) * HEAD_PAD] = (
                qh * (1.0 / math.sqrt(GQA_DIM))).astype(BF16)
        g_gk = g_gk_ref[...]
        for hd in range(GQA_KV_HEADS):
            kh = _head_norm(p[:, IN_GK + hd * HEAD_PAD:IN_GK + (hd + 1) * HEAD_PAD], g_gk, GQA_DIM)
            gk_ref[:, hd * HEAD_PAD:(hd + 1) * HEAD_PAD] = kh
            if rope:
                kh = _rope(kh, cos_g, sin_g, GQA_DIM // 4)
            k_ref[:, (MLA_HEADS + hd) * HEAD_PAD:(MLA_HEADS + hd + 1) * HEAD_PAD] = kh.astype(BF16)

    is_latent = pl.program_id(0) >= N_PROMPT // x_ref.shape[0]
    pl.when(is_latent)(lambda: heads(True))
    pl.when(jnp.logical_not(is_latent))(lambda: heads(False))


def _attn_pre(x, mod_l, g1, aw, tabs):
    tm = TM_PRE
    row = _mod_row(tm)
    full = lambda a: pl.BlockSpec(a.shape, lambda i: (0,) * a.ndim)
    tok = lambda w: pl.BlockSpec((tm, w), lambda i: (i, 0))
    pos_block = lambda i: (jnp.maximum(i - N_PROMPT // tm, 0) % (DEC_SEQ // tm), 0)
    consts = [g1.reshape(1, D_MODEL), aw["w_in"], aw["g_qa"], aw["g_kva"], aw["w_uq"], aw["w_ukv"],
              aw["g_mq"], aw["g_mk"], aw["g_gq"], aw["g_gk"]]
    return pl.pallas_call(
        _attn_pre_kernel,
        out_shape=[
            jax.ShapeDtypeStruct((N_TOK, N_HEADS * HEAD_PAD), BF16),
            jax.ShapeDtypeStruct((N_TOK, K_ALL), BF16),
            jax.ShapeDtypeStruct((N_TOK, V_ALL), BF16),
            jax.ShapeDtypeStruct((N_TOK, KV_LORA), F32),
            jax.ShapeDtypeStruct((N_TOK, HEAD_PAD), F32),
            jax.ShapeDtypeStruct((N_TOK, GQA_KV_HEADS * HEAD_PAD), F32),
            jax.ShapeDtypeStruct((N_TOK, GQA_KV_HEADS * GQA_DIM), F32),
        ],
        grid=(N_TOK // tm,),
        in_specs=[tok(D_MODEL), pl.BlockSpec((1, 6, D_MODEL), lambda i: (row(i), 0, 0))]
        + [full(a) for a in consts] + [pl.BlockSpec((tm, HEAD_PAD), pos_block)] * 4,
        out_specs=[tok(N_HEADS * HEAD_PAD), tok(K_ALL), tok(V_ALL), tok(KV_LORA), tok(HEAD_PAD),
                   tok(GQA_KV_HEADS * HEAD_PAD), tok(GQA_KV_HEADS * GQA_DIM)],
        compiler_params=_cparams(("arbitrary",)),
        name="attn_pre",
    )(x, mod_l, *consts, *tabs)


def _cache_kv_kernel(ckv_ref, kpe_ref, w_ukv_ref, g_mk_ref, k_ref, v_ref):
    _mla_kv_heads(ckv_ref[...].astype(BF16), kpe_ref[...], w_ukv_ref, g_mk_ref[...], None, None, k_ref, v_ref)


def _cache_kv(ckv, kpe_pad, aw):
    n = ckv.shape[0]
    return pl.pallas_call(
        _cache_kv_kernel,
        out_shape=[jax.ShapeDtypeStruct((n, MLA_HEADS * HEAD_PAD), BF16),
                   jax.ShapeDtypeStruct((n, MLA_HEADS * MLA_V), BF16)],
        name="cache_kv",
    )(ckv, kpe_pad, aw["w_ukv"], aw["g_mk"])


def _attn_kernel(*refs, n_seg):
    q_ref, kv_refs = refs[0], refs[1:1 + 2 * n_seg]
    x_ref, mod_ref, wo_ref, o_ref, oh_scr = refs[1 + 2 * n_seg:]
    for hd in range(N_HEADS):
        if hd < MLA_HEADS:
            kc, vc = hd * HEAD_PAD, hd * MLA_V
        else:
            kvh = (hd - MLA_HEADS) // GQA_REP
            kc, vc = (MLA_HEADS + kvh) * HEAD_PAD, MLA_HEADS * MLA_V + kvh * GQA_DIM
        q = q_ref[:, hd * HEAD_PAD:(hd + 1) * HEAD_PAD]
        ss = [lax.dot_general(q, kv_refs[2 * j][:, kc:kc + HEAD_PAD], (((1,), (1,)), ((), ())),
                              preferred_element_type=F32) for j in range(n_seg)]
        m = functools.reduce(jnp.maximum, [jnp.max(s, axis=-1, keepdims=True) for s in ss])
        ps = [jnp.exp(s - m) for s in ss]
        den = sum(jnp.sum(p, axis=-1, keepdims=True) for p in ps)
        o = sum(jnp.dot(p.astype(BF16), kv_refs[2 * j + 1][:, vc:vc + MLA_V], preferred_element_type=F32)
                for j, p in enumerate(ps))
        oh_scr[:, hd * MLA_V:(hd + 1) * MLA_V] = (o / den).astype(BF16)
    gt = mod_ref[0][2:3]
    o_ref[...] = x_ref[...] + gt * jnp.dot(oh_scr[...], wo_ref[...], preferred_element_type=F32)


def _attention(q, segs, x, mod_l, w_out, tq, n_batch, t_len, tok0, mod_row):
    tiles = t_len // tq
    qmap = lambda bi, qi: (tok0 // tq + bi * tiles + qi, 0)
    kv_specs, kv_args = [], []
    for k, v, rows, blk0 in segs:
        kv_specs += [pl.BlockSpec((rows, K_ALL), lambda bi, qi, blk0=blk0: (blk0 + bi, 0)),
                     pl.BlockSpec((rows, V_ALL), lambda bi, qi, blk0=blk0: (blk0 + bi, 0))]
        kv_args += [k, v]
    return pl.pallas_call(
        functools.partial(_attn_kernel, n_seg=len(segs)),
        out_shape=jax.ShapeDtypeStruct((N_TOK, D_MODEL), F32),
        grid=(n_batch, tiles),
        in_specs=[pl.BlockSpec((tq, N_HEADS * HEAD_PAD), qmap)] + kv_specs + [
            pl.BlockSpec((tq, D_MODEL), qmap),
            pl.BlockSpec((1, 6, D_MODEL), lambda bi, qi: (mod_row(bi), 0, 0)),
            pl.BlockSpec((D_MODEL, D_MODEL), lambda bi, qi: (0, 0)),
        ],
        out_specs=pl.BlockSpec((tq, D_MODEL), qmap),
        scratch_shapes=[pltpu.VMEM((tq, D_MODEL), BF16)],
        input_output_aliases={1 + len(kv_args): 0},
        compiler_params=_cparams(("arbitrary", "arbitrary")),
        name="attention",
    )(q, *kv_args, x, mod_l, w_out)


def _cmul(ar, ai, br, bi):
    return ar * br - ai * bi, ar * bi + ai * br


def _ssm_ops_kernel(a_ref, dt_ref, bt_ref, c_ref, ops_ref, toep_ref, wct_ref, lam_ref,
                    cl_re_scr, cl_im_scr, op_scr, tp_scr, wc_scr):
    L, H, P = SSM_CHUNK, SSM_GROUP, SSM_STATE
    lane = lax.broadcasted_iota(jnp.int32, (H, CHUNK_W), 1)

    def one_group(g, carry):
        lag = []
        for d in range(2):
            a = a_ref[g, d]
            a_re, a_im = a[0:1], a[1:2]
            dt = jnp.exp(dt_ref[g, d])
            mag = jnp.exp(a_re * dt)
            ang = a_im * dt
            ab_re, ab_im = mag * jnp.cos(ang), mag * jnp.sin(ang)
            den = a_re * a_re + a_im * a_im
            n_re, n_im = ab_re - 1.0, ab_im
            k_re = (n_re * a_re + n_im * a_im) / den
            k_im = (n_im * a_re - n_re * a_im) / den
            bt_re, bt_im = bt_ref[g, d, 0], bt_ref[g, d, 1]
            bb_re, bb_im = _cmul(k_re, k_im, bt_re, bt_im)
            c_re, c_im = c_ref[g, d, 0], c_ref[g, d, 1]

            pw = [(jnp.ones_like(ab_re), jnp.zeros_like(ab_im))]
            for _ in range(L):
                pw.append(_cmul(pw[-1][0], pw[-1][1], ab_re, ab_im))

            for e in range(L + 1):
                pr, pi = pw[e] if d == 0 else pw[L - e]
                cr, ci = _cmul(c_re, c_im, pr, pi)
                cl_re_scr[d, e * H:(e + 1) * H, :] = cr
                cl_im_scr[d, e * H:(e + 1) * H, :] = ci
            lo = 0 if d == 0 else H
            lag.append(lax.dot_general(bb_re, cl_re_scr[d, lo:lo + CHUNK_W, :], (((1,), (1,)), ((), ())),
                                       preferred_element_type=F32, precision=HIGHEST)
                       - lax.dot_general(bb_im, cl_im_scr[d, lo:lo + CHUNK_W, :], (((1,), (1,)), ((), ())),
                                         preferred_element_type=F32, precision=HIGHEST))
            wlo = H if d == 0 else 0
            wc_scr[:, d * STATE_W:d * STATE_W + P] = cl_re_scr[d, wlo:wlo + CHUNK_W, :]
            wc_scr[:, d * STATE_W + P:(d + 1) * STATE_W] = -cl_im_scr[d, wlo:wlo + CHUNK_W, :]
            for j in range(L):
                pj = pw[L - 1 - j] if d == 0 else pw[j]
                sr, si = _cmul(pj[0], pj[1], bb_re, bb_im)
                op_scr[j * H:(j + 1) * H, d * STATE_W:d * STATE_W + P] = sr
                op_scr[j * H:(j + 1) * H, d * STATE_W + P:(d + 1) * STATE_W] = si
                op_scr[j * H:(j + 1) * H, (2 + d) * STATE_W:(2 + d) * STATE_W + P] = si
                op_scr[j * H:(j + 1) * H, (2 + d) * STATE_W + P:(3 + d) * STATE_W] = sr
            pr, pi = pw[L]
            lam_ref[g, 2 * d:2 * d + 1, 0:P] = pr
            lam_ref[g, 2 * d:2 * d + 1, P:STATE_W] = pr
            lam_ref[g, 2 * d + 1:2 * d + 2, 0:P] = -pi
            lam_ref[g, 2 * d + 1:2 * d + 2, P:STATE_W] = pi
        for j in range(L):
            fwd = lag[0] if j == 0 else jnp.where(lane >= H * j, pltpu.roll(lag[0], H * j, 1), 0.0)
            sft = (L - 1 - j) * H
            bwd = lag[1] if sft == 0 else jnp.where(lane < H * (j + 1), pltpu.roll(lag[1], CHUNK_W - sft, 1), 0.0)
            tp_scr[j * H:(j + 1) * H, :] = fwd + bwd
        ops_ref[g] = op_scr[...].astype(BF16)
        toep_ref[g] = tp_scr[...].astype(BF16)
        wct_ref[g] = wc_scr[...].astype(BF16)
        return carry

    lax.fori_loop(0, a_ref.shape[0], one_group, 0)


def _ssm_ops(a, dt, bt, c):
    gb = GB_SSM
    G = SSM_GROUPS
    blk = lambda shape: pl.BlockSpec((gb,) + shape, lambda i: (i,) + (0,) * len(shape))
    return pl.pallas_call(
        _ssm_ops_kernel,
        out_shape=[jax.ShapeDtypeStruct((G, CHUNK_W, 4 * STATE_W), BF16),
                   jax.ShapeDtypeStruct((G, CHUNK_W, CHUNK_W), BF16),
                   jax.ShapeDtypeStruct((G, CHUNK_W, 2 * STATE_W), BF16),
                   jax.ShapeDtypeStruct((G, 4, STATE_W), F32)],
        grid=(G // gb,),
        in_specs=[blk((2, 2, SSM_STATE)), blk((2, 1, SSM_STATE)), blk((2, 2, SSM_GROUP, SSM_STATE)),
                  blk((2, 2, SSM_GROUP, SSM_STATE))],
        out_specs=[blk((CHUNK_W, 4 * STATE_W)), blk((CHUNK_W, CHUNK_W)), blk((CHUNK_W, 2 * STATE_W)),
                   blk((4, STATE_W))],
        scratch_shapes=[pltpu.VMEM((2, (SSM_CHUNK + 1) * SSM_GROUP, SSM_STATE), F32),
                        pltpu.VMEM((2, (SSM_CHUNK + 1) * SSM_GROUP, SSM_STATE), F32),
                        pltpu.VMEM((CHUNK_W, 4 * STATE_W), F32),
                        pltpu.VMEM((CHUNK_W, CHUNK_W), F32),
                        pltpu.VMEM((CHUNK_W, 2 * STATE_W), F32)],
        compiler_params=_cparams(("arbitrary",)),
        name="ssm_ops",
    )(a, dt, bt, c)


GRANULES = LANES // SSM_GROUP


def _granule_transpose(arrs):
    a = list(arrs)
    lane = lax.broadcasted_iota(jnp.int32, a[0].shape, 1)
    for d in (4, 2, 1):
        upper = (lane & (d * SSM_GROUP)) != 0
        for s in range(GRANULES):
            if s & d:
                continue
            lo, hi = a[s], a[s + d]
            a[s] = jnp.where(upper, pltpu.roll(hi, d * SSM_GROUP, 1), lo)
            a[s + d] = jnp.where(upper, hi, pltpu.roll(lo, LANES - d * SSM_GROUP, 1))
    return a


def _ssm_in_kernel(x_ref, mod_ref, g_ref, u_ref, h_scr):
    m = mod_ref[0]
    h = _rms(x_ref[...], g_ref[...]) * (1.0 + m[1:2]) + m[0:1]
    mb = h_scr.shape[1] // SSM_CHUNK
    for g8 in range(SSM_GROUPS // GRANULES):
        h_scr[g8] = h[:, g8 * LANES:(g8 + 1) * LANES]
    for g8 in range(SSM_GROUPS // GRANULES):
        for half in range(SSM_CHUNK // GRANULES):
            a = [h_scr[g8, pl.ds(GRANULES * half + s, mb, stride=SSM_CHUNK), :] for s in range(GRANULES)]
            b = _granule_transpose(a)
            for g in range(GRANULES):
                u_ref[g8 * GRANULES + g, :, half * LANES:(half + 1) * LANES] = b[g].astype(BF16)


def _ssm_in(x, mod_l, g1):
    tm = TM_SSM_IN
    row = _mod_row(tm)
    return pl.pallas_call(
        _ssm_in_kernel,
        out_shape=jax.ShapeDtypeStruct((SSM_GROUPS, N_CHUNK_ROWS, CHUNK_W), BF16),
        grid=(N_TOK // tm,),
        in_specs=[pl.BlockSpec((tm, D_MODEL), lambda i: (i, 0)),
                  pl.BlockSpec((1, 6, D_MODEL), lambda i: (row(i), 0, 0)),
                  pl.BlockSpec((1, D_MODEL), lambda i: (0, 0))],
        out_specs=pl.BlockSpec((SSM_GROUPS, tm // SSM_CHUNK, CHUNK_W), lambda i: (0, i, 0)),
        scratch_shapes=[pltpu.VMEM((D_MODEL // LANES, tm, LANES), F32)],
        compiler_params=_cparams(("arbitrary",)),
        name="ssm_in",
    )(x, mod_l, g1.reshape(1, D_MODEL))


def _ssm_ends_kernel(u_ref, ops_ref, *s_refs):
    for g in range(u_ref.shape[0]):
        s = jnp.dot(u_ref[g], ops_ref[g], preferred_element_type=F32)
        for k, s_ref in enumerate(s_refs):
            s_ref[g] = s[:, k * STATE_W:(k + 1) * STATE_W]


def _ssm_ends(u, ops):
    gb = GB_SSM
    G = SSM_GROUPS
    return pl.pallas_call(
        _ssm_ends_kernel,
        out_shape=[jax.ShapeDtypeStruct((G, N_CHUNK_ROWS, STATE_W), F32)] * 4,
        grid=(G // gb,),
        in_specs=[pl.BlockSpec((gb, N_CHUNK_ROWS, CHUNK_W), lambda i: (i, 0, 0)),
                  pl.BlockSpec((gb, CHUNK_W, 4 * STATE_W), lambda i: (i, 0, 0))],
        out_specs=[pl.BlockSpec((gb, N_CHUNK_ROWS, STATE_W), lambda i: (i, 0, 0))] * 4,
        compiler_params=_cparams(("arbitrary",)),
        name="ssm_chunk_ends",
    )(u, ops)


def _ssm_carry_kernel(sf_ref, sb_ref, sfs_ref, sbs_ref, lam_ref, h0_ref, xf_ref, xb_ref, fin_ref):
    n_g = sf_ref.shape[0] // N_CHUNK_ROWS
    n_p, n_s = SEQ // SSM_CHUNK, DEC_SEQ // SSM_CHUNK

    def swap(t):
        return pltpu.roll(t, SSM_STATE, 1)

    def step(x, lam, rf, rb):
        xf, xfs, xb, xbs = x
        a1f, a2f, a1b, a2b = lam
        xf_ref[rf, :] = xf
        xb_ref[rb, :] = xb
        return [a1f * xf + a2f * xfs + sf_ref[rf, :], a1f * xfs - a2f * xf + sfs_ref[rf, :],
                a1b * xb + a2b * xbs + sb_ref[rb, :], a1b * xbs - a2b * xb + sbs_ref[rb, :]]

    zero = jnp.zeros((BATCH, STATE_W), F32)
    for g0 in range(0, n_g, GB_CARRY):
        groups = range(g0, g0 + GB_CARRY)
        lam = [[jnp.broadcast_to(lam_ref[4 * g + k:4 * g + k + 1, :], (BATCH, STATE_W)) for k in range(4)]
               for g in groups]

        def body_p(c, carry, groups=groups, lam=lam):
            out = []
            for j, g in enumerate(groups):
                rf = pl.ds(g * N_CHUNK_ROWS + c, BATCH, stride=n_p)
                rb = pl.ds(g * N_CHUNK_ROWS + (n_p - 1 - c), BATCH, stride=n_p)
                out += step(carry[4 * j:4 * j + 4], lam[j], rf, rb)
            return tuple(out)

        fin = lax.fori_loop(0, n_p, body_p, (zero,) * (4 * GB_CARRY), unroll=2)
        for j, g in enumerate(groups):
            fin_ref[g, 0] = fin[4 * j]
            fin_ref[g, 1] = fin[4 * j + 2]

    packs = [(g0, b) for g0 in range(0, n_g, 8) for b in range(DEC_BATCH)]
    lam = {g0: [lam_ref[pl.ds(4 * g0 + k, 8, stride=4), :] for k in range(4)] for g0 in range(0, n_g, 8)}

    def body_s(c, carry):
        out = []
        for j, (g0, b) in enumerate(packs):
            base = g0 * N_CHUNK_ROWS + N_CHUNK_PROMPT + b * n_s
            rf = pl.ds(base + c, 8, stride=N_CHUNK_ROWS)
            rb = pl.ds(base + (n_s - 1 - c), 8, stride=N_CHUNK_ROWS)
            out += step(carry[4 * j:4 * j + 4], lam[g0], rf, rb)
        return tuple(out)

    init = []
    for g0, b in packs:
        h0f = h0_ref[pl.ds(4 * g0 + b, 8, stride=4), :]
        h0b = h0_ref[pl.ds(4 * g0 + DEC_BATCH + b, 8, stride=4), :]
        init += [h0f, swap(h0f), h0b, swap(h0b)]
    lax.fori_loop(0, n_s, body_s, tuple(init), unroll=2)


def _ssm_carry(sf, sb, sfs, sbs, lam, h0):
    gb = GB_CARRY_BLOCK
    G = SSM_GROUPS
    st = pl.BlockSpec((gb * N_CHUNK_ROWS, STATE_W), lambda i: (i, 0))
    small = pl.BlockSpec((gb * 4, STATE_W), lambda i: (i, 0))
    return pl.pallas_call(
        _ssm_carry_kernel,
        out_shape=[jax.ShapeDtypeStruct(sf.shape, F32)] * 2 + [jax.ShapeDtypeStruct((G, 2, BATCH, STATE_W), F32)],
        grid=(G // gb,),
        in_specs=[st, st, st, st, small, small],
        out_specs=[st, st, pl.BlockSpec((gb, 2, BATCH, STATE_W), lambda i: (i, 0, 0, 0))],
        compiler_params=_cparams(("arbitrary",)),
        name="ssm_carry",
    )(sf, sb, sfs, sbs, lam, h0)


def _ssm_out_kernel(u_ref, toep_ref, xf_ref, xb_ref, wct_ref, y_ref, y_scr):
    for g in range(GRANULES):
        y = jnp.dot(u_ref[g], toep_ref[g], preferred_element_type=F32)
        for d, x_ref in enumerate((xf_ref, xb_ref)):
            xin = x_ref[g].astype(BF16)
            w = wct_ref[g, :, d * STATE_W:(d + 1) * STATE_W]
            y += lax.dot_general(xin, w, (((1,), (1,)), ((), ())), preferred_element_type=F32)
        y_scr[g] = y
    rb = OUT_ROW_BLOCK
    for r in range(N_CHUNK_ROWS // rb):
        for half in range(SSM_CHUNK // GRANULES):
            a = [y_scr[g, r * rb:(r + 1) * rb, half * LANES:(half + 1) * LANES] for g in range(GRANULES)]
            b = _granule_transpose(a)
            for s in range(GRANULES):
                y_ref[pl.ds(r * rb * SSM_CHUNK + GRANULES * half + s, rb, stride=SSM_CHUNK), :] = b[s]


def _ssm_out(u, toep, xf, xb, wct):
    gb = GRANULES
    G = SSM_GROUPS
    return pl.pallas_call(
        _ssm_out_kernel,
        out_shape=jax.ShapeDtypeStruct((N_TOK, D_MODEL), F32),
        grid=(G // gb,),
        in_specs=[pl.BlockSpec((gb, N_CHUNK_ROWS, CHUNK_W), lambda i: (i, 0, 0)),
                  pl.BlockSpec((gb, CHUNK_W, CHUNK_W), lambda i: (i, 0, 0)),
                  pl.BlockSpec((gb, N_CHUNK_ROWS, STATE_W), lambda i: (i, 0, 0)),
                  pl.BlockSpec((gb, N_CHUNK_ROWS, STATE_W), lambda i: (i, 0, 0)),
                  pl.BlockSpec((gb, CHUNK_W, 2 * STATE_W), lambda i: (i, 0, 0))],
        out_specs=pl.BlockSpec((N_TOK, LANES), lambda i: (0, i)),
        scratch_shapes=[pltpu.VMEM((gb, N_CHUNK_ROWS, CHUNK_W), F32)],
        compiler_params=_cparams(("arbitrary",)),
        name="ssm_chunk_out",
    )(u, toep, xf, xb, wct)


def _ssm_post_kernel(x_ref, mod_ref, g_ref, y_ref, d_ref, w_ref, b_ref, o_ref):
    m = mod_ref[0]
    x = x_ref[...]
    h = _rms(x, g_ref[...]) * (1.0 + m[1:2]) + m[0:1]
    y = h * d_ref[...] + y_ref[...]
    gl = jax.nn.gelu(y).astype(BF16)
    z = jnp.dot(gl, w_ref[...], preferred_element_type=F32) + b_ref[...]
    o_ref[...] = x + m[2:3] * (z[:, :D_MODEL] * jax.nn.sigmoid(z[:, D_MODEL:]))


def _ssm_post(x, mod_l, g1, y, d_skip, w_glu, b_glu):
    tm = TM_SSM
    row = _mod_row(tm)
    full = lambda a: pl.BlockSpec(a.shape, lambda i: (0,) * a.ndim)
    d2, b2 = d_skip.reshape(1, D_MODEL), b_glu.reshape(1, 2 * D_MODEL)
    g2 = g1.reshape(1, D_MODEL)
    return pl.pallas_call(
        _ssm_post_kernel,
        out_shape=jax.ShapeDtypeStruct((N_TOK, D_MODEL), F32),
        grid=(N_TOK // tm,),
        in_specs=[pl.BlockSpec((tm, D_MODEL), lambda i: (i, 0)),
                  pl.BlockSpec((1, 6, D_MODEL), lambda i: (row(i), 0, 0)),
                  full(g2), pl.BlockSpec((tm, D_MODEL), lambda i: (i, 0)), full(d2), full(w_glu), full(b2)],
        out_specs=pl.BlockSpec((tm, D_MODEL), lambda i: (i, 0)),
        compiler_params=_cparams(("arbitrary",)),
        name="ssm_post",
    )(x, mod_l, g2, y, d2, w_glu, b2)


def _rope_tables():
    rows = DEC_SEQ // GRID_W
    t = np.arange(DEC_SEQ)
    row, col = (t // GRID_W).astype(np.float32), (t % GRID_W).astype(np.float32)

    def table(rot_dim, lane0):
        n_freq = rot_dim // 4
        inv_freq = jnp.asarray(ROPE_THETA, F32) ** (-jnp.arange(n_freq, dtype=F32) / n_freq)
        ang_row = jnp.asarray(row)[:, None] * inv_freq
        ang_col = jnp.asarray(col)[:, None] * inv_freq
        ang = jnp.concatenate([ang_row, ang_row, ang_col, ang_col], axis=1)
        sign = jnp.tile(jnp.concatenate([-jnp.ones(n_freq, F32), jnp.ones(n_freq, F32)]), 2)
        cos = jnp.ones((DEC_SEQ, HEAD_PAD), F32).at[:, lane0:lane0 + rot_dim].set(jnp.cos(ang))
        sin = jnp.zeros((DEC_SEQ, HEAD_PAD), F32).at[:, lane0:lane0 + rot_dim].set(jnp.sin(ang) * sign)
        return cos, sin

    assert rows * GRID_W == DEC_SEQ
    return table(MLA_ROPE, MLA_NOPE) + table(GQA_DIM, 0)


def _pad_heads(w, n_heads, dim):
    lead = w.shape[:-1]
    w = w.reshape(lead + (n_heads, dim))
    w = jnp.pad(w, [(0, 0)] * len(lead) + [(0, 0), (0, HEAD_PAD - dim)])
    return w.reshape(lead + (n_heads * HEAD_PAD,))


def _attn_weights(w_in, g_qa, g_kva, w_uq, w_ukv, g_mq, g_mk, g_gq, g_gk, w_out):
    o1, o2, o3 = Q_LORA, Q_LORA + KV_LORA, Q_LORA + KV_LORA + MLA_ROPE
    o4 = o3 + GQA_HEADS * GQA_DIM
    o5 = o4 + GQA_KV_HEADS * GQA_DIM
    kpe = jnp.pad(w_in[:, o2:o3], ((0, 0), (MLA_NOPE, HEAD_PAD - MLA_QK)))
    w_in_ext = jnp.concatenate([w_in[:, :o2], kpe, _pad_heads(w_in[:, o3:o4], GQA_HEADS, GQA_DIM),
                                _pad_heads(w_in[:, o4:o5], GQA_KV_HEADS, GQA_DIM), w_in[:, o5:]], axis=1)
    ukv = w_ukv.reshape(KV_LORA, MLA_HEADS, MLA_NOPE + MLA_V)
    w_ukv_perm = jnp.concatenate([_pad_heads(ukv[:, :, :MLA_NOPE].reshape(KV_LORA, -1), MLA_HEADS, MLA_NOPE),
                                  ukv[:, :, MLA_NOPE:].reshape(KV_LORA, -1)], axis=1)
    pad_g = lambda g: jnp.pad(g, (0, HEAD_PAD - g.shape[0])).reshape(1, HEAD_PAD)
    return dict(w_in=w_in_ext.astype(BF16), g_qa=g_qa.reshape(1, -1), g_kva=g_kva.reshape(1, -1),
                w_uq=_pad_heads(w_uq, MLA_HEADS, MLA_QK).astype(BF16), w_ukv=w_ukv_perm.astype(BF16),
                g_mq=pad_g(g_mq), g_mk=pad_g(g_mk), g_gq=pad_g(g_gq), g_gk=pad_g(g_gk),
                w_out=w_out.astype(BF16))


def _attn_layer(x, mod_l, g1, aw, tabs, cache):
    q, k, v, ckv, kpe, gk, gv = _attn_pre(x, mod_l, g1, aw, tabs)
    c_ckv, c_krope, c_gk, c_gv = cache
    n_c = DEC_BATCH * PAST_LEN
    kpe_c = jnp.pad(c_krope.reshape(n_c, MLA_ROPE), ((0, 0), (MLA_NOPE, HEAD_PAD - MLA_QK)))
    km_c, vm_c = _cache_kv(c_ckv.reshape(n_c, KV_LORA), kpe_c, aw)
    k_c = jnp.concatenate([km_c, _pad_heads(c_gk.reshape(n_c, -1), GQA_KV_HEADS, GQA_DIM).astype(BF16)], axis=1)
    v_c = jnp.concatenate([vm_c, c_gv.reshape(n_c, -1).astype(BF16)], axis=1)
    x = _attention(q, [(k, v, SEQ, 0)], x, mod_l, aw["w_out"], SEQ, BATCH, SEQ, 0, lambda b: 0)
    x = _attention(q, [(k_c, v_c, PAST_LEN, 0), (k, v, DEC_SEQ, N_PROMPT // DEC_SEQ)], x, mod_l, aw["w_out"],
                   TQ_SAMPLE, DEC_BATCH, DEC_SEQ, N_PROMPT, lambda b: 1 + b)
    new = (ckv[:N_PROMPT].reshape(BATCH, SEQ, KV_LORA),
           kpe[:N_PROMPT, MLA_NOPE:MLA_QK].reshape(BATCH, SEQ, MLA_ROPE),
           gk[:N_PROMPT].reshape(BATCH, SEQ, GQA_KV_HEADS, HEAD_PAD)[..., :GQA_DIM],
           gv[:N_PROMPT].reshape(BATCH, SEQ, GQA_KV_HEADS, GQA_DIM))
    return x, new


def _ssm_layer(x, mod_l, g1, a_re, a_im, log_dt, b_re, b_im, c_re, c_im, d_skip, w_glu, b_glu, state0):
    G, P, H = SSM_GROUPS, SSM_STATE, SSM_GROUP
    a = jnp.stack([a_re, a_im], axis=2).transpose(1, 0, 2, 3)
    dt = jnp.broadcast_to(log_dt.transpose(1, 0)[:, :, None, None], (G, 2, 1, P))
    bt = jnp.stack([b_re, b_im], axis=2).transpose(1, 0, 2, 4, 3)
    c = jnp.stack([c_re, c_im], axis=2).transpose(1, 0, 2, 3, 4)
    ops, toep, wct, lam = _ssm_ops(a, dt, bt, c)

    u = _ssm_in(x, mod_l, g1)
    sf, sb, sfs, sbs = _ssm_ends(u, ops)
    h0 = state0.transpose(2, 1, 0, 4, 3).reshape(G * 2 * DEC_BATCH, STATE_W)
    flat = lambda s: s.reshape(G * N_CHUNK_ROWS, STATE_W)
    xf, xb, fin = _ssm_carry(flat(sf), flat(sb), flat(sfs), flat(sbs), lam.reshape(G * 4, STATE_W), h0)
    y = _ssm_out(u, toep, xf.reshape(sf.shape), xb.reshape(sb.shape), wct)
    x = _ssm_post(x, mod_l, g1, y, d_skip, w_glu.astype(BF16), b_glu)
    fin = fin.reshape(G, 2, BATCH, 2, P).transpose(2, 1, 0, 4, 3)
    return x, fin


def kernel(x_prompt, x_sample, c, cache_mla_ckv, cache_mla_krope, cache_gqa_k, cache_gqa_v, state_ssm, c_ctx,
           norm1_g, norm2_g, w_mod, b_mod,
           attn_w_in, attn_qa_norm_g, attn_kva_norm_g, attn_w_uq, attn_w_ukv,
           attn_mla_q_norm_g, attn_mla_k_norm_g, attn_gqa_q_norm_g, attn_gqa_k_norm_g, attn_w_out,
           ssm_a_re, ssm_a_im, ssm_log_dt, ssm_b_re, ssm_b_im, ssm_c_re, ssm_c_im, ssm_d, ssm_w_glu, ssm_b_glu,
           ffn_w_up, ffn_conv_w, ffn_conv_b, ffn_w_down):
    x = jnp.concatenate([x_prompt.reshape(N_PROMPT, D_MODEL), x_sample.reshape(N_SAMPLE, D_MODEL)], axis=0)
    cond8 = jnp.concatenate([c_ctx[None, :], c, jnp.zeros((8 - 1 - DEC_BATCH, D_MODEL), F32)], axis=0)
    mod = _modulation(cond8, w_mod, b_mod).reshape(DEPTH, 8, 6, D_MODEL)
    tabs = _rope_tables()

    w_up_bf, w_down_bf = ffn_w_up.astype(BF16), ffn_w_down.astype(BF16)
    new_attn, new_ssm = [], []
    for i in range(DEPTH):
        j = i // 2
        if i % 2 == 0:
            aw = _attn_weights(attn_w_in[j], attn_qa_norm_g[j], attn_kva_norm_g[j], attn_w_uq[j], attn_w_ukv[j],
                               attn_mla_q_norm_g[j], attn_mla_k_norm_g[j], attn_gqa_q_norm_g[j],
                               attn_gqa_k_norm_g[j], attn_w_out[j])
            cache = (cache_mla_ckv[:, j], cache_mla_krope[:, j], cache_gqa_k[:, j], cache_gqa_v[:, j])
            x, new = _attn_layer(x, mod[i], norm1_g[i], aw, tabs, cache)
            new_attn.append(new)
        else:
            x, fin = _ssm_layer(x, mod[i], norm1_g[i], ssm_a_re[j], ssm_a_im[j], ssm_log_dt[j], ssm_b_re[j],
                                ssm_b_im[j], ssm_c_re[j], ssm_c_im[j], ssm_d[j], ssm_w_glu[j], ssm_b_glu[j],
                                state_ssm[:, j])
            new_ssm.append(fin)
        ffn = functools.partial(_conv_ffn, x, mod[i], norm2_g[i], i, w_up_bf, ffn_conv_w, ffn_conv_b, w_down_bf)
        if i < DEPTH - 1:
            x = ffn()
        else:
            y_prompt, y_sample = ffn(0, N_PROMPT), ffn(N_PROMPT, N_SAMPLE)

    outs = [jnp.stack([n[k] for n in new_attn], axis=1) for k in range(4)]
    return (y_prompt.reshape(BATCH, SEQ, D_MODEL), y_sample.reshape(DEC_BATCH, DEC_SEQ, D_MODEL),
            outs[0], outs[1], outs[2], outs[3], jnp.stack(new_ssm, axis=1))
```

```python
import functools
import math

import jax
import jax.numpy as jnp
import numpy as np
from jax import lax
from jax.experimental import pallas as pl
from jax.experimental.pallas import tpu as pltpu

F32 = jnp.float32
BF16 = jnp.bfloat16
HIGHEST = lax.Precision.HIGHEST

LANES = 128
BF16_SUBLANES = 16
VMEM_LIMIT_BYTES = 56 * 1024 * 1024

D_MODEL = 1024
BATCH = 16
SEQ = 256
DEPTH = 4
DEC_BATCH = 2
DEC_SEQ = 2048
PAST_LEN = 256
GRID_W = 64
MLA_HEADS = 8
Q_LORA = 384
KV_LORA = 256
MLA_NOPE = 64
MLA_ROPE = 32
MLA_QK = MLA_NOPE + MLA_ROPE
MLA_V = 64
GQA_HEADS = 8
GQA_KV_HEADS = 2
GQA_DIM = 64
GQA_REP = GQA_HEADS // GQA_KV_HEADS
ROPE_THETA = 10000.0
SSM_GROUP = 16
SSM_GROUPS = D_MODEL // SSM_GROUP
SSM_STATE = 64
D_FF = 2816
EPS = 1e-6

N_PROMPT = BATCH * SEQ
N_SAMPLE = DEC_BATCH * DEC_SEQ
N_TOK = N_PROMPT + N_SAMPLE
N_HEADS = MLA_HEADS + GQA_HEADS
HEAD_PAD = LANES
KV_LEN_SAMPLE = PAST_LEN + DEC_SEQ

IN_Q = 0
IN_CKV = IN_Q + Q_LORA
IN_KPE = IN_CKV + KV_LORA
IN_GQ = IN_KPE + HEAD_PAD
IN_GK = IN_GQ + GQA_HEADS * HEAD_PAD
IN_GV = IN_GK + GQA_KV_HEADS * HEAD_PAD
IN_EXT = IN_GV + GQA_KV_HEADS * GQA_DIM

K_ALL = (MLA_HEADS + GQA_KV_HEADS) * HEAD_PAD
V_ALL = MLA_HEADS * MLA_V + GQA_KV_HEADS * GQA_DIM

SSM_CHUNK = 16
CHUNK_W = SSM_CHUNK * SSM_GROUP
N_CHUNK_ROWS = N_TOK // SSM_CHUNK
N_CHUNK_PROMPT = N_PROMPT // SSM_CHUNK
STATE_W = 2 * SSM_STATE

TM_FFN = 512
TN_FFN = 256
FFN_HALO = BF16_SUBLANES
TM_PRE = 512
TQ_SAMPLE = 256
TM_SSM = 1024
TM_SSM_IN = 512
OUT_ROW_BLOCK = 32
GB_CARRY = 4
GB_CARRY_BLOCK = 16
TN_MOD = 1536
GB_SSM = 8


def _cparams(sem):
    return pltpu.CompilerParams(dimension_semantics=sem, vmem_limit_bytes=VMEM_LIMIT_BYTES)


def _rms(x, g):
    return x * lax.rsqrt(jnp.mean(x * x, axis=-1, keepdims=True) + EPS) * g


def _mod_row(tm):
    n_prompt_tiles = N_PROMPT // tm
    tiles_per_seq = DEC_SEQ // tm

    def row(i):
        return jnp.where(i < n_prompt_tiles, 0, 1 + (i - n_prompt_tiles) // tiles_per_seq)

    return row


def _mod_kernel(cond_ref, w_ref, b_ref, o_ref):
    a = jax.nn.silu(cond_ref[...])
    w = w_ref[0]
    a_hi, w_hi = a.astype(BF16), w.astype(BF16)
    a_lo = (a - a_hi.astype(F32)).astype(BF16)
    w_lo = (w - w_hi.astype(F32)).astype(BF16)
    dot = functools.partial(jnp.dot, preferred_element_type=F32)
    o_ref[0] = dot(a_hi, w_hi) + dot(a_lo, w_hi) + dot(a_hi, w_lo) + b_ref[0]


def _modulation(cond8, w_mod, b_mod):
    n = w_mod.shape[-1]
    return pl.pallas_call(
        _mod_kernel,
        out_shape=jax.ShapeDtypeStruct((DEPTH, 8, n), F32),
        grid=(DEPTH, n // TN_MOD),
        in_specs=[
            pl.BlockSpec((8, D_MODEL), lambda l, j: (0, 0)),
            pl.BlockSpec((1, D_MODEL, TN_MOD), lambda l, j: (l, 0, j)),
            pl.BlockSpec((1, 1, TN_MOD), lambda l, j: (l, 0, j)),
        ],
        out_specs=pl.BlockSpec((1, 8, TN_MOD), lambda l, j: (l, 0, j)),
        compiler_params=_cparams(("arbitrary", "arbitrary")),
        name="modulation",
    )(cond8, w_mod, b_mod.reshape(DEPTH, 1, n))


def _ffn_kernel(xp_ref, x_ref, xn_ref, mod_ref, g_ref, wup_ref, cw_ref, cb_ref, wd_ref, o_ref,
                h_scr, act_scr, z_scr, *, tile0):
    i = pl.program_id(0) + tile0
    tm = x_ref.shape[0]
    tn = TN_FFN
    m = mod_ref[0]
    sh, sc, gt = m[3:4], m[4:5], m[5:6]
    g = g_ref[...]

    def hmod(x):
        return (_rms(x, g) * (1.0 + sc) + sh).astype(BF16)

    h_scr[0:FFN_HALO] = hmod(xp_ref[...])
    h_scr[FFN_HALO:FFN_HALO + tm] = hmod(x_ref[...])
    h_scr[FFN_HALO + tm:] = hmod(xn_ref[...])
    seq_len = jnp.where(i * tm < N_PROMPT, SEQ, DEC_SEQ)
    pos = (i * tm + lax.broadcasted_iota(jnp.int32, (tm, 1), 0)) & (seq_len - 1)
    eg = BF16_SUBLANES
    bounds = range(0, tm + 1, SEQ)
    cuts = sorted({0, tm} | {b + o for b in bounds for o in (-eg, eg) if 0 < b + o < tm})
    segments = [(r0, r1, any(r0 <= b < r1 or r0 < b <= r1 for b in bounds)) for r0, r1 in zip(cuts[:-1], cuts[1:])]

    def conv(slot, k, col, r0, r1, masked):
        cw = cw_ref[:, col:col + tn]
        zp = z_scr[slot, k, FFN_HALO - 1 + r0:FFN_HALO - 1 + r1, :]
        zc = z_scr[slot, k, FFN_HALO + r0:FFN_HALO + r1, :]
        zn = z_scr[slot, k, FFN_HALO + 1 + r0:FFN_HALO + 1 + r1, :]
        if masked:
            zp = jnp.where(pos[r0:r1] == 0, 0.0, zp)
            zn = jnp.where(pos[r0:r1] == seq_len - 1, 0.0, zn)
        return zp * cw[0:1] + zc * cw[1:2] + zn * cw[2:3] + cb_ref[:, col:col + tn]

    h = h_scr[...]
    for jc in range(D_FF // tn):
        ca, cb = jc * tn, D_FF + jc * tn
        slot = jc % 2
        z_scr[slot, 0] = jnp.dot(h, wup_ref[:, ca:ca + tn], preferred_element_type=F32)
        z_scr[slot, 1] = jnp.dot(h, wup_ref[:, cb:cb + tn], preferred_element_type=F32)
        for r0, r1, masked in segments:
            act_scr[r0:r1, ca:ca + tn] = (jax.nn.silu(conv(slot, 0, ca, r0, r1, masked))
                                          * conv(slot, 1, cb, r0, r1, masked)).astype(BF16)
    o_ref[...] = x_ref[...] + gt * jnp.dot(act_scr[...], wd_ref[...], preferred_element_type=F32)


def _conv_ffn(x, mod_l, g, layer, w_up, conv_w, conv_b, w_down, tok0=0, n_tok=N_TOK):
    tm, halo = TM_FFN, FFN_HALO
    row = _mod_row(tm)
    n_halo_blocks = N_TOK // halo
    tile0 = tok0 // tm
    resident = lambda a: pl.BlockSpec((None,) + a.shape[1:], lambda i: (layer,) + (0,) * (a.ndim - 1),
                                      pipeline_mode=pl.Buffered(1))
    cb3 = conv_b.reshape(DEPTH, 1, -1)
    return pl.pallas_call(
        functools.partial(_ffn_kernel, tile0=tile0),
        out_shape=jax.ShapeDtypeStruct((n_tok, D_MODEL), F32),
        grid=(n_tok // tm,),
        in_specs=[
            pl.BlockSpec((halo, D_MODEL), lambda i: (jnp.maximum((i + tile0) * (tm // halo) - 1, 0), 0)),
            pl.BlockSpec((tm, D_MODEL), lambda i: (i + tile0, 0)),
            pl.BlockSpec((halo, D_MODEL),
                         lambda i: (jnp.minimum((i + tile0 + 1) * (tm // halo), n_halo_blocks - 1), 0)),
            pl.BlockSpec((1, 6, D_MODEL), lambda i: (row(i + tile0), 0, 0)),
            pl.BlockSpec((1, D_MODEL), lambda i: (0, 0)),
            resident(w_up), resident(conv_w), resident(cb3), resident(w_down),
        ],
        out_specs=pl.BlockSpec((tm, D_MODEL), lambda i: (i, 0)),
        scratch_shapes=[pltpu.VMEM((tm + 2 * halo, D_MODEL), BF16), pltpu.VMEM((tm, D_FF), BF16),
                        pltpu.VMEM((2, 2, tm + 2 * halo, TN_FFN), F32)],
        compiler_params=_cparams(("arbitrary",)),
        name="conv_ffn",
    )(x, x, x, mod_l, g.reshape(1, D_MODEL), w_up, conv_w, cb3, w_down)


def _rope(x, cos, sin, half):
    lane = lax.broadcasted_iota(jnp.int32, x.shape, 1)
    first = ((lane // half) & 1) == 0
    partner = jnp.where(first, pltpu.roll(x, LANES - half, 1), pltpu.roll(x, half, 1))
    return x * cos + partner * sin


def _head_norm(xh, g, dim):
    return xh * lax.rsqrt(jnp.sum(xh * xh, axis=-1, keepdims=True) * (1.0 / dim) + EPS) * g


def _mla_kv_heads(ckv_bf, kpe, w_ukv_ref, g_mk, cos_m, sin_m, k_ref, v_ref):
    kv = jnp.dot(ckv_bf, w_ukv_ref[...], preferred_element_type=F32)
    for h in range(MLA_HEADS):
        kh = kv[:, h * HEAD_PAD:(h + 1) * HEAD_PAD] + kpe
        kh = _head_norm(kh, g_mk, MLA_QK)
        if cos_m is not None:
            kh = _rope(kh, cos_m, sin_m, MLA_ROPE // 4)
        k_ref[:, h * HEAD_PAD:(h + 1) * HEAD_PAD] = kh.astype(BF16)
    v_ref[:, 0:MLA_HEADS * MLA_V] = kv[:, MLA_HEADS * HEAD_PAD:].astype(BF16)


def _attn_pre_kernel(x_ref, mod_ref, g1_ref, w_in_ref, g_qa_ref, g_kva_ref, w_uq_ref, w_ukv_ref,
                     g_mq_ref, g_mk_ref, g_gq_ref, g_gk_ref, cos_m_ref, sin_m_ref, cos_g_ref, sin_g_ref,
                     q_ref, k_ref, v_ref, ckv_ref, kpe_ref, gk_ref, gv_ref):
    m = mod_ref[0]
    sh, sc = m[0:1], m[1:2]
    h = (_rms(x_ref[...], g1_ref[...]) * (1.0 + sc) + sh).astype(BF16)
    p = jnp.dot(h, w_in_ref[...], preferred_element_type=F32)
    q_c = _rms(p[:, IN_Q:IN_CKV], g_qa_ref[...]).astype(BF16)
    ckv = _rms(p[:, IN_CKV:IN_KPE], g_kva_ref[...])
    kpe = p[:, IN_KPE:IN_GQ]
    ckv_ref[...] = ckv
    kpe_ref[...] = kpe
    gv = p[:, IN_GV:IN_EXT]
    gv_ref[...] = gv
    v_ref[:, MLA_HEADS * MLA_V:] = gv.astype(BF16)
    qm = jnp.dot(q_c, w_uq_ref[...], preferred_element_type=F32)

    def heads(rope):
        cos_m, sin_m = (cos_m_ref[...], sin_m_ref[...]) if rope else (None, None)
        cos_g, sin_g = (cos_g_ref[...], sin_g_ref[...]) if rope else (None, None)
        g_mq = g_mq_ref[...]
        for hd in range(MLA_HEADS):
            qh = _head_norm(qm[:, hd * HEAD_PAD:(hd + 1) * HEAD_PAD], g_mq, MLA_QK)
            if rope:
                qh = _rope(qh, cos_m, sin_m, MLA_ROPE // 4)
            q_ref[:, hd * HEAD_PAD:(hd + 1) * HEAD_PAD] = (qh * (1.0 / math.sqrt(MLA_QK))).astype(BF16)
        _mla_kv_heads(ckv.astype(BF16), kpe, w_ukv_ref, g_mk_ref[...], cos_m, sin_m, k_ref, v_ref)
        g_gq = g_gq_ref[...]
        for hd in range(GQA_HEADS):
            qh = _head_norm(p[:, IN_GQ + hd * HEAD_PAD:IN_GQ + (hd + 1) * HEAD_PAD], g_gq, GQA_DIM)
            if rope:
                qh = _rope(qh, cos_g, sin_g, GQA_DIM // 4)
            q_ref[:, (MLA_HEADS + hd) * HEAD_PAD:(MLA_HEADS + hd + 1) * HEAD_PAD] = (
                qh * (1.0 / math.sqrt(GQA_DIM))).astype(BF16)
        g_gk = g_gk_ref[...]
        for hd in range(GQA_KV_HEADS):
            kh = _head_norm(p[:, IN_GK + hd * HEAD_PAD:IN_GK + (hd + 1) * HEAD_PAD], g_gk, GQA_DIM)
            gk_ref[:, hd * HEAD_PAD:(hd + 1) * HEAD_PAD] = kh
            if rope:
                kh = _rope(kh, cos_g, sin_g, GQA_DIM // 4)
            k_ref[:, (MLA_HEADS + hd) * HEAD_PAD:(MLA_HEADS + hd + 1) * HEAD_PAD] = kh.astype(BF16)

    is_latent = pl.program_id(0) >= N_PROMPT // x_ref.shape[0]
    pl.when(is_latent)(lambda: heads(True))
    pl.when(jnp.logical_not(is_latent))(lambda: heads(False))


def _attn_pre(x, mod_l, g1, aw, tabs):
    tm = TM_PRE
    row = _mod_row(tm)
    full = lambda a: pl.BlockSpec(a.shape, lambda i: (0,) * a.ndim)
    tok = lambda w: pl.BlockSpec((tm, w), lambda i: (i, 0))
    pos_block = lambda i: (jnp.maximum(i - N_PROMPT // tm, 0) % (DEC_SEQ // tm), 0)
    consts = [g1.reshape(1, D_MODEL), aw["w_in"], aw["g_qa"], aw["g_kva"], aw["w_uq"], aw["w_ukv"],
              aw["g_mq"], aw["g_mk"], aw["g_gq"], aw["g_gk"]]
    return pl.pallas_call(
        _attn_pre_kernel,
        out_shape=[
            jax.ShapeDtypeStruct((N_TOK, N_HEADS * HEAD_PAD), BF16),
            jax.ShapeDtypeStruct((N_TOK, K_ALL), BF16),
            jax.ShapeDtypeStruct((N_TOK, V_ALL), BF16),
            jax.ShapeDtypeStruct((N_TOK, KV_LORA), F32),
            jax.ShapeDtypeStruct((N_TOK, HEAD_PAD), F32),
            jax.ShapeDtypeStruct((N_TOK, GQA_KV_HEADS * HEAD_PAD), F32),
            jax.ShapeDtypeStruct((N_TOK, GQA_KV_HEADS * GQA_DIM), F32),
        ],
        grid=(N_TOK // tm,),
        in_specs=[tok(D_MODEL), pl.BlockSpec((1, 6, D_MODEL), lambda i: (row(i), 0, 0))]
        + [full(a) for a in consts] + [pl.BlockSpec((tm, HEAD_PAD), pos_block)] * 4,
        out_specs=[tok(N_HEADS * HEAD_PAD), tok(K_ALL), tok(V_ALL), tok(KV_LORA), tok(HEAD_PAD),
                   tok(GQA_KV_HEADS * HEAD_PAD), tok(GQA_KV_HEADS * GQA_DIM)],
        compiler_params=_cparams(("arbitrary",)),
        name="attn_pre",
    )(x, mod_l, *consts, *tabs)


def _cache_kv_kernel(ckv_ref, kpe_ref, w_ukv_ref, g_mk_ref, k_ref, v_ref):
    _mla_kv_heads(ckv_ref[...].astype(BF16), kpe_ref[...], w_ukv_ref, g_mk_ref[...], None, None, k_ref, v_ref)


def _cache_kv(ckv, kpe_pad, aw):
    n = ckv.shape[0]
    return pl.pallas_call(
        _cache_kv_kernel,
        out_shape=[jax.ShapeDtypeStruct((n, MLA_HEADS * HEAD_PAD), BF16),
                   jax.ShapeDtypeStruct((n, MLA_HEADS * MLA_V), BF16)],
        name="cache_kv",
    )(ckv, kpe_pad, aw["w_ukv"], aw["g_mk"])


def _attn_kernel(*refs, n_seg):
    q_ref, kv_refs = refs[0], refs[1:1 + 2 * n_seg]
    x_ref, mod_ref, wo_ref, o_ref, oh_scr = refs[1 + 2 * n_seg:]
    for hd in range(N_HEADS):
        if hd < MLA_HEADS:
            kc, vc = hd * HEAD_PAD, hd * MLA_V
        else:
            kvh = (hd - MLA_HEADS) // GQA_REP
            kc, vc = (MLA_HEADS + kvh) * HEAD_PAD, MLA_HEADS * MLA_V + kvh * GQA_DIM
        q = q_ref[:, hd * HEAD_PAD:(hd + 1) * HEAD_PAD]
        ss = [lax.dot_general(q, kv_refs[2 * j][:, kc:kc + HEAD_PAD], (((1,), (1,)), ((), ())),
                              preferred_element_type=F32) for j in range(n_seg)]
        m = functools.reduce(jnp.maximum, [jnp.max(s, axis=-1, keepdims=True) for s in ss])
        ps = [jnp.exp(s - m) for s in ss]
        den = sum(jnp.sum(p, axis=-1, keepdims=True) for p in ps)
        o = sum(jnp.dot(p.astype(BF16), kv_refs[2 * j + 1][:, vc:vc + MLA_V], preferred_element_type=F32)
                for j, p in enumerate(ps))
        oh_scr[:, hd * MLA_V:(hd + 1) * MLA_V] = (o / den).astype(BF16)
    gt = mod_ref[0][2:3]
    o_ref[...] = x_ref[...] + gt * jnp.dot(oh_scr[...], wo_ref[...], preferred_element_type=F32)


def _attention(q, segs, x, mod_l, w_out, tq, n_batch, t_len, tok0, mod_row):
    tiles = t_len // tq
    qmap = lambda bi, qi: (tok0 // tq + bi * tiles + qi, 0)
    kv_specs, kv_args = [], []
    for k, v, rows, blk0 in segs:
        kv_specs += [pl.BlockSpec((rows, K_ALL), lambda bi, qi, blk0=blk0: (blk0 + bi, 0)),
                     pl.BlockSpec((rows, V_ALL), lambda bi, qi, blk0=blk0: (blk0 + bi, 0))]
        kv_args += [k, v]
    return pl.pallas_call(
        functools.partial(_attn_kernel, n_seg=len(segs)),
        out_shape=jax.ShapeDtypeStruct((N_TOK, D_MODEL), F32),
        grid=(n_batch, tiles),
        in_specs=[pl.BlockSpec((tq, N_HEADS * HEAD_PAD), qmap)] + kv_specs + [
            pl.BlockSpec((tq, D_MODEL), qmap),
            pl.BlockSpec((1, 6, D_MODEL), lambda bi, qi: (mod_row(bi), 0, 0)),
            pl.BlockSpec((D_MODEL, D_MODEL), lambda bi, qi: (0, 0)),
        ],
        out_specs=pl.BlockSpec((tq, D_MODEL), qmap),
        scratch_shapes=[pltpu.VMEM((tq, D_MODEL), BF16)],
        input_output_aliases={1 + len(kv_args): 0},
        compiler_params=_cparams(("arbitrary", "arbitrary")),
        name="attention",
    )(q, *kv_args, x, mod_l, w_out)


def _cmul(ar, ai, br, bi):
    return ar * br - ai * bi, ar * bi + ai * br


def _ssm_ops_kernel(a_ref, dt_ref, bt_ref, c_ref, ops_ref, toep_ref, wct_ref, lam_ref,
                    cl_re_scr, cl_im_scr, op_scr, tp_scr, wc_scr):
    L, H, P = SSM_CHUNK, SSM_GROUP, SSM_STATE
    lane = lax.broadcasted_iota(jnp.int32, (H, CHUNK_W), 1)

    a_re, a_im = a_ref[0], a_ref[1]
    dt_all = jnp.exp(dt_ref[...])
    mag = jnp.exp(a_re * dt_all)
    ang = a_im * dt_all
    ab_re, ab_im = mag * jnp.cos(ang), mag * jnp.sin(ang)
    den = a_re * a_re + a_im * a_im
    n_re, n_im = ab_re - 1.0, ab_im
    k_re_all = (n_re * a_re + n_im * a_im) / den
    k_im_all = (n_im * a_re - n_re * a_im) / den
    pw_all = [(jnp.ones_like(ab_re), jnp.zeros_like(ab_im))]
    for _ in range(L):
        pw_all.append(_cmul(pw_all[-1][0], pw_all[-1][1], ab_re, ab_im))

    for g in range(ops_ref.shape[0]):
        lag = []
        for d in range(2):
            r = 2 * g + d
            bt_re, bt_im = bt_ref[g, d, 0], bt_ref[g, d, 1]
            bb_re, bb_im = _cmul(k_re_all[r:r + 1], k_im_all[r:r + 1], bt_re, bt_im)
            c_re, c_im = c_ref[g, d, 0], c_ref[g, d, 1]
            pw = [(p_re[r:r + 1], p_im[r:r + 1]) for p_re, p_im in pw_all]

            for e in range(L + 1):
                pr, pi = pw[e] if d == 0 else pw[L - e]
                cr, ci = _cmul(c_re, c_im, pr, pi)
                cl_re_scr[r,e * H:(e + 1) * H, :] = cr
                cl_im_scr[r,e * H:(e + 1) * H, :] = ci
            lo = 0 if d == 0 else H
            lag.append(lax.dot_general(bb_re, cl_re_scr[r,lo:lo + CHUNK_W, :], (((1,), (1,)), ((), ())),
                                       preferred_element_type=F32, precision=HIGHEST)
                       - lax.dot_general(bb_im, cl_im_scr[r,lo:lo + CHUNK_W, :], (((1,), (1,)), ((), ())),
                                         preferred_element_type=F32, precision=HIGHEST))
            wlo = H if d == 0 else 0
            wc_scr[g, :, d * STATE_W:d * STATE_W + P] = cl_re_scr[r,wlo:wlo + CHUNK_W, :]
            wc_scr[g, :, d * STATE_W + P:(d + 1) * STATE_W] = -cl_im_scr[r,wlo:wlo + CHUNK_W, :]
            for j in range(L):
                pj = pw[L - 1 - j] if d == 0 else pw[j]
                sr, si = _cmul(pj[0], pj[1], bb_re, bb_im)
                op_scr[g, j * H:(j + 1) * H, d * STATE_W:d * STATE_W + P] = sr
                op_scr[g, j * H:(j + 1) * H, d * STATE_W + P:(d + 1) * STATE_W] = si
            pr, pi = pw[L]
            lam_ref[g, 2 * d:2 * d + 1, 0:P] = pr
            lam_ref[g, 2 * d:2 * d + 1, P:STATE_W] = pr
            lam_ref[g, 2 * d + 1:2 * d + 2, 0:P] = -pi
            lam_ref[g, 2 * d + 1:2 * d + 2, P:STATE_W] = pi
        for j in range(L):
            fwd = lag[0] if j == 0 else jnp.where(lane >= H * j, pltpu.roll(lag[0], H * j, 1), 0.0)
            sft = (L - 1 - j) * H
            bwd = lag[1] if sft == 0 else jnp.where(lane < H * (j + 1), pltpu.roll(lag[1], CHUNK_W - sft, 1), 0.0)
            tp_scr[g, j * H:(j + 1) * H, :] = fwd + bwd
        ops_ref[g] = op_scr[g].astype(BF16)
        toep_ref[g] = tp_scr[g].astype(BF16)
        wct_ref[g] = wc_scr[g].astype(BF16)


def _ssm_ops(a, dt, bt, c):
    gb = GB_SSM
    G = SSM_GROUPS
    blk = lambda shape: pl.BlockSpec((gb,) + shape, lambda i: (i,) + (0,) * len(shape))
    return pl.pallas_call(
        _ssm_ops_kernel,
        out_shape=[jax.ShapeDtypeStruct((G, CHUNK_W, 2 * STATE_W), BF16),
                   jax.ShapeDtypeStruct((G, CHUNK_W, CHUNK_W), BF16),
                   jax.ShapeDtypeStruct((G, CHUNK_W, 2 * STATE_W), BF16),
                   jax.ShapeDtypeStruct((G, 4, STATE_W), F32)],
        grid=(G // gb,),
        in_specs=[pl.BlockSpec((2, 2 * gb, SSM_STATE), lambda i: (0, i, 0)),
                  pl.BlockSpec((2 * gb, SSM_STATE), lambda i: (i, 0)),
                  blk((2, 2, SSM_GROUP, SSM_STATE)), blk((2, 2, SSM_GROUP, SSM_STATE))],
        out_specs=[blk((CHUNK_W, 2 * STATE_W)), blk((CHUNK_W, CHUNK_W)), blk((CHUNK_W, 2 * STATE_W)),
                   blk((4, STATE_W))],
        scratch_shapes=[pltpu.VMEM((2 * gb, (SSM_CHUNK + 1) * SSM_GROUP, SSM_STATE), F32),
                        pltpu.VMEM((2 * gb, (SSM_CHUNK + 1) * SSM_GROUP, SSM_STATE), F32),
                        pltpu.VMEM((gb, CHUNK_W, 2 * STATE_W), F32),
                        pltpu.VMEM((gb, CHUNK_W, CHUNK_W), F32),
                        pltpu.VMEM((gb, CHUNK_W, 2 * STATE_W), F32)],
        compiler_params=_cparams(("arbitrary",)),
        name="ssm_ops",
    )(a, dt, bt, c)


GRANULES = LANES // SSM_GROUP


def _granule_transpose(arrs):
    a = list(arrs)
    lane = lax.broadcasted_iota(jnp.int32, a[0].shape, 1)
    for d in (4, 2, 1):
        upper = (lane & (d * SSM_GROUP)) != 0
        for s in range(GRANULES):
            if s & d:
                continue
            lo, hi = a[s], a[s + d]
            a[s] = jnp.where(upper, pltpu.roll(hi, d * SSM_GROUP, 1), lo)
            a[s + d] = jnp.where(upper, hi, pltpu.roll(lo, LANES - d * SSM_GROUP, 1))
    return a


def _ssm_in_kernel(x_ref, mod_ref, g_ref, u_ref, h_scr):
    m = mod_ref[0]
    h = _rms(x_ref[...], g_ref[...]) * (1.0 + m[1:2]) + m[0:1]
    mb = h_scr.shape[1] // SSM_CHUNK
    for g8 in range(SSM_GROUPS // GRANULES):
        h_scr[g8] = h[:, g8 * LANES:(g8 + 1) * LANES]
    for g8 in range(SSM_GROUPS // GRANULES):
        for half in range(SSM_CHUNK // GRANULES):
            a = [h_scr[g8, pl.ds(GRANULES * half + s, mb, stride=SSM_CHUNK), :] for s in range(GRANULES)]
            b = _granule_transpose(a)
            for g in range(GRANULES):
                u_ref[g8 * GRANULES + g, :, half * LANES:(half + 1) * LANES] = b[g].astype(BF16)


def _ssm_in(x, mod_l, g1):
    tm = TM_SSM_IN
    row = _mod_row(tm)
    return pl.pallas_call(
        _ssm_in_kernel,
        out_shape=jax.ShapeDtypeStruct((SSM_GROUPS, N_CHUNK_ROWS, CHUNK_W), BF16),
        grid=(N_TOK // tm,),
        in_specs=[pl.BlockSpec((tm, D_MODEL), lambda i: (i, 0)),
                  pl.BlockSpec((1, 6, D_MODEL), lambda i: (row(i), 0, 0)),
                  pl.BlockSpec((1, D_MODEL), lambda i: (0, 0))],
        out_specs=pl.BlockSpec((SSM_GROUPS, tm // SSM_CHUNK, CHUNK_W), lambda i: (0, i, 0)),
        scratch_shapes=[pltpu.VMEM((D_MODEL // LANES, tm, LANES), F32)],
        compiler_params=_cparams(("arbitrary",)),
        name="ssm_in",
    )(x, mod_l, g1.reshape(1, D_MODEL))


def _ssm_ends_kernel(u_ref, ops_ref, *s_refs):
    for g in range(u_ref.shape[0]):
        s = jnp.dot(u_ref[g], ops_ref[g], preferred_element_type=F32)
        for k, s_ref in enumerate(s_refs):
            s_ref[g] = s[:, k * STATE_W:(k + 1) * STATE_W]


def _ssm_ends(u, ops):
    gb = GB_SSM
    G = SSM_GROUPS
    return pl.pallas_call(
        _ssm_ends_kernel,
        out_shape=[jax.ShapeDtypeStruct((G, N_CHUNK_ROWS, STATE_W), F32)] * 2,
        grid=(G // gb,),
        in_specs=[pl.BlockSpec((gb, N_CHUNK_ROWS, CHUNK_W), lambda i: (i, 0, 0)),
                  pl.BlockSpec((gb, CHUNK_W, 2 * STATE_W), lambda i: (i, 0, 0))],
        out_specs=[pl.BlockSpec((gb, N_CHUNK_ROWS, STATE_W), lambda i: (i, 0, 0))] * 2,
        compiler_params=_cparams(("arbitrary",)),
        name="ssm_chunk_ends",
    )(u, ops)


def _ssm_carry_kernel(sf_ref, sb_ref, lam_ref, h0_ref, xf_ref, xb_ref, fin_ref, sfs_ref, sbs_ref):
    n_g = sf_ref.shape[0] // N_CHUNK_ROWS
    n_p, n_s = SEQ // SSM_CHUNK, DEC_SEQ // SSM_CHUNK

    def swap(t):
        return pltpu.roll(t, SSM_STATE, 1)

    sfs_ref[...] = swap(sf_ref[...])
    sbs_ref[...] = swap(sb_ref[...])

    def step(x, lam, rf, rb):
        xf, xfs, xb, xbs = x
        a1f, a2f, a1b, a2b = lam
        xf_ref[rf, :] = xf
        xb_ref[rb, :] = xb
        return [a1f * xf + a2f * xfs + sf_ref[rf, :], a1f * xfs - a2f * xf + sfs_ref[rf, :],
                a1b * xb + a2b * xbs + sb_ref[rb, :], a1b * xbs - a2b * xb + sbs_ref[rb, :]]

    zero = jnp.zeros((BATCH, STATE_W), F32)
    for g0 in range(0, n_g, GB_CARRY):
        groups = range(g0, g0 + GB_CARRY)
        lam = [[jnp.broadcast_to(lam_ref[4 * g + k:4 * g + k + 1, :], (BATCH, STATE_W)) for k in range(4)]
               for g in groups]

        def body_p(c, carry, groups=groups, lam=lam):
            out = []
            for j, g in enumerate(groups):
                rf = pl.ds(g * N_CHUNK_ROWS + c, BATCH, stride=n_p)
                rb = pl.ds(g * N_CHUNK_ROWS + (n_p - 1 - c), BATCH, stride=n_p)
                out += step(carry[4 * j:4 * j + 4], lam[j], rf, rb)
            return tuple(out)

        fin = lax.fori_loop(0, n_p, body_p, (zero,) * (4 * GB_CARRY), unroll=2)
        for j, g in enumerate(groups):
            fin_ref[g, 0] = fin[4 * j]
            fin_ref[g, 1] = fin[4 * j + 2]

    packs = [(g0, b) for g0 in range(0, n_g, 8) for b in range(DEC_BATCH)]
    lam = {g0: [lam_ref[pl.ds(4 * g0 + k, 8, stride=4), :] for k in range(4)] for g0 in range(0, n_g, 8)}

    def body_s(c, carry):
        out = []
        for j, (g0, b) in enumerate(packs):
            base = g0 * N_CHUNK_ROWS + N_CHUNK_PROMPT + b * n_s
            rf = pl.ds(base + c, 8, stride=N_CHUNK_ROWS)
            rb = pl.ds(base + (n_s - 1 - c), 8, stride=N_CHUNK_ROWS)
            out += step(carry[4 * j:4 * j + 4], lam[g0], rf, rb)
        return tuple(out)

    init = []
    for g0, b in packs:
        h0f = h0_ref[pl.ds(4 * g0 + b, 8, stride=4), :]
        h0b = h0_ref[pl.ds(4 * g0 + DEC_BATCH + b, 8, stride=4), :]
        init += [h0f, swap(h0f), h0b, swap(h0b)]
    lax.fori_loop(0, n_s, body_s, tuple(init), unroll=2)


def _ssm_carry(sf, sb, lam, h0):
    gb = GB_CARRY_BLOCK
    G = SSM_GROUPS
    st = pl.BlockSpec((gb * N_CHUNK_ROWS, STATE_W), lambda i: (i, 0))
    small = pl.BlockSpec((gb * 4, STATE_W), lambda i: (i, 0))
    return pl.pallas_call(
        _ssm_carry_kernel,
        out_shape=[jax.ShapeDtypeStruct(sf.shape, F32)] * 2 + [jax.ShapeDtypeStruct((G, 2, BATCH, STATE_W), F32)],
        grid=(G // gb,),
        in_specs=[st, st, small, small],
        out_specs=[st, st, pl.BlockSpec((gb, 2, BATCH, STATE_W), lambda i: (i, 0, 0, 0))],
        scratch_shapes=[pltpu.VMEM((gb * N_CHUNK_ROWS, STATE_W), F32)] * 2,
        compiler_params=_cparams(("arbitrary",)),
        name="ssm_carry",
    )(sf, sb, lam, h0)


def _ssm_out_kernel(u_ref, toep_ref, xf_ref, xb_ref, wct_ref, y_ref, y_scr):
    for g in range(GRANULES):
        y = jnp.dot(u_ref[g], toep_ref[g], preferred_element_type=F32)
        for d, x_ref in enumerate((xf_ref, xb_ref)):
            xin = x_ref[g].astype(BF16)
            w = wct_ref[g, :, d * STATE_W:(d + 1) * STATE_W]
            y += lax.dot_general(xin, w, (((1,), (1,)), ((), ())), preferred_element_type=F32)
        y_scr[g] = y
    rb = OUT_ROW_BLOCK
    for r in range(N_CHUNK_ROWS // rb):
        for half in range(SSM_CHUNK // GRANULES):
            a = [y_scr[g, r * rb:(r + 1) * rb, half * LANES:(half + 1) * LANES] for g in range(GRANULES)]
            b = _granule_transpose(a)
            for s in range(GRANULES):
                y_ref[pl.ds(r * rb * SSM_CHUNK + GRANULES * half + s, rb, stride=SSM_CHUNK), :] = b[s]


def _ssm_out(u, toep, xf, xb, wct):
    gb = GRANULES
    G = SSM_GROUPS
    return pl.pallas_call(
        _ssm_out_kernel,
        out_shape=jax.ShapeDtypeStruct((N_TOK, D_MODEL), F32),
        grid=(G // gb,),
        in_specs=[pl.BlockSpec((gb, N_CHUNK_ROWS, CHUNK_W), lambda i: (i, 0, 0)),
                  pl.BlockSpec((gb, CHUNK_W, CHUNK_W), lambda i: (i, 0, 0)),
                  pl.BlockSpec((gb, N_CHUNK_ROWS, STATE_W), lambda i: (i, 0, 0)),
                  pl.BlockSpec((gb, N_CHUNK_ROWS, STATE_W), lambda i: (i, 0, 0)),
                  pl.BlockSpec((gb, CHUNK_W, 2 * STATE_W), lambda i: (i, 0, 0))],
        out_specs=pl.BlockSpec((N_TOK, LANES), lambda i: (0, i)),
        scratch_shapes=[pltpu.VMEM((gb, N_CHUNK_ROWS, CHUNK_W), F32)],
        compiler_params=_cparams(("arbitrary",)),
        name="ssm_chunk_out",
    )(u, toep, xf, xb, wct)


def _ssm_post_kernel(x_ref, mod_ref, g_ref, y_ref, d_ref, w_ref, b_ref, o_ref):
    m = mod_ref[0]
    x = x_ref[...]
    h = _rms(x, g_ref[...]) * (1.0 + m[1:2]) + m[0:1]
    y = h * d_ref[...] + y_ref[...]
    gl = jax.nn.gelu(y).astype(BF16)
    z = jnp.dot(gl, w_ref[...], preferred_element_type=F32) + b_ref[...]
    o_ref[...] = x + m[2:3] * (z[:, :D_MODEL] * jax.nn.sigmoid(z[:, D_MODEL:]))


def _ssm_post(x, mod_l, g1, y, d_skip, w_glu, b_glu):
    tm = TM_SSM
    row = _mod_row(tm)
    full = lambda a: pl.BlockSpec(a.shape, lambda i: (0,) * a.ndim)
    d2, b2 = d_skip.reshape(1, D_MODEL), b_glu.reshape(1, 2 * D_MODEL)
    g2 = g1.reshape(1, D_MODEL)
    return pl.pallas_call(
        _ssm_post_kernel,
        out_shape=jax.ShapeDtypeStruct((N_TOK, D_MODEL), F32),
        grid=(N_TOK // tm,),
        in_specs=[pl.BlockSpec((tm, D_MODEL), lambda i: (i, 0)),
                  pl.BlockSpec((1, 6, D_MODEL), lambda i: (row(i), 0, 0)),
                  full(g2), pl.BlockSpec((tm, D_MODEL), lambda i: (i, 0)), full(d2), full(w_glu), full(b2)],
        out_specs=pl.BlockSpec((tm, D_MODEL), lambda i: (i, 0)),
        compiler_params=_cparams(("arbitrary",)),
        name="ssm_post",
    )(x, mod_l, g2, y, d2, w_glu, b2)


def _rope_tables():
    rows = DEC_SEQ // GRID_W
    t = np.arange(DEC_SEQ)
    row, col = (t // GRID_W).astype(np.float32), (t % GRID_W).astype(np.float32)

    def table(rot_dim, lane0):
        n_freq = rot_dim // 4
        inv_freq = jnp.asarray(ROPE_THETA, F32) ** (-jnp.arange(n_freq, dtype=F32) / n_freq)
        ang_row = jnp.asarray(row)[:, None] * inv_freq
        ang_col = jnp.asarray(col)[:, None] * inv_freq
        ang = jnp.concatenate([ang_row, ang_row, ang_col, ang_col], axis=1)
        sign = jnp.tile(jnp.concatenate([-jnp.ones(n_freq, F32), jnp.ones(n_freq, F32)]), 2)
        cos = jnp.ones((DEC_SEQ, HEAD_PAD), F32).at[:, lane0:lane0 + rot_dim].set(jnp.cos(ang))
        sin = jnp.zeros((DEC_SEQ, HEAD_PAD), F32).at[:, lane0:lane0 + rot_dim].set(jnp.sin(ang) * sign)
        return cos, sin

    assert rows * GRID_W == DEC_SEQ
    return table(MLA_ROPE, MLA_NOPE) + table(GQA_DIM, 0)


def _pad_heads(w, n_heads, dim):
    lead = w.shape[:-1]
    w = w.reshape(lead + (n_heads, dim))
    w = jnp.pad(w, [(0, 0)] * len(lead) + [(0, 0), (0, HEAD_PAD - dim)])
    return w.reshape(lead + (n_heads * HEAD_PAD,))


def _attn_weights(w_in, g_qa, g_kva, w_uq, w_ukv, g_mq, g_mk, g_gq, g_gk, w_out):
    o1, o2, o3 = Q_LORA, Q_LORA + KV_LORA, Q_LORA + KV_LORA + MLA_ROPE
    o4 = o3 + GQA_HEADS * GQA_DIM
    o5 = o4 + GQA_KV_HEADS * GQA_DIM
    kpe = jnp.pad(w_in[:, o2:o3], ((0, 0), (MLA_NOPE, HEAD_PAD - MLA_QK)))
    w_in_ext = jnp.concatenate([w_in[:, :o2], kpe, _pad_heads(w_in[:, o3:o4], GQA_HEADS, GQA_DIM),
                                _pad_heads(w_in[:, o4:o5], GQA_KV_HEADS, GQA_DIM), w_in[:, o5:]], axis=1)
    ukv = w_ukv.reshape(KV_LORA, MLA_HEADS, MLA_NOPE + MLA_V)
    w_ukv_perm = jnp.concatenate([_pad_heads(ukv[:, :, :MLA_NOPE].reshape(KV_LORA, -1), MLA_HEADS, MLA_NOPE),
                                  ukv[:, :, MLA_NOPE:].reshape(KV_LORA, -1)], axis=1)
    pad_g = lambda g: jnp.pad(g, (0, HEAD_PAD - g.shape[0])).reshape(1, HEAD_PAD)
    return dict(w_in=w_in_ext.astype(BF16), g_qa=g_qa.reshape(1, -1), g_kva=g_kva.reshape(1, -1),
                w_uq=_pad_heads(w_uq, MLA_HEADS, MLA_QK).astype(BF16), w_ukv=w_ukv_perm.astype(BF16),
                g_mq=pad_g(g_mq), g_mk=pad_g(g_mk), g_gq=pad_g(g_gq), g_gk=pad_g(g_gk),
                w_out=w_out.astype(BF16))


def _attn_layer(x, mod_l, g1, aw, tabs, cache):
    q, k, v, ckv, kpe, gk, gv = _attn_pre(x, mod_l, g1, aw, tabs)
    c_ckv, c_krope, c_gk, c_gv = cache
    n_c = DEC_BATCH * PAST_LEN
    kpe_c = jnp.pad(c_krope.reshape(n_c, MLA_ROPE), ((0, 0), (MLA_NOPE, HEAD_PAD - MLA_QK)))
    km_c, vm_c = _cache_kv(c_ckv.reshape(n_c, KV_LORA), kpe_c, aw)
    k_c = jnp.concatenate([km_c, _pad_heads(c_gk.reshape(n_c, -1), GQA_KV_HEADS, GQA_DIM).astype(BF16)], axis=1)
    v_c = jnp.concatenate([vm_c, c_gv.reshape(n_c, -1).astype(BF16)], axis=1)
    x = _attention(q, [(k, v, SEQ, 0)], x, mod_l, aw["w_out"], SEQ, BATCH, SEQ, 0, lambda b: 0)
    x = _attention(q, [(k_c, v_c, PAST_LEN, 0), (k, v, DEC_SEQ, N_PROMPT // DEC_SEQ)], x, mod_l, aw["w_out"],
                   TQ_SAMPLE, DEC_BATCH, DEC_SEQ, N_PROMPT, lambda b: 1 + b)
    new = (ckv[:N_PROMPT].reshape(BATCH, SEQ, KV_LORA),
           kpe[:N_PROMPT, MLA_NOPE:MLA_QK].reshape(BATCH, SEQ, MLA_ROPE),
           gk[:N_PROMPT].reshape(BATCH, SEQ, GQA_KV_HEADS, HEAD_PAD)[..., :GQA_DIM],
           gv[:N_PROMPT].reshape(BATCH, SEQ, GQA_KV_HEADS, GQA_DIM))
    return x, new


def _ssm_layer(x, mod_l, g1, a_re, a_im, log_dt, b_re, b_im, c_re, c_im, d_skip, w_glu, b_glu, state0):
    G, P, H = SSM_GROUPS, SSM_STATE, SSM_GROUP
    a = jnp.stack([a_re, a_im], axis=0).transpose(0, 2, 1, 3).reshape(2, G * 2, P)
    dt = jnp.broadcast_to(log_dt.transpose(1, 0)[:, :, None], (G, 2, P)).reshape(G * 2, P)
    bt = jnp.stack([b_re, b_im], axis=2).transpose(1, 0, 2, 4, 3)
    c = jnp.stack([c_re, c_im], axis=2).transpose(1, 0, 2, 3, 4)
    ops, toep, wct, lam = _ssm_ops(a, dt, bt, c)

    u = _ssm_in(x, mod_l, g1)
    sf, sb = _ssm_ends(u, ops)
    h0 = state0.transpose(2, 1, 0, 4, 3).reshape(G * 2 * DEC_BATCH, STATE_W)
    flat = lambda s: s.reshape(G * N_CHUNK_ROWS, STATE_W)
    xf, xb, fin = _ssm_carry(flat(sf), flat(sb), lam.reshape(G * 4, STATE_W), h0)
    y = _ssm_out(u, toep, xf.reshape(sf.shape), xb.reshape(sb.shape), wct)
    x = _ssm_post(x, mod_l, g1, y, d_skip, w_glu.astype(BF16), b_glu)
    fin = fin.reshape(G, 2, BATCH, 2, P).transpose(2, 1, 0, 4, 3)
    return x, fin


def kernel(x_prompt, x_sample, c, cache_mla_ckv, cache_mla_krope, cache_gqa_k, cache_gqa_v, state_ssm, c_ctx,
           norm1_g, norm2_g, w_mod, b_mod,
           attn_w_in, attn_qa_norm_g, attn_kva_norm_g, attn_w_uq, attn_w_ukv,
           attn_mla_q_norm_g, attn_mla_k_norm_g, attn_gqa_q_norm_g, attn_gqa_k_norm_g, attn_w_out,
           ssm_a_re, ssm_a_im, ssm_log_dt, ssm_b_re, ssm_b_im, ssm_c_re, ssm_c_im, ssm_d, ssm_w_glu, ssm_b_glu,
           ffn_w_up, ffn_conv_w, ffn_conv_b, ffn_w_down):
    x = jnp.concatenate([x_prompt.reshape(N_PROMPT, D_MODEL), x_sample.reshape(N_SAMPLE, D_MODEL)], axis=0)
    cond8 = jnp.concatenate([c_ctx[None, :], c, jnp.zeros((8 - 1 - DEC_BATCH, D_MODEL), F32)], axis=0)
    mod = _modulation(cond8, w_mod, b_mod).reshape(DEPTH, 8, 6, D_MODEL)
    tabs = _rope_tables()

    w_up_bf, w_down_bf = ffn_w_up.astype(BF16), ffn_w_down.astype(BF16)
    new_attn, new_ssm = [], []
    for i in range(DEPTH):
        j = i // 2
        if i % 2 == 0:
            aw = _attn_weights(attn_w_in[j], attn_qa_norm_g[j], attn_kva_norm_g[j], attn_w_uq[j], attn_w_ukv[j],
                               attn_mla_q_norm_g[j], attn_mla_k_norm_g[j], attn_gqa_q_norm_g[j],
                               attn_gqa_k_norm_g[j], attn_w_out[j])
            cache = (cache_mla_ckv[:, j], cache_mla_krope[:, j], cache_gqa_k[:, j], cache_gqa_v[:, j])
            x, new = _attn_layer(x, mod[i], norm1_g[i], aw, tabs, cache)
            new_attn.append(new)
        else:
            x, fin = _ssm_layer(x, mod[i], norm1_g[i], ssm_a_re[j], ssm_a_im[j], ssm_log_dt[j], ssm_b_re[j],
                                ssm_b_im[j], ssm_c_re[j], ssm_c_im[j], ssm_d[j], ssm_w_glu[j], ssm_b_glu[j],
                                state_ssm[:, j])
            new_ssm.append(fin)
        ffn = functools.partial(_conv_ffn, x, mod[i], norm2_g[i], i, w_up_bf, ffn_conv_w, ffn_conv_b, w_down_bf)
        if i < DEPTH - 1:
            x = ffn()
        else:
            y_prompt, y_sample = ffn(0, N_PROMPT), ffn(N_PROMPT, N_SAMPLE)

    outs = [jnp.stack([n[k] for n in new_attn], axis=1) for k in range(4)]
    return (y_prompt.reshape(BATCH, SEQ, D_MODEL), y_sample.reshape(DEC_BATCH, DEC_SEQ, D_MODEL),
            outs[0], outs[1], outs[2], outs[3], jnp.stack(new_ssm, axis=1))
```

```python
import functools
import math

import jax
import jax.numpy as jnp
import numpy as np
from jax import lax
from jax.experimental import pallas as pl
from jax.experimental.pallas import tpu as pltpu

F32 = jnp.float32
BF16 = jnp.bfloat16
HIGHEST = lax.Precision.HIGHEST

LANES = 128
BF16_SUBLANES = 16
VMEM_LIMIT_BYTES = 56 * 1024 * 1024

D_MODEL = 1024
BATCH = 16
SEQ = 256
DEPTH = 4
DEC_BATCH = 2
DEC_SEQ = 2048
PAST_LEN = 256
GRID_W = 64
MLA_HEADS = 8
Q_LORA = 384
KV_LORA = 256
MLA_NOPE = 64
MLA_ROPE = 32
MLA_QK = MLA_NOPE + MLA_ROPE
MLA_V = 64
GQA_HEADS = 8
GQA_KV_HEADS = 2
GQA_DIM = 64
GQA_REP = GQA_HEADS // GQA_KV_HEADS
ROPE_THETA = 10000.0
SSM_GROUP = 16
SSM_GROUPS = D_MODEL // SSM_GROUP
SSM_STATE = 64
D_FF = 2816
EPS = 1e-6

N_PROMPT = BATCH * SEQ
N_SAMPLE = DEC_BATCH * DEC_SEQ
N_TOK = N_PROMPT + N_SAMPLE
N_HEADS = MLA_HEADS + GQA_HEADS
HEAD_PAD = LANES
KV_LEN_SAMPLE = PAST_LEN + DEC_SEQ

IN_Q = 0
IN_CKV = IN_Q + Q_LORA
IN_KPE = IN_CKV + KV_LORA
IN_GQ = IN_KPE + HEAD_PAD
IN_GK = IN_GQ + GQA_HEADS * HEAD_PAD
IN_GV = IN_GK + GQA_KV_HEADS * HEAD_PAD
IN_EXT = IN_GV + GQA_KV_HEADS * GQA_DIM

K_ALL = (MLA_HEADS + GQA_KV_HEADS) * HEAD_PAD
V_ALL = MLA_HEADS * MLA_V + GQA_KV_HEADS * GQA_DIM

SSM_CHUNK = 16
CHUNK_W = SSM_CHUNK * SSM_GROUP
N_CHUNK_ROWS = N_TOK // SSM_CHUNK
N_CHUNK_PROMPT = N_PROMPT // SSM_CHUNK
STATE_W = 2 * SSM_STATE

TM_FFN = 512
TN_FFN = 256
FFN_HALO = BF16_SUBLANES
TM_PRE = 512
TQ_SAMPLE = 256
TM_SSM = 1024
TM_SSM_IN = 512
OUT_ROW_BLOCK = 32
GB_CARRY = 4
GB_CARRY_BLOCK = 16
TN_MOD = 1536
GB_SSM = 8


def _cparams(sem):
    return pltpu.CompilerParams(dimension_semantics=sem, vmem_limit_bytes=VMEM_LIMIT_BYTES)


def _rms(x, g):
    return x * lax.rsqrt(jnp.mean(x * x, axis=-1, keepdims=True) + EPS) * g


def _mod_row(tm):
    n_prompt_tiles = N_PROMPT // tm
    tiles_per_seq = DEC_SEQ // tm

    def row(i):
        return jnp.where(i < n_prompt_tiles, 0, 1 + (i - n_prompt_tiles) // tiles_per_seq)

    return row


def _mod_kernel(cond_ref, w_ref, b_ref, o_ref):
    a = jax.nn.silu(cond_ref[...])
    w = w_ref[0]
    a_hi, w_hi = a.astype(BF16), w.astype(BF16)
    a_lo = (a - a_hi.astype(F32)).astype(BF16)
    w_lo = (w - w_hi.astype(F32)).astype(BF16)
    dot = functools.partial(jnp.dot, preferred_element_type=F32)
    o_ref[0] = dot(a_hi, w_hi) + dot(a_lo, w_hi) + dot(a_hi, w_lo) + b_ref[0]


def _modulation(cond8, w_mod, b_mod):
    n = w_mod.shape[-1]
    return pl.pallas_call(
        _mod_kernel,
        out_shape=jax.ShapeDtypeStruct((DEPTH, 8, n), F32),
        grid=(DEPTH, n // TN_MOD),
        in_specs=[
            pl.BlockSpec((8, D_MODEL), lambda l, j: (0, 0)),
            pl.BlockSpec((1, D_MODEL, TN_MOD), lambda l, j: (l, 0, j)),
            pl.BlockSpec((1, 1, TN_MOD), lambda l, j: (l, 0, j)),
        ],
        out_specs=pl.BlockSpec((1, 8, TN_MOD), lambda l, j: (l, 0, j)),
        compiler_params=_cparams(("arbitrary", "arbitrary")),
        name="modulation",
    )(cond8, w_mod, b_mod.reshape(DEPTH, 1, n))


def _ffn_kernel(xp_ref, x_ref, xn_ref, mod_ref, g_ref, wup_ref, cw_ref, cb_ref, wd_ref, o_ref,
                h_scr, act_scr, z_scr, *, tile0):
    i = pl.program_id(0) + tile0
    tm = x_ref.shape[0]
    tn = TN_FFN
    m = mod_ref[0]
    sh, sc, gt = m[3:4], m[4:5], m[5:6]
    g = g_ref[...]

    def hmod(x):
        return (_rms(x, g) * (1.0 + sc) + sh).astype(BF16)

    h_scr[0:FFN_HALO] = hmod(xp_ref[...])
    h_scr[FFN_HALO:FFN_HALO + tm] = hmod(x_ref[...])
    h_scr[FFN_HALO + tm:] = hmod(xn_ref[...])
    seq_len = jnp.where(i * tm < N_PROMPT, SEQ, DEC_SEQ)
    pos = (i * tm + lax.broadcasted_iota(jnp.int32, (tm, 1), 0)) & (seq_len - 1)
    eg = BF16_SUBLANES
    bounds = range(0, tm + 1, SEQ)
    cuts = sorted({0, tm} | {b + o for b in bounds for o in (-eg, eg) if 0 < b + o < tm})
    segments = [(r0, r1, any(r0 <= b < r1 or r0 < b <= r1 for b in bounds)) for r0, r1 in zip(cuts[:-1], cuts[1:])]

    def conv(slot, k, col, r0, r1, masked):
        cw = cw_ref[:, col:col + tn]
        zp = z_scr[slot, k, FFN_HALO - 1 + r0:FFN_HALO - 1 + r1, :]
        zc = z_scr[slot, k, FFN_HALO + r0:FFN_HALO + r1, :]
        zn = z_scr[slot, k, FFN_HALO + 1 + r0:FFN_HALO + 1 + r1, :]
        if masked:
            zp = jnp.where(pos[r0:r1] == 0, 0.0, zp)
            zn = jnp.where(pos[r0:r1] == seq_len - 1, 0.0, zn)
        return zp * cw[0:1] + zc * cw[1:2] + zn * cw[2:3] + cb_ref[:, col:col + tn]

    h = h_scr[...]
    for jc in range(D_FF // tn):
        ca, cb = jc * tn, D_FF + jc * tn
        slot = jc % 2
        z_scr[slot, 0] = jnp.dot(h, wup_ref[:, ca:ca + tn], preferred_element_type=F32)
        z_scr[slot, 1] = jnp.dot(h, wup_ref[:, cb:cb + tn], preferred_element_type=F32)
        for r0, r1, masked in segments:
            act_scr[r0:r1, ca:ca + tn] = (jax.nn.silu(conv(slot, 0, ca, r0, r1, masked))
                                          * conv(slot, 1, cb, r0, r1, masked)).astype(BF16)
    o_ref[...] = x_ref[...] + gt * jnp.dot(act_scr[...], wd_ref[...], preferred_element_type=F32)


def _conv_ffn(x, mod_l, g, layer, w_up, conv_w, conv_b, w_down, tok0=0, n_tok=N_TOK):
    tm, halo = TM_FFN, FFN_HALO
    row = _mod_row(tm)
    n_halo_blocks = N_TOK // halo
    tile0 = tok0 // tm
    resident = lambda a: pl.BlockSpec((None,) + a.shape[1:], lambda i: (layer,) + (0,) * (a.ndim - 1),
                                      pipeline_mode=pl.Buffered(1))
    cb3 = conv_b.reshape(DEPTH, 1, -1)
    return pl.pallas_call(
        functools.partial(_ffn_kernel, tile0=tile0),
        out_shape=jax.ShapeDtypeStruct((n_tok, D_MODEL), F32),
        grid=(n_tok // tm,),
        in_specs=[
            pl.BlockSpec((halo, D_MODEL), lambda i: (jnp.maximum((i + tile0) * (tm // halo) - 1, 0), 0)),
            pl.BlockSpec((tm, D_MODEL), lambda i: (i + tile0, 0)),
            pl.BlockSpec((halo, D_MODEL),
                         lambda i: (jnp.minimum((i + tile0 + 1) * (tm // halo), n_halo_blocks - 1), 0)),
            pl.BlockSpec((1, 6, D_MODEL), lambda i: (row(i + tile0), 0, 0)),
            pl.BlockSpec((1, D_MODEL), lambda i: (0, 0)),
            resident(w_up), resident(conv_w), resident(cb3), resident(w_down),
        ],
        out_specs=pl.BlockSpec((tm, D_MODEL), lambda i: (i, 0)),
        scratch_shapes=[pltpu.VMEM((tm + 2 * halo, D_MODEL), BF16), pltpu.VMEM((tm, D_FF), BF16),
                        pltpu.VMEM((2, 2, tm + 2 * halo, TN_FFN), F32)],
        compiler_params=_cparams(("arbitrary",)),
        name="conv_ffn",
    )(x, x, x, mod_l, g.reshape(1, D_MODEL), w_up, conv_w, cb3, w_down)


def _rope(x, cos, sin, half):
    lane = lax.broadcasted_iota(jnp.int32, x.shape, 1)
    first = ((lane // half) & 1) == 0
    partner = jnp.where(first, pltpu.roll(x, LANES - half, 1), pltpu.roll(x, half, 1))
    return x * cos + partner * sin


def _head_norm(xh, g, dim):
    sq = xh * xh
    hi = sq.astype(BF16)
    lo = (sq - hi.astype(F32)).astype(BF16)
    ones = jnp.ones((2 * HEAD_PAD, HEAD_PAD), BF16)
    total = jnp.dot(jnp.concatenate([hi, lo], axis=1), ones, preferred_element_type=F32)
    return xh * lax.rsqrt(total * (1.0 / dim) + EPS) * g


def _mla_kv_heads(ckv_bf, kpe, w_ukv_ref, g_mk, cos_m, sin_m, k_ref, v_ref):
    kv = jnp.dot(ckv_bf, w_ukv_ref[...], preferred_element_type=F32)
    for h in range(MLA_HEADS):
        kh = kv[:, h * HEAD_PAD:(h + 1) * HEAD_PAD] + kpe
        kh = _head_norm(kh, g_mk, MLA_QK)
        if cos_m is not None:
            kh = _rope(kh, cos_m, sin_m, MLA_ROPE // 4)
        k_ref[:, h * HEAD_PAD:(h + 1) * HEAD_PAD] = kh.astype(BF16)
    v_ref[:, 0:MLA_HEADS * MLA_V] = kv[:, MLA_HEADS * HEAD_PAD:].astype(BF16)


def _attn_pre_kernel(x_ref, mod_ref, g1_ref, w_in_ref, g_qa_ref, g_kva_ref, w_uq_ref, w_ukv_ref,
                     g_mq_ref, g_mk_ref, g_gq_ref, g_gk_ref, cos_m_ref, sin_m_ref, cos_g_ref, sin_g_ref,
                     q_ref, k_ref, v_ref, ckv_ref, kpe_ref, gk_ref, gv_ref):
    m = mod_ref[0]
    sh, sc = m[0:1], m[1:2]
    h = (_rms(x_ref[...], g1_ref[...]) * (1.0 + sc) + sh).astype(BF16)
    p = jnp.dot(h, w_in_ref[...], preferred_element_type=F32)
    q_c = _rms(p[:, IN_Q:IN_CKV], g_qa_ref[...]).astype(BF16)
    ckv = _rms(p[:, IN_CKV:IN_KPE], g_kva_ref[...])
    kpe = p[:, IN_KPE:IN_GQ]
    ckv_ref[...] = ckv
    kpe_ref[...] = kpe
    gv = p[:, IN_GV:IN_EXT]
    gv_ref[...] = gv
    v_ref[:, MLA_HEADS * MLA_V:] = gv.astype(BF16)
    qm = jnp.dot(q_c, w_uq_ref[...], preferred_element_type=F32)

    def heads(rope):
        cos_m, sin_m = (cos_m_ref[...], sin_m_ref[...]) if rope else (None, None)
        cos_g, sin_g = (cos_g_ref[...], sin_g_ref[...]) if rope else (None, None)
        g_mq = g_mq_ref[...]
        for hd in range(MLA_HEADS):
            qh = _head_norm(qm[:, hd * HEAD_PAD:(hd + 1) * HEAD_PAD], g_mq, MLA_QK)
            if rope:
                qh = _rope(qh, cos_m, sin_m, MLA_ROPE // 4)
            q_ref[:, hd * HEAD_PAD:(hd + 1) * HEAD_PAD] = (qh * (1.0 / math.sqrt(MLA_QK))).astype(BF16)
        _mla_kv_heads(ckv.astype(BF16), kpe, w_ukv_ref, g_mk_ref[...], cos_m, sin_m, k_ref, v_ref)
        g_gq = g_gq_ref[...]
        for hd in range(GQA_HEADS):
            qh = _head_norm(p[:, IN_GQ + hd * HEAD_PAD:IN_GQ + (hd + 1) * HEAD_PAD], g_gq, GQA_DIM)
            if rope:
                qh = _rope(qh, cos_g, sin_g, GQA_DIM // 4)
            q_ref[:, (MLA_HEADS + hd) * HEAD_PAD:(MLA_HEADS + hd + 1) * HEAD_PAD] = (
                qh * (1.0 / math.sqrt(GQA_DIM))).astype(BF16)
        g_gk = g_gk_ref[...]
        for hd in range(GQA_KV_HEADS):
            kh = _head_norm(p[:, IN_GK + hd * HEAD_PAD:IN_GK + (hd + 1) * HEAD_PAD], g_gk, GQA_DIM)
            gk_ref[:, hd * HEAD_PAD:(hd + 1) * HEAD_PAD] = kh
            if rope:
                kh = _rope(kh, cos_g, sin_g, GQA_DIM // 4)
            k_ref[:, (MLA_HEADS + hd) * HEAD_PAD:(MLA_HEADS + hd + 1) * HEAD_PAD] = kh.astype(BF16)

    is_latent = pl.program_id(0) >= N_PROMPT // x_ref.shape[0]
    pl.when(is_latent)(lambda: heads(True))
    pl.when(jnp.logical_not(is_latent))(lambda: heads(False))


def _attn_pre(x, mod_l, g1, aw, tabs):
    tm = TM_PRE
    row = _mod_row(tm)
    full = lambda a: pl.BlockSpec(a.shape, lambda i: (0,) * a.ndim)
    tok = lambda w: pl.BlockSpec((tm, w), lambda i: (i, 0))
    pos_block = lambda i: (jnp.maximum(i - N_PROMPT // tm, 0) % (DEC_SEQ // tm), 0)
    consts = [g1.reshape(1, D_MODEL), aw["w_in"], aw["g_qa"], aw["g_kva"], aw["w_uq"], aw["w_ukv"],
              aw["g_mq"], aw["g_mk"], aw["g_gq"], aw["g_gk"]]
    return pl.pallas_call(
        _attn_pre_kernel,
        out_shape=[
            jax.ShapeDtypeStruct((N_TOK, N_HEADS * HEAD_PAD), BF16),
            jax.ShapeDtypeStruct((N_TOK, K_ALL), BF16),
            jax.ShapeDtypeStruct((N_TOK, V_ALL), BF16),
            jax.ShapeDtypeStruct((N_TOK, KV_LORA), F32),
            jax.ShapeDtypeStruct((N_TOK, HEAD_PAD), F32),
            jax.ShapeDtypeStruct((N_TOK, GQA_KV_HEADS * HEAD_PAD), F32),
            jax.ShapeDtypeStruct((N_TOK, GQA_KV_HEADS * GQA_DIM), F32),
        ],
        grid=(N_TOK // tm,),
        in_specs=[tok(D_MODEL), pl.BlockSpec((1, 6, D_MODEL), lambda i: (row(i), 0, 0))]
        + [full(a) for a in consts] + [pl.BlockSpec((tm, HEAD_PAD), pos_block)] * 4,
        out_specs=[tok(N_HEADS * HEAD_PAD), tok(K_ALL), tok(V_ALL), tok(KV_LORA), tok(HEAD_PAD),
                   tok(GQA_KV_HEADS * HEAD_PAD), tok(GQA_KV_HEADS * GQA_DIM)],
        compiler_params=_cparams(("arbitrary",)),
        name="attn_pre",
    )(x, mod_l, *consts, *tabs)


def _cache_kv_kernel(ckv_ref, kpe_ref, w_ukv_ref, g_mk_ref, k_ref, v_ref):
    _mla_kv_heads(ckv_ref[...].astype(BF16), kpe_ref[...], w_ukv_ref, g_mk_ref[...], None, None, k_ref, v_ref)


def _cache_kv(ckv, kpe_pad, aw):
    n = ckv.shape[0]
    return pl.pallas_call(
        _cache_kv_kernel,
        out_shape=[jax.ShapeDtypeStruct((n, MLA_HEADS * HEAD_PAD), BF16),
                   jax.ShapeDtypeStruct((n, MLA_HEADS * MLA_V), BF16)],
        name="cache_kv",
    )(ckv, kpe_pad, aw["w_ukv"], aw["g_mk"])


def _attn_kernel(*refs, n_seg):
    q_ref, kv_refs = refs[0], refs[1:1 + 2 * n_seg]
    x_ref, mod_ref, wo_ref, o_ref, oh_scr = refs[1 + 2 * n_seg:]
    for hd in range(N_HEADS):
        if hd < MLA_HEADS:
            kc, vc = hd * HEAD_PAD, hd * MLA_V
        else:
            kvh = (hd - MLA_HEADS) // GQA_REP
            kc, vc = (MLA_HEADS + kvh) * HEAD_PAD, MLA_HEADS * MLA_V + kvh * GQA_DIM
        q = q_ref[:, hd * HEAD_PAD:(hd + 1) * HEAD_PAD]
        ss = [lax.dot_general(q, kv_refs[2 * j][:, kc:kc + HEAD_PAD], (((1,), (1,)), ((), ())),
                              preferred_element_type=F32) for j in range(n_seg)]
        m = functools.reduce(jnp.maximum, [jnp.max(s, axis=-1, keepdims=True) for s in ss])
        ps = [jnp.exp(s - m) for s in ss]
        den = sum(jnp.sum(p, axis=-1, keepdims=True) for p in ps)
        o = sum(jnp.dot(p.astype(BF16), kv_refs[2 * j + 1][:, vc:vc + MLA_V], preferred_element_type=F32)
                for j, p in enumerate(ps))
        oh_scr[:, hd * MLA_V:(hd + 1) * MLA_V] = (o / den).astype(BF16)
    gt = mod_ref[0][2:3]
    o_ref[...] = x_ref[...] + gt * jnp.dot(oh_scr[...], wo_ref[...], preferred_element_type=F32)


def _attention(q, segs, x, mod_l, w_out, tq, n_batch, t_len, tok0, mod_row):
    tiles = t_len // tq
    qmap = lambda bi, qi: (tok0 // tq + bi * tiles + qi, 0)
    kv_specs, kv_args = [], []
    for k, v, rows, blk0 in segs:
        kv_specs += [pl.BlockSpec((rows, K_ALL), lambda bi, qi, blk0=blk0: (blk0 + bi, 0)),
                     pl.BlockSpec((rows, V_ALL), lambda bi, qi, blk0=blk0: (blk0 + bi, 0))]
        kv_args += [k, v]
    return pl.pallas_call(
        functools.partial(_attn_kernel, n_seg=len(segs)),
        out_shape=jax.ShapeDtypeStruct((N_TOK, D_MODEL), F32),
        grid=(n_batch, tiles),
        in_specs=[pl.BlockSpec((tq, N_HEADS * HEAD_PAD), qmap)] + kv_specs + [
            pl.BlockSpec((tq, D_MODEL), qmap),
            pl.BlockSpec((1, 6, D_MODEL), lambda bi, qi: (mod_row(bi), 0, 0)),
            pl.BlockSpec((D_MODEL, D_MODEL), lambda bi, qi: (0, 0)),
        ],
        out_specs=pl.BlockSpec((tq, D_MODEL), qmap),
        scratch_shapes=[pltpu.VMEM((tq, D_MODEL), BF16)],
        input_output_aliases={1 + len(kv_args): 0},
        compiler_params=_cparams(("arbitrary", "arbitrary")),
        name="attention",
    )(q, *kv_args, x, mod_l, w_out)


def _cmul(ar, ai, br, bi):
    return ar * br - ai * bi, ar * bi + ai * br


def _ssm_ops_kernel(a_ref, dt_ref, bt_ref, c_ref, ops_ref, toep_ref, wct_ref, lam_ref,
                    cl_re_scr, cl_im_scr, op_scr, tp_scr, wc_scr):
    L, H, P = SSM_CHUNK, SSM_GROUP, SSM_STATE
    lane = lax.broadcasted_iota(jnp.int32, (H, CHUNK_W), 1)

    a_re, a_im = a_ref[0], a_ref[1]
    dt_all = jnp.exp(dt_ref[...])
    mag = jnp.exp(a_re * dt_all)
    ang = a_im * dt_all
    ab_re, ab_im = mag * jnp.cos(ang), mag * jnp.sin(ang)
    den = a_re * a_re + a_im * a_im
    n_re, n_im = ab_re - 1.0, ab_im
    k_re_all = (n_re * a_re + n_im * a_im) / den
    k_im_all = (n_im * a_re - n_re * a_im) / den
    pw_all = [(jnp.ones_like(ab_re), jnp.zeros_like(ab_im))]
    for _ in range(L):
        pw_all.append(_cmul(pw_all[-1][0], pw_all[-1][1], ab_re, ab_im))

    for g in range(ops_ref.shape[0]):
        lag = []
        for d in range(2):
            r = 2 * g + d
            bt_re, bt_im = bt_ref[g, d, 0], bt_ref[g, d, 1]
            bb_re, bb_im = _cmul(k_re_all[r:r + 1], k_im_all[r:r + 1], bt_re, bt_im)
            c_re, c_im = c_ref[g, d, 0], c_ref[g, d, 1]
            pw = [(p_re[r:r + 1], p_im[r:r + 1]) for p_re, p_im in pw_all]

            for e in range(L + 1):
                pr, pi = pw[e] if d == 0 else pw[L - e]
                cr, ci = _cmul(c_re, c_im, pr, pi)
                cl_re_scr[r,e * H:(e + 1) * H, :] = cr
                cl_im_scr[r,e * H:(e + 1) * H, :] = ci
            lo = 0 if d == 0 else H
            lag.append(lax.dot_general(bb_re, cl_re_scr[r,lo:lo + CHUNK_W, :], (((1,), (1,)), ((), ())),
                                       preferred_element_type=F32, precision=HIGHEST)
                       - lax.dot_general(bb_im, cl_im_scr[r,lo:lo + CHUNK_W, :], (((1,), (1,)), ((), ())),
                                         preferred_element_type=F32, precision=HIGHEST))
            wlo = H if d == 0 else 0
            wc_scr[g, :, d * STATE_W:d * STATE_W + P] = cl_re_scr[r,wlo:wlo + CHUNK_W, :]
            wc_scr[g, :, d * STATE_W + P:(d + 1) * STATE_W] = -cl_im_scr[r,wlo:wlo + CHUNK_W, :]
            for j in range(L):
                pj = pw[L - 1 - j] if d == 0 else pw[j]
                sr, si = _cmul(pj[0], pj[1], bb_re, bb_im)
                op_scr[g, j * H:(j + 1) * H, d * STATE_W:d * STATE_W + P] = sr
                op_scr[g, j * H:(j + 1) * H, d * STATE_W + P:(d + 1) * STATE_W] = si
            pr, pi = pw[L]
            lam_ref[g, 2 * d:2 * d + 1, 0:P] = pr
            lam_ref[g, 2 * d:2 * d + 1, P:STATE_W] = pr
            lam_ref[g, 2 * d + 1:2 * d + 2, 0:P] = -pi
            lam_ref[g, 2 * d + 1:2 * d + 2, P:STATE_W] = pi
        for j in range(L):
            fwd = lag[0] if j == 0 else jnp.where(lane >= H * j, pltpu.roll(lag[0], H * j, 1), 0.0)
            sft = (L - 1 - j) * H
            bwd = lag[1] if sft == 0 else jnp.where(lane < H * (j + 1), pltpu.roll(lag[1], CHUNK_W - sft, 1), 0.0)
            tp_scr[g, j * H:(j + 1) * H, :] = fwd + bwd
        ops_ref[g] = op_scr[g].astype(BF16)
        toep_ref[g] = tp_scr[g].astype(BF16)
        wct_ref[g] = wc_scr[g].astype(BF16)


def _ssm_ops(a, dt, bt, c):
    gb = GB_SSM
    G = SSM_GROUPS
    blk = lambda shape: pl.BlockSpec((gb,) + shape, lambda i: (i,) + (0,) * len(shape))
    return pl.pallas_call(
        _ssm_ops_kernel,
        out_shape=[jax.ShapeDtypeStruct((G, CHUNK_W, 2 * STATE_W), BF16),
                   jax.ShapeDtypeStruct((G, CHUNK_W, CHUNK_W), BF16),
                   jax.ShapeDtypeStruct((G, CHUNK_W, 2 * STATE_W), BF16),
                   jax.ShapeDtypeStruct((G, 4, STATE_W), F32)],
        grid=(G // gb,),
        in_specs=[pl.BlockSpec((2, 2 * gb, SSM_STATE), lambda i: (0, i, 0)),
                  pl.BlockSpec((2 * gb, SSM_STATE), lambda i: (i, 0)),
                  blk((2, 2, SSM_GROUP, SSM_STATE)), blk((2, 2, SSM_GROUP, SSM_STATE))],
        out_specs=[blk((CHUNK_W, 2 * STATE_W)), blk((CHUNK_W, CHUNK_W)), blk((CHUNK_W, 2 * STATE_W)),
                   blk((4, STATE_W))],
        scratch_shapes=[pltpu.VMEM((2 * gb, (SSM_CHUNK + 1) * SSM_GROUP, SSM_STATE), F32),
                        pltpu.VMEM((2 * gb, (SSM_CHUNK + 1) * SSM_GROUP, SSM_STATE), F32),
                        pltpu.VMEM((gb, CHUNK_W, 2 * STATE_W), F32),
                        pltpu.VMEM((gb, CHUNK_W, CHUNK_W), F32),
                        pltpu.VMEM((gb, CHUNK_W, 2 * STATE_W), F32)],
        compiler_params=_cparams(("arbitrary",)),
        name="ssm_ops",
    )(a, dt, bt, c)


GRANULES = LANES // SSM_GROUP


def _granule_transpose(arrs):
    a = list(arrs)
    lane = lax.broadcasted_iota(jnp.int32, a[0].shape, 1)
    for d in (4, 2, 1):
        upper = (lane & (d * SSM_GROUP)) != 0
        for s in range(GRANULES):
            if s & d:
                continue
            lo, hi = a[s], a[s + d]
            a[s] = jnp.where(upper, pltpu.roll(hi, d * SSM_GROUP, 1), lo)
            a[s + d] = jnp.where(upper, hi, pltpu.roll(lo, LANES - d * SSM_GROUP, 1))
    return a


def _ssm_in_kernel(x_ref, mod_ref, g_ref, u_ref, h_scr):
    m = mod_ref[0]
    h = _rms(x_ref[...], g_ref[...]) * (1.0 + m[1:2]) + m[0:1]
    mb = h_scr.shape[1] // SSM_CHUNK
    for g8 in range(SSM_GROUPS // GRANULES):
        h_scr[g8] = h[:, g8 * LANES:(g8 + 1) * LANES]
    for g8 in range(SSM_GROUPS // GRANULES):
        for half in range(SSM_CHUNK // GRANULES):
            a = [h_scr[g8, pl.ds(GRANULES * half + s, mb, stride=SSM_CHUNK), :] for s in range(GRANULES)]
            b = _granule_transpose(a)
            for g in range(GRANULES):
                u_ref[g8 * GRANULES + g, :, half * LANES:(half + 1) * LANES] = b[g].astype(BF16)


def _ssm_in(x, mod_l, g1):
    tm = TM_SSM_IN
    row = _mod_row(tm)
    return pl.pallas_call(
        _ssm_in_kernel,
        out_shape=jax.ShapeDtypeStruct((SSM_GROUPS, N_CHUNK_ROWS, CHUNK_W), BF16),
        grid=(N_TOK // tm,),
        in_specs=[pl.BlockSpec((tm, D_MODEL), lambda i: (i, 0)),
                  pl.BlockSpec((1, 6, D_MODEL), lambda i: (row(i), 0, 0)),
                  pl.BlockSpec((1, D_MODEL), lambda i: (0, 0))],
        out_specs=pl.BlockSpec((SSM_GROUPS, tm // SSM_CHUNK, CHUNK_W), lambda i: (0, i, 0)),
        scratch_shapes=[pltpu.VMEM((D_MODEL // LANES, tm, LANES), F32)],
        compiler_params=_cparams(("arbitrary",)),
        name="ssm_in",
    )(x, mod_l, g1.reshape(1, D_MODEL))


def _step_major_perm():
    n = SEQ // SSM_CHUNK
    assert n == BATCH
    idx = np.arange(N_CHUNK_PROMPT).reshape(BATCH, n).T.reshape(-1)
    return jnp.asarray(np.eye(N_CHUNK_PROMPT, dtype=np.float32)[idx], BF16)


def _ssm_ends_kernel(u_ref, ops_ref, perm_ref, *s_refs):
    n_p = N_CHUNK_PROMPT
    for g in range(u_ref.shape[0]):
        u_ctx = jnp.dot(perm_ref[...], u_ref[g, 0:n_p], preferred_element_type=F32).astype(BF16)
        s_ctx = jnp.dot(u_ctx, ops_ref[g], preferred_element_type=F32)
        s_lat = jnp.dot(u_ref[g, n_p:], ops_ref[g], preferred_element_type=F32)
        for k, s_ref in enumerate(s_refs):
            s_ref[g, 0:n_p] = s_ctx[:, k * STATE_W:(k + 1) * STATE_W]
            s_ref[g, n_p:] = s_lat[:, k * STATE_W:(k + 1) * STATE_W]


def _ssm_ends(u, ops, perm):
    gb = GB_SSM
    G = SSM_GROUPS
    return pl.pallas_call(
        _ssm_ends_kernel,
        out_shape=[jax.ShapeDtypeStruct((G, N_CHUNK_ROWS, STATE_W), F32)] * 2,
        grid=(G // gb,),
        in_specs=[pl.BlockSpec((gb, N_CHUNK_ROWS, CHUNK_W), lambda i: (i, 0, 0)),
                  pl.BlockSpec((gb, CHUNK_W, 2 * STATE_W), lambda i: (i, 0, 0)),
                  pl.BlockSpec(perm.shape, lambda i: (0, 0))],
        out_specs=[pl.BlockSpec((gb, N_CHUNK_ROWS, STATE_W), lambda i: (i, 0, 0))] * 2,
        compiler_params=_cparams(("arbitrary",)),
        name="ssm_chunk_ends",
    )(u, ops, perm)


def _ssm_carry_kernel(sf_ref, sb_ref, lam_ref, h0_ref, xf_ref, xb_ref, fin_ref, sfs_ref, sbs_ref):
    n_g = sf_ref.shape[0] // N_CHUNK_ROWS
    n_p, n_s = SEQ // SSM_CHUNK, DEC_SEQ // SSM_CHUNK

    def swap(t):
        return pltpu.roll(t, SSM_STATE, 1)

    sfs_ref[...] = swap(sf_ref[...])
    sbs_ref[...] = swap(sb_ref[...])

    def step(x, lam, rf, rb):
        xf, xfs, xb, xbs = x
        a1f, a2f, a1b, a2b = lam
        xf_ref[rf, :] = xf
        xb_ref[rb, :] = xb
        return [a1f * xf + a2f * xfs + sf_ref[rf, :], a1f * xfs - a2f * xf + sfs_ref[rf, :],
                a1b * xb + a2b * xbs + sb_ref[rb, :], a1b * xbs - a2b * xb + sbs_ref[rb, :]]

    zero = jnp.zeros((BATCH, STATE_W), F32)
    for g0 in range(0, n_g, GB_CARRY):
        groups = range(g0, g0 + GB_CARRY)
        lam = [[jnp.broadcast_to(lam_ref[4 * g + k:4 * g + k + 1, :], (BATCH, STATE_W)) for k in range(4)]
               for g in groups]

        def body_p(c, carry, groups=groups, lam=lam):
            out = []
            for j, g in enumerate(groups):
                rf = pl.ds(pl.multiple_of(g * N_CHUNK_ROWS + c * BATCH, BATCH), BATCH)
                rb = pl.ds(pl.multiple_of(g * N_CHUNK_ROWS + (n_p - 1 - c) * BATCH, BATCH), BATCH)
                out += step(carry[4 * j:4 * j + 4], lam[j], rf, rb)
            return tuple(out)

        fin = lax.fori_loop(0, n_p, body_p, (zero,) * (4 * GB_CARRY), unroll=2)
        for j, g in enumerate(groups):
            fin_ref[g, 0] = fin[4 * j]
            fin_ref[g, 1] = fin[4 * j + 2]

    packs = [(g0, b) for g0 in range(0, n_g, 8) for b in range(DEC_BATCH)]
    lam = {g0: [lam_ref[pl.ds(4 * g0 + k, 8, stride=4), :] for k in range(4)] for g0 in range(0, n_g, 8)}

    def body_s(c, carry):
        out = []
        for j, (g0, b) in enumerate(packs):
            base = g0 * N_CHUNK_ROWS + N_CHUNK_PROMPT + b * n_s
            rf = pl.ds(base + c, 8, stride=N_CHUNK_ROWS)
            rb = pl.ds(base + (n_s - 1 - c), 8, stride=N_CHUNK_ROWS)
            out += step(carry[4 * j:4 * j + 4], lam[g0], rf, rb)
        return tuple(out)

    init = []
    for g0, b in packs:
        h0f = h0_ref[pl.ds(4 * g0 + b, 8, stride=4), :]
        h0b = h0_ref[pl.ds(4 * g0 + DEC_BATCH + b, 8, stride=4), :]
        init += [h0f, swap(h0f), h0b, swap(h0b)]
    lax.fori_loop(0, n_s, body_s, tuple(init), unroll=2)


def _ssm_carry(sf, sb, lam, h0):
    gb = GB_CARRY_BLOCK
    G = SSM_GROUPS
    st = pl.BlockSpec((gb * N_CHUNK_ROWS, STATE_W), lambda i: (i, 0))
    small = pl.BlockSpec((gb * 4, STATE_W), lambda i: (i, 0))
    return pl.pallas_call(
        _ssm_carry_kernel,
        out_shape=[jax.ShapeDtypeStruct(sf.shape, F32)] * 2 + [jax.ShapeDtypeStruct((G, 2, BATCH, STATE_W), F32)],
        grid=(G // gb,),
        in_specs=[st, st, small, small],
        out_specs=[st, st, pl.BlockSpec((gb, 2, BATCH, STATE_W), lambda i: (i, 0, 0, 0))],
        scratch_shapes=[pltpu.VMEM((gb * N_CHUNK_ROWS, STATE_W), F32)] * 2,
        compiler_params=_cparams(("arbitrary",)),
        name="ssm_carry",
    )(sf, sb, lam, h0)


def _ssm_out_kernel(u_ref, toep_ref, xf_ref, xb_ref, wct_ref, perm_ref, y_ref, y_scr):
    n_p = N_CHUNK_PROMPT
    for g in range(GRANULES):
        y = jnp.dot(u_ref[g], toep_ref[g], preferred_element_type=F32)
        for d, x_ref in enumerate((xf_ref, xb_ref)):
            x_ctx = jnp.dot(perm_ref[...], x_ref[g, 0:n_p].astype(BF16), preferred_element_type=F32).astype(BF16)
            xin = jnp.concatenate([x_ctx, x_ref[g, n_p:].astype(BF16)], axis=0)
            w = wct_ref[g, :, d * STATE_W:(d + 1) * STATE_W]
            y += lax.dot_general(xin, w, (((1,), (1,)), ((), ())), preferred_element_type=F32)
        y_scr[g] = y
    rb = OUT_ROW_BLOCK
    for r in range(N_CHUNK_ROWS // rb):
        for half in range(SSM_CHUNK // GRANULES):
            a = [y_scr[g, r * rb:(r + 1) * rb, half * LANES:(half + 1) * LANES] for g in range(GRANULES)]
            b = _granule_transpose(a)
            for s in range(GRANULES):
                y_ref[pl.ds(r * rb * SSM_CHUNK + GRANULES * half + s, rb, stride=SSM_CHUNK), :] = b[s]


def _ssm_out(u, toep, xf, xb, wct, perm):
    gb = GRANULES
    G = SSM_GROUPS
    return pl.pallas_call(
        _ssm_out_kernel,
        out_shape=jax.ShapeDtypeStruct((N_TOK, D_MODEL), F32),
        grid=(G // gb,),
        in_specs=[pl.BlockSpec((gb, N_CHUNK_ROWS, CHUNK_W), lambda i: (i, 0, 0)),
                  pl.BlockSpec((gb, CHUNK_W, CHUNK_W), lambda i: (i, 0, 0)),
                  pl.BlockSpec((gb, N_CHUNK_ROWS, STATE_W), lambda i: (i, 0, 0)),
                  pl.BlockSpec((gb, N_CHUNK_ROWS, STATE_W), lambda i: (i, 0, 0)),
                  pl.BlockSpec((gb, CHUNK_W, 2 * STATE_W), lambda i: (i, 0, 0)),
                  pl.BlockSpec(perm.shape, lambda i: (0, 0))],
        out_specs=pl.BlockSpec((N_TOK, LANES), lambda i: (0, i)),
        scratch_shapes=[pltpu.VMEM((gb, N_CHUNK_ROWS, CHUNK_W), F32)],
        compiler_params=_cparams(("arbitrary",)),
        name="ssm_chunk_out",
    )(u, toep, xf, xb, wct, perm)


def _ssm_post_kernel(x_ref, mod_ref, g_ref, y_ref, d_ref, w_ref, b_ref, o_ref):
    m = mod_ref[0]
    x = x_ref[...]
    h = _rms(x, g_ref[...]) * (1.0 + m[1:2]) + m[0:1]
    y = h * d_ref[...] + y_ref[...]
    gl = jax.nn.gelu(y).astype(BF16)
    z = jnp.dot(gl, w_ref[...], preferred_element_type=F32) + b_ref[...]
    o_ref[...] = x + m[2:3] * (z[:, :D_MODEL] * jax.nn.sigmoid(z[:, D_MODEL:]))


def _ssm_post(x, mod_l, g1, y, d_skip, w_glu, b_glu):
    tm = TM_SSM
    row = _mod_row(tm)
    full = lambda a: pl.BlockSpec(a.shape, lambda i: (0,) * a.ndim)
    d2, b2 = d_skip.reshape(1, D_MODEL), b_glu.reshape(1, 2 * D_MODEL)
    g2 = g1.reshape(1, D_MODEL)
    return pl.pallas_call(
        _ssm_post_kernel,
        out_shape=jax.ShapeDtypeStruct((N_TOK, D_MODEL), F32),
        grid=(N_TOK // tm,),
        in_specs=[pl.BlockSpec((tm, D_MODEL), lambda i: (i, 0)),
                  pl.BlockSpec((1, 6, D_MODEL), lambda i: (row(i), 0, 0)),
                  full(g2), pl.BlockSpec((tm, D_MODEL), lambda i: (i, 0)), full(d2), full(w_glu), full(b2)],
        out_specs=pl.BlockSpec((tm, D_MODEL), lambda i: (i, 0)),
        compiler_params=_cparams(("arbitrary",)),
        name="ssm_post",
    )(x, mod_l, g2, y, d2, w_glu, b2)


def _rope_tables():
    rows = DEC_SEQ // GRID_W
    t = np.arange(DEC_SEQ)
    row, col = (t // GRID_W).astype(np.float32), (t % GRID_W).astype(np.float32)

    def table(rot_dim, lane0):
        n_freq = rot_dim // 4
        inv_freq = jnp.asarray(ROPE_THETA, F32) ** (-jnp.arange(n_freq, dtype=F32) / n_freq)
        ang_row = jnp.asarray(row)[:, None] * inv_freq
        ang_col = jnp.asarray(col)[:, None] * inv_freq
        ang = jnp.concatenate([ang_row, ang_row, ang_col, ang_col], axis=1)
        sign = jnp.tile(jnp.concatenate([-jnp.ones(n_freq, F32), jnp.ones(n_freq, F32)]), 2)
        cos = jnp.ones((DEC_SEQ, HEAD_PAD), F32).at[:, lane0:lane0 + rot_dim].set(jnp.cos(ang))
        sin = jnp.zeros((DEC_SEQ, HEAD_PAD), F32).at[:, lane0:lane0 + rot_dim].set(jnp.sin(ang) * sign)
        return cos, sin

    assert rows * GRID_W == DEC_SEQ
    return table(MLA_ROPE, MLA_NOPE) + table(GQA_DIM, 0)


def _pad_heads(w, n_heads, dim):
    lead = w.shape[:-1]
    w = w.reshape(lead + (n_heads, dim))
    w = jnp.pad(w, [(0, 0)] * len(lead) + [(0, 0), (0, HEAD_PAD - dim)])
    return w.reshape(lead + (n_heads * HEAD_PAD,))


def _attn_weights(w_in, g_qa, g_kva, w_uq, w_ukv, g_mq, g_mk, g_gq, g_gk, w_out):
    o1, o2, o3 = Q_LORA, Q_LORA + KV_LORA, Q_LORA + KV_LORA + MLA_ROPE
    o4 = o3 + GQA_HEADS * GQA_DIM
    o5 = o4 + GQA_KV_HEADS * GQA_DIM
    kpe = jnp.pad(w_in[:, o2:o3], ((0, 0), (MLA_NOPE, HEAD_PAD - MLA_QK)))
    w_in_ext = jnp.concatenate([w_in[:, :o2], kpe, _pad_heads(w_in[:, o3:o4], GQA_HEADS, GQA_DIM),
                                _pad_heads(w_in[:, o4:o5], GQA_KV_HEADS, GQA_DIM), w_in[:, o5:]], axis=1)
    ukv = w_ukv.reshape(KV_LORA, MLA_HEADS, MLA_NOPE + MLA_V)
    w_ukv_perm = jnp.concatenate([_pad_heads(ukv[:, :, :MLA_NOPE].reshape(KV_LORA, -1), MLA_HEADS, MLA_NOPE),
                                  ukv[:, :, MLA_NOPE:].reshape(KV_LORA, -1)], axis=1)
    pad_g = lambda g: jnp.pad(g, (0, HEAD_PAD - g.shape[0])).reshape(1, HEAD_PAD)
    return dict(w_in=w_in_ext.astype(BF16), g_qa=g_qa.reshape(1, -1), g_kva=g_kva.reshape(1, -1),
                w_uq=_pad_heads(w_uq, MLA_HEADS, MLA_QK).astype(BF16), w_ukv=w_ukv_perm.astype(BF16),
                g_mq=pad_g(g_mq), g_mk=pad_g(g_mk), g_gq=pad_g(g_gq), g_gk=pad_g(g_gk),
                w_out=w_out.astype(BF16))


def _attn_layer(x, mod_l, g1, aw, tabs, cache):
    q, k, v, ckv, kpe, gk, gv = _attn_pre(x, mod_l, g1, aw, tabs)
    c_ckv, c_krope, c_gk, c_gv = cache
    n_c = DEC_BATCH * PAST_LEN
    kpe_c = jnp.pad(c_krope.reshape(n_c, MLA_ROPE), ((0, 0), (MLA_NOPE, HEAD_PAD - MLA_QK)))
    km_c, vm_c = _cache_kv(c_ckv.reshape(n_c, KV_LORA), kpe_c, aw)
    k_c = jnp.concatenate([km_c, _pad_heads(c_gk.reshape(n_c, -1), GQA_KV_HEADS, GQA_DIM).astype(BF16)], axis=1)
    v_c = jnp.concatenate([vm_c, c_gv.reshape(n_c, -1).astype(BF16)], axis=1)
    x = _attention(q, [(k, v, SEQ, 0)], x, mod_l, aw["w_out"], SEQ, BATCH, SEQ, 0, lambda b: 0)
    x = _attention(q, [(k_c, v_c, PAST_LEN, 0), (k, v, DEC_SEQ, N_PROMPT // DEC_SEQ)], x, mod_l, aw["w_out"],
                   TQ_SAMPLE, DEC_BATCH, DEC_SEQ, N_PROMPT, lambda b: 1 + b)
    new = (ckv[:N_PROMPT].reshape(BATCH, SEQ, KV_LORA),
           kpe[:N_PROMPT, MLA_NOPE:MLA_QK].reshape(BATCH, SEQ, MLA_ROPE),
           gk[:N_PROMPT].reshape(BATCH, SEQ, GQA_KV_HEADS, HEAD_PAD)[..., :GQA_DIM],
           gv[:N_PROMPT].reshape(BATCH, SEQ, GQA_KV_HEADS, GQA_DIM))
    return x, new


def _ssm_layer(x, mod_l, g1, a_re, a_im, log_dt, b_re, b_im, c_re, c_im, d_skip, w_glu, b_glu, state0):
    G, P, H = SSM_GROUPS, SSM_STATE, SSM_GROUP
    a = jnp.stack([a_re, a_im], axis=0).transpose(0, 2, 1, 3).reshape(2, G * 2, P)
    dt = jnp.broadcast_to(log_dt.transpose(1, 0)[:, :, None], (G, 2, P)).reshape(G * 2, P)
    bt = jnp.stack([b_re, b_im], axis=2).transpose(1, 0, 2, 4, 3)
    c = jnp.stack([c_re, c_im], axis=2).transpose(1, 0, 2, 3, 4)
    ops, toep, wct, lam = _ssm_ops(a, dt, bt, c)

    u = _ssm_in(x, mod_l, g1)
    perm = _step_major_perm()
    sf, sb = _ssm_ends(u, ops, perm)
    h0 = state0.transpose(2, 1, 0, 4, 3).reshape(G * 2 * DEC_BATCH, STATE_W)
    flat = lambda s: s.reshape(G * N_CHUNK_ROWS, STATE_W)
    xf, xb, fin = _ssm_carry(flat(sf), flat(sb), lam.reshape(G * 4, STATE_W), h0)
    y = _ssm_out(u, toep, xf.reshape(sf.shape), xb.reshape(sb.shape), wct, perm)
    x = _ssm_post(x, mod_l, g1, y, d_skip, w_glu.astype(BF16), b_glu)
    fin = fin.reshape(G, 2, BATCH, 2, P).transpose(2, 1, 0, 4, 3)
    return x, fin


def kernel(x_prompt, x_sample, c, cache_mla_ckv, cache_mla_krope, cache_gqa_k, cache_gqa_v, state_ssm, c_ctx,
           norm1_g, norm2_g, w_mod, b_mod,
           attn_w_in, attn_qa_norm_g, attn_kva_norm_g, attn_w_uq, attn_w_ukv,
           attn_mla_q_norm_g, attn_mla_k_norm_g, attn_gqa_q_norm_g, attn_gqa_k_norm_g, attn_w_out,
           ssm_a_re, ssm_a_im, ssm_log_dt, ssm_b_re, ssm_b_im, ssm_c_re, ssm_c_im, ssm_d, ssm_w_glu, ssm_b_glu,
           ffn_w_up, ffn_conv_w, ffn_conv_b, ffn_w_down):
    x = jnp.concatenate([x_prompt.reshape(N_PROMPT, D_MODEL), x_sample.reshape(N_SAMPLE, D_MODEL)], axis=0)
    cond8 = jnp.concatenate([c_ctx[None, :], c, jnp.zeros((8 - 1 - DEC_BATCH, D_MODEL), F32)], axis=0)
    mod = _modulation(cond8, w_mod, b_mod).reshape(DEPTH, 8, 6, D_MODEL)
    tabs = _rope_tables()

    w_up_bf, w_down_bf = ffn_w_up.astype(BF16), ffn_w_down.astype(BF16)
    new_attn, new_ssm = [], []
    for i in range(DEPTH):
        j = i // 2
        if i % 2 == 0:
            aw = _attn_weights(attn_w_in[j], attn_qa_norm_g[j], attn_kva_norm_g[j], attn_w_uq[j], attn_w_ukv[j],
                               attn_mla_q_norm_g[j], attn_mla_k_norm_g[j], attn_gqa_q_norm_g[j],
                               attn_gqa_k_norm_g[j], attn_w_out[j])
            cache = (cache_mla_ckv[:, j], cache_mla_krope[:, j], cache_gqa_k[:, j], cache_gqa_v[:, j])
            x, new = _attn_layer(x, mod[i], norm1_g[i], aw, tabs, cache)
            new_attn.append(new)
        else:
            x, fin = _ssm_layer(x, mod[i], norm1_g[i], ssm_a_re[j], ssm_a_im[j], ssm_log_dt[j], ssm_b_re[j],
                                ssm_b_im[j], ssm_c_re[j], ssm_c_im[j], ssm_d[j], ssm_w_glu[j], ssm_b_glu[j],
                                state_ssm[:, j])
            new_ssm.append(fin)
        ffn = functools.partial(_conv_ffn, x, mod[i], norm2_g[i], i, w_up_bf, ffn_conv_w, ffn_conv_b, w_down_bf)
        if i < DEPTH - 1:
            x = ffn()
        else:
            y_prompt, y_sample = ffn(0, N_PROMPT), ffn(N_PROMPT, N_SAMPLE)

    outs = [jnp.stack([n[k] for n in new_attn], axis=1) for k in range(4)]
    return (y_prompt.reshape(BATCH, SEQ, D_MODEL), y_sample.reshape(DEC_BATCH, DEC_SEQ, D_MODEL),
            outs[0], outs[1], outs[2], outs[3], jnp.stack(new_ssm, axis=1))
```

```python
import functools
import math

import jax
import jax.numpy as jnp
import numpy as np
from jax import lax
from jax.experimental import pallas as pl
from jax.experimental.pallas import tpu as pltpu

F32 = jnp.float32
BF16 = jnp.bfloat16
HIGHEST = lax.Precision.HIGHEST

LANES = 128
BF16_SUBLANES = 16
VMEM_LIMIT_BYTES = 56 * 1024 * 1024

D_MODEL = 1024
BATCH = 16
SEQ = 256
DEPTH = 4
DEC_BATCH = 2
DEC_SEQ = 2048
PAST_LEN = 256
GRID_W = 64
MLA_HEADS = 8
Q_LORA = 384
KV_LORA = 256
MLA_NOPE = 64
MLA_ROPE = 32
MLA_QK = MLA_NOPE + MLA_ROPE
MLA_V = 64
GQA_HEADS = 8
GQA_KV_HEADS = 2
GQA_DIM = 64
GQA_REP = GQA_HEADS // GQA_KV_HEADS
ROPE_THETA = 10000.0
SSM_GROUP = 16
SSM_GROUPS = D_MODEL // SSM_GROUP
SSM_STATE = 64
D_FF = 2816
EPS = 1e-6

N_PROMPT = BATCH * SEQ
N_SAMPLE = DEC_BATCH * DEC_SEQ
N_TOK = N_PROMPT + N_SAMPLE
N_HEADS = MLA_HEADS + GQA_HEADS
HEAD_PAD = LANES
KV_LEN_SAMPLE = PAST_LEN + DEC_SEQ

IN_Q = 0
IN_CKV = IN_Q + Q_LORA
IN_KPE = IN_CKV + KV_LORA
IN_GQ = IN_KPE + HEAD_PAD
IN_GK = IN_GQ + GQA_HEADS * HEAD_PAD
IN_GV = IN_GK + GQA_KV_HEADS * HEAD_PAD
IN_EXT = IN_GV + GQA_KV_HEADS * GQA_DIM

K_ALL = (MLA_HEADS + GQA_KV_HEADS) * HEAD_PAD
V_ALL = MLA_HEADS * MLA_V + GQA_KV_HEADS * GQA_DIM

SSM_CHUNK = 16
CHUNK_W = SSM_CHUNK * SSM_GROUP
N_CHUNK_ROWS = N_TOK // SSM_CHUNK
N_CHUNK_PROMPT = N_PROMPT // SSM_CHUNK
STATE_W = 2 * SSM_STATE

TM_FFN = 512
TN_FFN = 256
FFN_HALO = BF16_SUBLANES
TM_PRE = 512
TQ_SAMPLE = 256
TM_SSM = 1024
TM_SSM_IN = 512
OUT_ROW_BLOCK = 32
GB_CARRY = 4
GB_CARRY_BLOCK = 16
TN_MOD = 1536
GB_SSM = 8


def _cparams(sem):
    return pltpu.CompilerParams(dimension_semantics=sem, vmem_limit_bytes=VMEM_LIMIT_BYTES)


def _rms(x, g):
    return x * lax.rsqrt(jnp.mean(x * x, axis=-1, keepdims=True) + EPS) * g


def _mod_row(tm):
    n_prompt_tiles = N_PROMPT // tm
    tiles_per_seq = DEC_SEQ // tm

    def row(i):
        return jnp.where(i < n_prompt_tiles, 0, 1 + (i - n_prompt_tiles) // tiles_per_seq)

    return row


def _mod_kernel(cond_ref, w_ref, b_ref, o_ref):
    a = jax.nn.silu(cond_ref[...])
    w = w_ref[0]
    a_hi, w_hi = a.astype(BF16), w.astype(BF16)
    a_lo = (a - a_hi.astype(F32)).astype(BF16)
    w_lo = (w - w_hi.astype(F32)).astype(BF16)
    dot = functools.partial(jnp.dot, preferred_element_type=F32)
    o_ref[0] = dot(a_hi, w_hi) + dot(a_lo, w_hi) + dot(a_hi, w_lo) + b_ref[0]


def _modulation(cond8, w_mod, b_mod):
    n = w_mod.shape[-1]
    return pl.pallas_call(
        _mod_kernel,
        out_shape=jax.ShapeDtypeStruct((DEPTH, 8, n), F32),
        grid=(DEPTH, n // TN_MOD),
        in_specs=[
            pl.BlockSpec((8, D_MODEL), lambda l, j: (0, 0)),
            pl.BlockSpec((1, D_MODEL, TN_MOD), lambda l, j: (l, 0, j)),
            pl.BlockSpec((1, 1, TN_MOD), lambda l, j: (l, 0, j)),
        ],
        out_specs=pl.BlockSpec((1, 8, TN_MOD), lambda l, j: (l, 0, j)),
        compiler_params=_cparams(("arbitrary", "arbitrary")),
        name="modulation",
    )(cond8, w_mod, b_mod.reshape(DEPTH, 1, n))


def _ffn_kernel(xp_ref, x_ref, xn_ref, mod_ref, g_ref, wup_ref, cw_ref, cb_ref, wd_ref, o_ref,
                h_scr, act_scr, z_scr, *, tile0):
    i = pl.program_id(0) + tile0
    tm = x_ref.shape[0]
    tn = TN_FFN
    m = mod_ref[0]
    sh, sc, gt = m[3:4], m[4:5], m[5:6]
    g = g_ref[...]

    def hmod(x):
        return (_rms(x, g) * (1.0 + sc) + sh).astype(BF16)

    h_scr[0:FFN_HALO] = hmod(xp_ref[...])
    h_scr[FFN_HALO:FFN_HALO + tm] = hmod(x_ref[...])
    h_scr[FFN_HALO + tm:] = hmod(xn_ref[...])
    seq_len = jnp.where(i * tm < N_PROMPT, SEQ, DEC_SEQ)
    pos = (i * tm + lax.broadcasted_iota(jnp.int32, (tm, 1), 0)) & (seq_len - 1)
    eg = BF16_SUBLANES
    bounds = range(0, tm + 1, SEQ)
    cuts = sorted({0, tm} | {b + o for b in bounds for o in (-eg, eg) if 0 < b + o < tm})
    segments = [(r0, r1, any(r0 <= b < r1 or r0 < b <= r1 for b in bounds)) for r0, r1 in zip(cuts[:-1], cuts[1:])]

    def conv(slot, k, col, r0, r1, masked):
        cw = cw_ref[:, col:col + tn]
        zp = z_scr[slot, k, FFN_HALO - 1 + r0:FFN_HALO - 1 + r1, :]
        zc = z_scr[slot, k, FFN_HALO + r0:FFN_HALO + r1, :]
        zn = z_scr[slot, k, FFN_HALO + 1 + r0:FFN_HALO + 1 + r1, :]
        if masked:
            zp = jnp.where(pos[r0:r1] == 0, 0.0, zp)
            zn = jnp.where(pos[r0:r1] == seq_len - 1, 0.0, zn)
        return zp * cw[0:1] + zc * cw[1:2] + zn * cw[2:3] + cb_ref[:, col:col + tn]

    h = h_scr[...]
    for jc in range(D_FF // tn):
        ca, cb = jc * tn, D_FF + jc * tn
        slot = jc % 2
        z_scr[slot, 0] = jnp.dot(h, wup_ref[:, ca:ca + tn], preferred_element_type=F32)
        z_scr[slot, 1] = jnp.dot(h, wup_ref[:, cb:cb + tn], preferred_element_type=F32)
        for r0, r1, masked in segments:
            act_scr[r0:r1, ca:ca + tn] = (jax.nn.silu(conv(slot, 0, ca, r0, r1, masked))
                                          * conv(slot, 1, cb, r0, r1, masked)).astype(BF16)
    o_ref[...] = x_ref[...] + gt * jnp.dot(act_scr[...], wd_ref[...], preferred_element_type=F32)


def _conv_ffn(x, mod_l, g, layer, w_up, conv_w, conv_b, w_down, tok0=0, n_tok=N_TOK):
    tm, halo = TM_FFN, FFN_HALO
    row = _mod_row(tm)
    n_halo_blocks = N_TOK // halo
    tile0 = tok0 // tm
    resident = lambda a: pl.BlockSpec((None,) + a.shape[1:], lambda i: (layer,) + (0,) * (a.ndim - 1),
                                      pipeline_mode=pl.Buffered(1))
    cb3 = conv_b.reshape(DEPTH, 1, -1)
    return pl.pallas_call(
        functools.partial(_ffn_kernel, tile0=tile0),
        out_shape=jax.ShapeDtypeStruct((n_tok, D_MODEL), F32),
        grid=(n_tok // tm,),
        in_specs=[
            pl.BlockSpec((halo, D_MODEL), lambda i: (jnp.maximum((i + tile0) * (tm // halo) - 1, 0), 0)),
            pl.BlockSpec((tm, D_MODEL), lambda i: (i + tile0, 0)),
            pl.BlockSpec((halo, D_MODEL),
                         lambda i: (jnp.minimum((i + tile0 + 1) * (tm // halo), n_halo_blocks - 1), 0)),
            pl.BlockSpec((1, 6, D_MODEL), lambda i: (row(i + tile0), 0, 0)),
            pl.BlockSpec((1, D_MODEL), lambda i: (0, 0)),
            resident(w_up), resident(conv_w), resident(cb3), resident(w_down),
        ],
        out_specs=pl.BlockSpec((tm, D_MODEL), lambda i: (i, 0)),
        scratch_shapes=[pltpu.VMEM((tm + 2 * halo, D_MODEL), BF16), pltpu.VMEM((tm, D_FF), BF16),
                        pltpu.VMEM((2, 2, tm + 2 * halo, TN_FFN), F32)],
        compiler_params=_cparams(("arbitrary",)),
        name="conv_ffn",
    )(x, x, x, mod_l, g.reshape(1, D_MODEL), w_up, conv_w, cb3, w_down)


def _rope(x, cos, sin, half):
    lane = lax.broadcasted_iota(jnp.int32, x.shape, 1)
    first = ((lane // half) & 1) == 0
    partner = jnp.where(first, pltpu.roll(x, LANES - half, 1), pltpu.roll(x, half, 1))
    return x * cos + partner * sin


def _head_norm(xh, g, dim):
    sq = xh * xh
    hi = sq.astype(BF16)
    lo = (sq - hi.astype(F32)).astype(BF16)
    ones = jnp.ones((2 * HEAD_PAD, HEAD_PAD), BF16)
    total = jnp.dot(jnp.concatenate([hi, lo], axis=1), ones, preferred_element_type=F32)
    return xh * lax.rsqrt(total * (1.0 / dim) + EPS) * g


def _mla_kv_heads(ckv_bf, kpe, w_ukv_ref, g_mk, cos_m, sin_m, k_ref, v_ref):
    kv = jnp.dot(ckv_bf, w_ukv_ref[...], preferred_element_type=F32)
    for h in range(MLA_HEADS):
        kh = kv[:, h * HEAD_PAD:(h + 1) * HEAD_PAD] + kpe
        kh = _head_norm(kh, g_mk, MLA_QK)
        if cos_m is not None:
            kh = _rope(kh, cos_m, sin_m, MLA_ROPE // 4)
        k_ref[:, h * HEAD_PAD:(h + 1) * HEAD_PAD] = kh.astype(BF16)
    v_ref[:, 0:MLA_HEADS * MLA_V] = kv[:, MLA_HEADS * HEAD_PAD:].astype(BF16)


def _attn_pre_kernel(x_ref, mod_ref, g1_ref, w_in_ref, g_qa_ref, g_kva_ref, w_uq_ref, w_ukv_ref,
                     g_mq_ref, g_mk_ref, g_gq_ref, g_gk_ref, cos_m_ref, sin_m_ref, cos_g_ref, sin_g_ref,
                     q_ref, k_ref, v_ref, ckv_ref, kpe_ref, gk_ref, gv_ref):
    m = mod_ref[0]
    sh, sc = m[0:1], m[1:2]
    h = (_rms(x_ref[...], g1_ref[...]) * (1.0 + sc) + sh).astype(BF16)
    p = jnp.dot(h, w_in_ref[...], preferred_element_type=F32)
    q_c = _rms(p[:, IN_Q:IN_CKV], g_qa_ref[...]).astype(BF16)
    ckv = _rms(p[:, IN_CKV:IN_KPE], g_kva_ref[...])
    kpe = p[:, IN_KPE:IN_GQ]
    ckv_ref[...] = ckv
    kpe_ref[...] = kpe
    gv = p[:, IN_GV:IN_EXT]
    gv_ref[...] = gv
    v_ref[:, MLA_HEADS * MLA_V:] = gv.astype(BF16)
    qm = jnp.dot(q_c, w_uq_ref[...], preferred_element_type=F32)

    def heads(rope):
        cos_m, sin_m = (cos_m_ref[...], sin_m_ref[...]) if rope else (None, None)
        cos_g, sin_g = (cos_g_ref[...], sin_g_ref[...]) if rope else (None, None)
        g_mq = g_mq_ref[...]
        for hd in range(MLA_HEADS):
            qh = _head_norm(qm[:, hd * HEAD_PAD:(hd + 1) * HEAD_PAD], g_mq, MLA_QK)
            if rope:
                qh = _rope(qh, cos_m, sin_m, MLA_ROPE // 4)
            q_ref[:, hd * HEAD_PAD:(hd + 1) * HEAD_PAD] = (qh * (1.0 / math.sqrt(MLA_QK))).astype(BF16)
        _mla_kv_heads(ckv.astype(BF16), kpe, w_ukv_ref, g_mk_ref[...], cos_m, sin_m, k_ref, v_ref)
        g_gq = g_gq_ref[...]
        for hd in range(GQA_HEADS):
            qh = _head_norm(p[:, IN_GQ + hd * HEAD_PAD:IN_GQ + (hd + 1) * HEAD_PAD], g_gq, GQA_DIM)
            if rope:
                qh = _rope(qh, cos_g, sin_g, GQA_DIM // 4)
            q_ref[:, (MLA_HEADS + hd) * HEAD_PAD:(MLA_HEADS + hd + 1) * HEAD_PAD] = (
                qh * (1.0 / math.sqrt(GQA_DIM))).astype(BF16)
        g_gk = g_gk_ref[...]
        for hd in range(GQA_KV_HEADS):
            kh = _head_norm(p[:, IN_GK + hd * HEAD_PAD:IN_GK + (hd + 1) * HEAD_PAD], g_gk, GQA_DIM)
            gk_ref[:, hd * HEAD_PAD:(hd + 1) * HEAD_PAD] = kh
            if rope:
                kh = _rope(kh, cos_g, sin_g, GQA_DIM // 4)
            k_ref[:, (MLA_HEADS + hd) * HEAD_PAD:(MLA_HEADS + hd + 1) * HEAD_PAD] = kh.astype(BF16)

    is_latent = pl.program_id(0) >= N_PROMPT // x_ref.shape[0]
    pl.when(is_latent)(lambda: heads(True))
    pl.when(jnp.logical_not(is_latent))(lambda: heads(False))


def _attn_pre(x, mod_l, g1, aw, tabs):
    tm = TM_PRE
    row = _mod_row(tm)
    full = lambda a: pl.BlockSpec(a.shape, lambda i: (0,) * a.ndim)
    tok = lambda w: pl.BlockSpec((tm, w), lambda i: (i, 0))
    pos_block = lambda i: (jnp.maximum(i - N_PROMPT // tm, 0) % (DEC_SEQ // tm), 0)
    consts = [g1.reshape(1, D_MODEL), aw["w_in"], aw["g_qa"], aw["g_kva"], aw["w_uq"], aw["w_ukv"],
              aw["g_mq"], aw["g_mk"], aw["g_gq"], aw["g_gk"]]
    return pl.pallas_call(
        _attn_pre_kernel,
        out_shape=[
            jax.ShapeDtypeStruct((N_TOK, N_HEADS * HEAD_PAD), BF16),
            jax.ShapeDtypeStruct((N_TOK, K_ALL), BF16),
            jax.ShapeDtypeStruct((N_TOK, V_ALL), BF16),
            jax.ShapeDtypeStruct((N_TOK, KV_LORA), F32),
            jax.ShapeDtypeStruct((N_TOK, HEAD_PAD), F32),
            jax.ShapeDtypeStruct((N_TOK, GQA_KV_HEADS * HEAD_PAD), F32),
            jax.ShapeDtypeStruct((N_TOK, GQA_KV_HEADS * GQA_DIM), F32),
        ],
        grid=(N_TOK // tm,),
        in_specs=[tok(D_MODEL), pl.BlockSpec((1, 6, D_MODEL), lambda i: (row(i), 0, 0))]
        + [full(a) for a in consts] + [pl.BlockSpec((tm, HEAD_PAD), pos_block)] * 4,
        out_specs=[tok(N_HEADS * HEAD_PAD), tok(K_ALL), tok(V_ALL), tok(KV_LORA), tok(HEAD_PAD),
                   tok(GQA_KV_HEADS * HEAD_PAD), tok(GQA_KV_HEADS * GQA_DIM)],
        compiler_params=_cparams(("arbitrary",)),
        name="attn_pre",
    )(x, mod_l, *consts, *tabs)


def _cache_kv_kernel(ckv_ref, kpe_ref, w_ukv_ref, g_mk_ref, k_ref, v_ref):
    _mla_kv_heads(ckv_ref[...].astype(BF16), kpe_ref[...], w_ukv_ref, g_mk_ref[...], None, None, k_ref, v_ref)


def _cache_kv(ckv, kpe_pad, aw):
    n = ckv.shape[0]
    return pl.pallas_call(
        _cache_kv_kernel,
        out_shape=[jax.ShapeDtypeStruct((n, MLA_HEADS * HEAD_PAD), BF16),
                   jax.ShapeDtypeStruct((n, MLA_HEADS * MLA_V), BF16)],
        name="cache_kv",
    )(ckv, kpe_pad, aw["w_ukv"], aw["g_mk"])


def _attn_kernel(*refs, n_seg):
    q_ref, kv_refs = refs[0], refs[1:1 + 2 * n_seg]
    x_ref, mod_ref, wo_ref, o_ref, oh_scr = refs[1 + 2 * n_seg:]
    for hd in range(N_HEADS):
        if hd < MLA_HEADS:
            kc, vc = hd * HEAD_PAD, hd * MLA_V
        else:
            kvh = (hd - MLA_HEADS) // GQA_REP
            kc, vc = (MLA_HEADS + kvh) * HEAD_PAD, MLA_HEADS * MLA_V + kvh * GQA_DIM
        q = q_ref[:, hd * HEAD_PAD:(hd + 1) * HEAD_PAD]
        ss = [lax.dot_general(q, kv_refs[2 * j][:, kc:kc + HEAD_PAD], (((1,), (1,)), ((), ())),
                              preferred_element_type=F32) for j in range(n_seg)]
        m = functools.reduce(jnp.maximum, [jnp.max(s, axis=-1, keepdims=True) for s in ss])
        ps = [jnp.exp(s - m) for s in ss]
        den = sum(jnp.sum(p, axis=-1, keepdims=True) for p in ps)
        o = sum(jnp.dot(p.astype(BF16), kv_refs[2 * j + 1][:, vc:vc + MLA_V], preferred_element_type=F32)
                for j, p in enumerate(ps))
        oh_scr[:, hd * MLA_V:(hd + 1) * MLA_V] = (o / den).astype(BF16)
    gt = mod_ref[0][2:3]
    o_ref[...] = x_ref[...] + gt * jnp.dot(oh_scr[...], wo_ref[...], preferred_element_type=F32)


def _attention(q, segs, x, mod_l, w_out, tq, n_batch, t_len, tok0, mod_row):
    tiles = t_len // tq
    qmap = lambda bi, qi: (tok0 // tq + bi * tiles + qi, 0)
    kv_specs, kv_args = [], []
    for k, v, rows, blk0 in segs:
        kv_specs += [pl.BlockSpec((rows, K_ALL), lambda bi, qi, blk0=blk0: (blk0 + bi, 0)),
                     pl.BlockSpec((rows, V_ALL), lambda bi, qi, blk0=blk0: (blk0 + bi, 0))]
        kv_args += [k, v]
    return pl.pallas_call(
        functools.partial(_attn_kernel, n_seg=len(segs)),
        out_shape=jax.ShapeDtypeStruct((N_TOK, D_MODEL), F32),
        grid=(n_batch, tiles),
        in_specs=[pl.BlockSpec((tq, N_HEADS * HEAD_PAD), qmap)] + kv_specs + [
            pl.BlockSpec((tq, D_MODEL), qmap),
            pl.BlockSpec((1, 6, D_MODEL), lambda bi, qi: (mod_row(bi), 0, 0)),
            pl.BlockSpec((D_MODEL, D_MODEL), lambda bi, qi: (0, 0)),
        ],
        out_specs=pl.BlockSpec((tq, D_MODEL), qmap),
        scratch_shapes=[pltpu.VMEM((tq, D_MODEL), BF16)],
        input_output_aliases={1 + len(kv_args): 0},
        compiler_params=_cparams(("arbitrary", "arbitrary")),
        name="attention",
    )(q, *kv_args, x, mod_l, w_out)


def _cmul(ar, ai, br, bi):
    return ar * br - ai * bi, ar * bi + ai * br


def _ssm_ops_kernel(a_ref, dt_ref, bt_ref, c_ref, ops_ref, toep_ref, wct_ref, lam_ref,
                    cl_re_scr, cl_im_scr, op_scr, tp_scr, wc_scr):
    L, H, P = SSM_CHUNK, SSM_GROUP, SSM_STATE
    lane = lax.broadcasted_iota(jnp.int32, (H, CHUNK_W), 1)

    a_re, a_im = a_ref[0], a_ref[1]
    dt_all = jnp.exp(dt_ref[...])
    mag = jnp.exp(a_re * dt_all)
    ang = a_im * dt_all
    ab_re, ab_im = mag * jnp.cos(ang), mag * jnp.sin(ang)
    den = a_re * a_re + a_im * a_im
    n_re, n_im = ab_re - 1.0, ab_im
    k_re_all = (n_re * a_re + n_im * a_im) / den
    k_im_all = (n_im * a_re - n_re * a_im) / den
    pw_all = [(jnp.ones_like(ab_re), jnp.zeros_like(ab_im))]
    for _ in range(L):
        pw_all.append(_cmul(pw_all[-1][0], pw_all[-1][1], ab_re, ab_im))

    for g in range(ops_ref.shape[0]):
        lag = []
        for d in range(2):
            r = 2 * g + d
            bt_re, bt_im = bt_ref[g, d, 0], bt_ref[g, d, 1]
            bb_re, bb_im = _cmul(k_re_all[r:r + 1], k_im_all[r:r + 1], bt_re, bt_im)
            c_re, c_im = c_ref[g, d, 0], c_ref[g, d, 1]
            pw = [(p_re[r:r + 1], p_im[r:r + 1]) for p_re, p_im in pw_all]

            for e in range(L + 1):
                pr, pi = pw[e] if d == 0 else pw[L - e]
                cr, ci = _cmul(c_re, c_im, pr, pi)
                cl_re_scr[r,e * H:(e + 1) * H, :] = cr
                cl_im_scr[r,e * H:(e + 1) * H, :] = ci
            lo = 0 if d == 0 else H
            lag.append(lax.dot_general(bb_re, cl_re_scr[r,lo:lo + CHUNK_W, :], (((1,), (1,)), ((), ())),
                                       preferred_element_type=F32, precision=HIGHEST)
                       - lax.dot_general(bb_im, cl_im_scr[r,lo:lo + CHUNK_W, :], (((1,), (1,)), ((), ())),
                                         preferred_element_type=F32, precision=HIGHEST))
            wlo = H if d == 0 else 0
            wc_scr[g, :, d * STATE_W:d * STATE_W + P] = cl_re_scr[r,wlo:wlo + CHUNK_W, :]
            wc_scr[g, :, d * STATE_W + P:(d + 1) * STATE_W] = -cl_im_scr[r,wlo:wlo + CHUNK_W, :]
            for j in range(L):
                pj = pw[L - 1 - j] if d == 0 else pw[j]
                sr, si = _cmul(pj[0], pj[1], bb_re, bb_im)
                op_scr[g, j * H:(j + 1) * H, d * STATE_W:d * STATE_W + P] = sr
                op_scr[g, j * H:(j + 1) * H, d * STATE_W + P:(d + 1) * STATE_W] = si
            pr, pi = pw[L]
            lam_ref[g, 2 * d:2 * d + 1, 0:P] = pr
            lam_ref[g, 2 * d:2 * d + 1, P:STATE_W] = pr
            lam_ref[g, 2 * d + 1:2 * d + 2, 0:P] = -pi
            lam_ref[g, 2 * d + 1:2 * d + 2, P:STATE_W] = pi
        for j in range(L):
            fwd = lag[0] if j == 0 else jnp.where(lane >= H * j, pltpu.roll(lag[0], H * j, 1), 0.0)
            sft = (L - 1 - j) * H
            bwd = lag[1] if sft == 0 else jnp.where(lane < H * (j + 1), pltpu.roll(lag[1], CHUNK_W - sft, 1), 0.0)
            tp_scr[g, j * H:(j + 1) * H, :] = fwd + bwd
        ops_ref[g] = op_scr[g].astype(BF16)
        toep_ref[g] = tp_scr[g].astype(BF16)
        wct_ref[g] = wc_scr[g].astype(BF16)


def _ssm_ops(a, dt, bt, c):
    gb = GB_SSM
    G = SSM_GROUPS
    blk = lambda shape: pl.BlockSpec((gb,) + shape, lambda i: (i,) + (0,) * len(shape))
    return pl.pallas_call(
        _ssm_ops_kernel,
        out_shape=[jax.ShapeDtypeStruct((G, CHUNK_W, 2 * STATE_W), BF16),
                   jax.ShapeDtypeStruct((G, CHUNK_W, CHUNK_W), BF16),
                   jax.ShapeDtypeStruct((G, CHUNK_W, 2 * STATE_W), BF16),
                   jax.ShapeDtypeStruct((G, 4, STATE_W), F32)],
        grid=(G // gb,),
        in_specs=[pl.BlockSpec((2, 2 * gb, SSM_STATE), lambda i: (0, i, 0)),
                  pl.BlockSpec((2 * gb, SSM_STATE), lambda i: (i, 0)),
                  blk((2, 2, SSM_GROUP, SSM_STATE)), blk((2, 2, SSM_GROUP, SSM_STATE))],
        out_specs=[blk((CHUNK_W, 2 * STATE_W)), blk((CHUNK_W, CHUNK_W)), blk((CHUNK_W, 2 * STATE_W)),
                   blk((4, STATE_W))],
        scratch_shapes=[pltpu.VMEM((2 * gb, (SSM_CHUNK + 1) * SSM_GROUP, SSM_STATE), F32),
                        pltpu.VMEM((2 * gb, (SSM_CHUNK + 1) * SSM_GROUP, SSM_STATE), F32),
                        pltpu.VMEM((gb, CHUNK_W, 2 * STATE_W), F32),
                        pltpu.VMEM((gb, CHUNK_W, CHUNK_W), F32),
                        pltpu.VMEM((gb, CHUNK_W, 2 * STATE_W), F32)],
        compiler_params=_cparams(("arbitrary",)),
        name="ssm_ops",
    )(a, dt, bt, c)


GRANULES = LANES // SSM_GROUP


def _granule_transpose(arrs):
    a = list(arrs)
    lane = lax.broadcasted_iota(jnp.int32, a[0].shape, 1)
    for d in (4, 2, 1):
        upper = (lane & (d * SSM_GROUP)) != 0
        for s in range(GRANULES):
            if s & d:
                continue
            lo, hi = a[s], a[s + d]
            a[s] = jnp.where(upper, pltpu.roll(hi, d * SSM_GROUP, 1), lo)
            a[s + d] = jnp.where(upper, hi, pltpu.roll(lo, LANES - d * SSM_GROUP, 1))
    return a


def _step_select(tm):
    mb = tm // SSM_CHUNK
    idx = (np.arange(mb)[None, :] * SSM_CHUNK + np.arange(SSM_CHUNK)[:, None]).reshape(-1)
    return jnp.asarray(np.eye(tm, dtype=np.float32)[idx], BF16)


def _ssm_in_kernel(x_ref, mod_ref, g_ref, sel_ref, u_ref):
    m = mod_ref[0]
    h = (_rms(x_ref[...], g_ref[...]) * (1.0 + m[1:2]) + m[0:1]).astype(BF16)
    by_step = jnp.dot(sel_ref[...], h, preferred_element_type=F32)
    mb = x_ref.shape[0] // SSM_CHUNK
    for g8 in range(SSM_GROUPS // GRANULES):
        for half in range(SSM_CHUNK // GRANULES):
            a = [by_step[(GRANULES * half + s) * mb:(GRANULES * half + s + 1) * mb, g8 * LANES:(g8 + 1) * LANES]
                 for s in range(GRANULES)]
            b = _granule_transpose(a)
            for g in range(GRANULES):
                u_ref[g8 * GRANULES + g, :, half * LANES:(half + 1) * LANES] = b[g].astype(BF16)


def _ssm_in(x, mod_l, g1):
    tm = TM_SSM_IN
    row = _mod_row(tm)
    return pl.pallas_call(
        _ssm_in_kernel,
        out_shape=jax.ShapeDtypeStruct((SSM_GROUPS, N_CHUNK_ROWS, CHUNK_W), BF16),
        grid=(N_TOK // tm,),
        in_specs=[pl.BlockSpec((tm, D_MODEL), lambda i: (i, 0)),
                  pl.BlockSpec((1, 6, D_MODEL), lambda i: (row(i), 0, 0)),
                  pl.BlockSpec((1, D_MODEL), lambda i: (0, 0)),
                  pl.BlockSpec((tm, tm), lambda i: (0, 0))],
        out_specs=pl.BlockSpec((SSM_GROUPS, tm // SSM_CHUNK, CHUNK_W), lambda i: (0, i, 0)),
        compiler_params=_cparams(("arbitrary",)),
        name="ssm_in",
    )(x, mod_l, g1.reshape(1, D_MODEL), _step_select(tm))


def _step_major_perm():
    n = SEQ // SSM_CHUNK
    assert n == BATCH
    idx = np.arange(N_CHUNK_PROMPT).reshape(BATCH, n).T.reshape(-1)
    return jnp.asarray(np.eye(N_CHUNK_PROMPT, dtype=np.float32)[idx], BF16)


def _ssm_ends_kernel(u_ref, ops_ref, perm_ref, *s_refs):
    n_p = N_CHUNK_PROMPT
    for g in range(u_ref.shape[0]):
        u_ctx = jnp.dot(perm_ref[...], u_ref[g, 0:n_p], preferred_element_type=F32).astype(BF16)
        s_ctx = jnp.dot(u_ctx, ops_ref[g], preferred_element_type=F32)
        s_lat = jnp.dot(u_ref[g, n_p:], ops_ref[g], preferred_element_type=F32)
        for k, s_ref in enumerate(s_refs):
            s_ref[g, 0:n_p] = s_ctx[:, k * STATE_W:(k + 1) * STATE_W]
            s_ref[g, n_p:] = s_lat[:, k * STATE_W:(k + 1) * STATE_W]


def _ssm_ends(u, ops, perm):
    gb = GB_SSM
    G = SSM_GROUPS
    return pl.pallas_call(
        _ssm_ends_kernel,
        out_shape=[jax.ShapeDtypeStruct((G, N_CHUNK_ROWS, STATE_W), F32)] * 2,
        grid=(G // gb,),
        in_specs=[pl.BlockSpec((gb, N_CHUNK_ROWS, CHUNK_W), lambda i: (i, 0, 0)),
                  pl.BlockSpec((gb, CHUNK_W, 2 * STATE_W), lambda i: (i, 0, 0)),
                  pl.BlockSpec(perm.shape, lambda i: (0, 0))],
        out_specs=[pl.BlockSpec((gb, N_CHUNK_ROWS, STATE_W), lambda i: (i, 0, 0))] * 2,
        compiler_params=_cparams(("arbitrary",)),
        name="ssm_chunk_ends",
    )(u, ops, perm)


def _ssm_carry_kernel(sf_ref, sb_ref, lam_ref, h0_ref, xf_ref, xb_ref, fin_ref):
    n_g = sf_ref.shape[0] // N_CHUNK_ROWS
    n_p, n_s = SEQ // SSM_CHUNK, DEC_SEQ // SSM_CHUNK

    def swap(t):
        return pltpu.roll(t, SSM_STATE, 1)

    def step(x, lam, rf, rb):
        xf, xfs, xb, xbs = x
        a1f, a2f, a1b, a2b = lam
        xf_ref[rf, :] = xf
        xb_ref[rb, :] = xb
        sf, sb = sf_ref[rf, :], sb_ref[rb, :]
        return [a1f * xf + a2f * xfs + sf, a1f * xfs - a2f * xf + swap(sf),
                a1b * xb + a2b * xbs + sb, a1b * xbs - a2b * xb + swap(sb)]

    zero = jnp.zeros((BATCH, STATE_W), F32)
    for g0 in range(0, n_g, GB_CARRY):
        groups = range(g0, g0 + GB_CARRY)
        lam = [[jnp.broadcast_to(lam_ref[4 * g + k:4 * g + k + 1, :], (BATCH, STATE_W)) for k in range(4)]
               for g in groups]

        def body_p(c, carry, groups=groups, lam=lam):
            out = []
            for j, g in enumerate(groups):
                rf = pl.ds(pl.multiple_of(g * N_CHUNK_ROWS + c * BATCH, BATCH), BATCH)
                rb = pl.ds(pl.multiple_of(g * N_CHUNK_ROWS + (n_p - 1 - c) * BATCH, BATCH), BATCH)
                out += step(carry[4 * j:4 * j + 4], lam[j], rf, rb)
            return tuple(out)

        fin = lax.fori_loop(0, n_p, body_p, (zero,) * (4 * GB_CARRY), unroll=2)
        for j, g in enumerate(groups):
            fin_ref[g, 0] = fin[4 * j]
            fin_ref[g, 1] = fin[4 * j + 2]

    packs = [(g0, b) for g0 in range(0, n_g, 8) for b in range(DEC_BATCH)]
    lam = {g0: [lam_ref[pl.ds(4 * g0 + k, 8, stride=4), :] for k in range(4)] for g0 in range(0, n_g, 8)}

    def body_s(c, carry):
        out = []
        for j, (g0, b) in enumerate(packs):
            base = g0 * N_CHUNK_ROWS + N_CHUNK_PROMPT + b * n_s
            rf = pl.ds(base + c, 8, stride=N_CHUNK_ROWS)
            rb = pl.ds(base + (n_s - 1 - c), 8, stride=N_CHUNK_ROWS)
            out += step(carry[4 * j:4 * j + 4], lam[g0], rf, rb)
        return tuple(out)

    init = []
    for g0, b in packs:
        h0f = h0_ref[pl.ds(4 * g0 + b, 8, stride=4), :]
        h0b = h0_ref[pl.ds(4 * g0 + DEC_BATCH + b, 8, stride=4), :]
        init += [h0f, swap(h0f), h0b, swap(h0b)]
    lax.fori_loop(0, n_s, body_s, tuple(init), unroll=4)


def _ssm_carry(sf, sb, lam, h0):
    gb = GB_CARRY_BLOCK
    G = SSM_GROUPS
    st = pl.BlockSpec((gb * N_CHUNK_ROWS, STATE_W), lambda i: (i, 0))
    small = pl.BlockSpec((gb * 4, STATE_W), lambda i: (i, 0))
    return pl.pallas_call(
        _ssm_carry_kernel,
        out_shape=[jax.ShapeDtypeStruct(sf.shape, F32)] * 2 + [jax.ShapeDtypeStruct((G, 2, BATCH, STATE_W), F32)],
        grid=(G // gb,),
        in_specs=[st, st, small, small],
        out_specs=[st, st, pl.BlockSpec((gb, 2, BATCH, STATE_W), lambda i: (i, 0, 0, 0))],
        compiler_params=_cparams(("arbitrary",)),
        name="ssm_carry",
    )(sf, sb, lam, h0)


def _ssm_out_kernel(u_ref, toep_ref, xf_ref, xb_ref, wct_ref, perm_ref, y_ref, y_scr):
    n_p = N_CHUNK_PROMPT
    for g in range(GRANULES):
        y = jnp.dot(u_ref[g], toep_ref[g], preferred_element_type=F32)
        for d, x_ref in enumerate((xf_ref, xb_ref)):
            x_ctx = jnp.dot(perm_ref[...], x_ref[g, 0:n_p].astype(BF16), preferred_element_type=F32).astype(BF16)
            xin = jnp.concatenate([x_ctx, x_ref[g, n_p:].astype(BF16)], axis=0)
            w = wct_ref[g, :, d * STATE_W:(d + 1) * STATE_W]
            y += lax.dot_general(xin, w, (((1,), (1,)), ((), ())), preferred_element_type=F32)
        y_scr[g] = y
    rb = OUT_ROW_BLOCK
    for r in range(N_CHUNK_ROWS // rb):
        for half in range(SSM_CHUNK // GRANULES):
            a = [y_scr[g, r * rb:(r + 1) * rb, half * LANES:(half + 1) * LANES] for g in range(GRANULES)]
            b = _granule_transpose(a)
            for s in range(GRANULES):
                y_ref[pl.ds(r * rb * SSM_CHUNK + GRANULES * half + s, rb, stride=SSM_CHUNK), :] = b[s]


def _ssm_out(u, toep, xf, xb, wct, perm):
    gb = GRANULES
    G = SSM_GROUPS
    return pl.pallas_call(
        _ssm_out_kernel,
        out_shape=jax.ShapeDtypeStruct((N_TOK, D_MODEL), F32),
        grid=(G // gb,),
        in_specs=[pl.BlockSpec((gb, N_CHUNK_ROWS, CHUNK_W), lambda i: (i, 0, 0)),
                  pl.BlockSpec((gb, CHUNK_W, CHUNK_W), lambda i: (i, 0, 0)),
                  pl.BlockSpec((gb, N_CHUNK_ROWS, STATE_W), lambda i: (i, 0, 0)),
                  pl.BlockSpec((gb, N_CHUNK_ROWS, STATE_W), lambda i: (i, 0, 0)),
                  pl.BlockSpec((gb, CHUNK_W, 2 * STATE_W), lambda i: (i, 0, 0)),
                  pl.BlockSpec(perm.shape, lambda i: (0, 0))],
        out_specs=pl.BlockSpec((N_TOK, LANES), lambda i: (0, i)),
        scratch_shapes=[pltpu.VMEM((gb, N_CHUNK_ROWS, CHUNK_W), F32)],
        compiler_params=_cparams(("arbitrary",)),
        name="ssm_chunk_out",
    )(u, toep, xf, xb, wct, perm)


def _ssm_post_kernel(x_ref, mod_ref, g_ref, y_ref, d_ref, w_ref, b_ref, o_ref):
    m = mod_ref[0]
    x = x_ref[...]
    h = _rms(x, g_ref[...]) * (1.0 + m[1:2]) + m[0:1]
    y = h * d_ref[...] + y_ref[...]
    gl = jax.nn.gelu(y).astype(BF16)
    z = jnp.dot(gl, w_ref[...], preferred_element_type=F32) + b_ref[...]
    o_ref[...] = x + m[2:3] * (z[:, :D_MODEL] * jax.nn.sigmoid(z[:, D_MODEL:]))


def _ssm_post(x, mod_l, g1, y, d_skip, w_glu, b_glu):
    tm = TM_SSM
    row = _mod_row(tm)
    full = lambda a: pl.BlockSpec(a.shape, lambda i: (0,) * a.ndim)
    d2, b2 = d_skip.reshape(1, D_MODEL), b_glu.reshape(1, 2 * D_MODEL)
    g2 = g1.reshape(1, D_MODEL)
    return pl.pallas_call(
        _ssm_post_kernel,
        out_shape=jax.ShapeDtypeStruct((N_TOK, D_MODEL), F32),
        grid=(N_TOK // tm,),
        in_specs=[pl.BlockSpec((tm, D_MODEL), lambda i: (i, 0)),
                  pl.BlockSpec((1, 6, D_MODEL), lambda i: (row(i), 0, 0)),
                  full(g2), pl.BlockSpec((tm, D_MODEL), lambda i: (i, 0)), full(d2), full(w_glu), full(b2)],
        out_specs=pl.BlockSpec((tm, D_MODEL), lambda i: (i, 0)),
        compiler_params=_cparams(("arbitrary",)),
        name="ssm_post",
    )(x, mod_l, g2, y, d2, w_glu, b2)


def _rope_tables():
    rows = DEC_SEQ // GRID_W
    t = np.arange(DEC_SEQ)
    row, col = (t // GRID_W).astype(np.float32), (t % GRID_W).astype(np.float32)

    def table(rot_dim, lane0):
        n_freq = rot_dim // 4
        inv_freq = jnp.asarray(ROPE_THETA, F32) ** (-jnp.arange(n_freq, dtype=F32) / n_freq)
        ang_row = jnp.asarray(row)[:, None] * inv_freq
        ang_col = jnp.asarray(col)[:, None] * inv_freq
        ang = jnp.concatenate([ang_row, ang_row, ang_col, ang_col], axis=1)
        sign = jnp.tile(jnp.concatenate([-jnp.ones(n_freq, F32), jnp.ones(n_freq, F32)]), 2)
        cos = jnp.ones((DEC_SEQ, HEAD_PAD), F32).at[:, lane0:lane0 + rot_dim].set(jnp.cos(ang))
        sin = jnp.zeros((DEC_SEQ, HEAD_PAD), F32).at[:, lane0:lane0 + rot_dim].set(jnp.sin(ang) * sign)
        return cos, sin

    assert rows * GRID_W == DEC_SEQ
    return table(MLA_ROPE, MLA_NOPE) + table(GQA_DIM, 0)


def _pad_heads(w, n_heads, dim):
    lead = w.shape[:-1]
    w = w.reshape(lead + (n_heads, dim))
    w = jnp.pad(w, [(0, 0)] * len(lead) + [(0, 0), (0, HEAD_PAD - dim)])
    return w.reshape(lead + (n_heads * HEAD_PAD,))


def _attn_weights(w_in, g_qa, g_kva, w_uq, w_ukv, g_mq, g_mk, g_gq, g_gk, w_out):
    o1, o2, o3 = Q_LORA, Q_LORA + KV_LORA, Q_LORA + KV_LORA + MLA_ROPE
    o4 = o3 + GQA_HEADS * GQA_DIM
    o5 = o4 + GQA_KV_HEADS * GQA_DIM
    kpe = jnp.pad(w_in[:, o2:o3], ((0, 0), (MLA_NOPE, HEAD_PAD - MLA_QK)))
    w_in_ext = jnp.concatenate([w_in[:, :o2], kpe, _pad_heads(w_in[:, o3:o4], GQA_HEADS, GQA_DIM),
                                _pad_heads(w_in[:, o4:o5], GQA_KV_HEADS, GQA_DIM), w_in[:, o5:]], axis=1)
    ukv = w_ukv.reshape(KV_LORA, MLA_HEADS, MLA_NOPE + MLA_V)
    w_ukv_perm = jnp.concatenate([_pad_heads(ukv[:, :, :MLA_NOPE].reshape(KV_LORA, -1), MLA_HEADS, MLA_NOPE),
                                  ukv[:, :, MLA_NOPE:].reshape(KV_LORA, -1)], axis=1)
    pad_g = lambda g: jnp.pad(g, (0, HEAD_PAD - g.shape[0])).reshape(1, HEAD_PAD)
    return dict(w_in=w_in_ext.astype(BF16), g_qa=g_qa.reshape(1, -1), g_kva=g_kva.reshape(1, -1),
                w_uq=_pad_heads(w_uq, MLA_HEADS, MLA_QK).astype(BF16), w_ukv=w_ukv_perm.astype(BF16),
                g_mq=pad_g(g_mq), g_mk=pad_g(g_mk), g_gq=pad_g(g_gq), g_gk=pad_g(g_gk),
                w_out=w_out.astype(BF16))


def _attn_layer(x, mod_l, g1, aw, tabs, cache):
    q, k, v, ckv, kpe, gk, gv = _attn_pre(x, mod_l, g1, aw, tabs)
    c_ckv, c_krope, c_gk, c_gv = cache
    n_c = DEC_BATCH * PAST_LEN
    kpe_c = jnp.pad(c_krope.reshape(n_c, MLA_ROPE), ((0, 0), (MLA_NOPE, HEAD_PAD - MLA_QK)))
    km_c, vm_c = _cache_kv(c_ckv.reshape(n_c, KV_LORA), kpe_c, aw)
    k_c = jnp.concatenate([km_c, _pad_heads(c_gk.reshape(n_c, -1), GQA_KV_HEADS, GQA_DIM).astype(BF16)], axis=1)
    v_c = jnp.concatenate([vm_c, c_gv.reshape(n_c, -1).astype(BF16)], axis=1)
    x = _attention(q, [(k, v, SEQ, 0)], x, mod_l, aw["w_out"], SEQ, BATCH, SEQ, 0, lambda b: 0)
    x = _attention(q, [(k_c, v_c, PAST_LEN, 0), (k, v, DEC_SEQ, N_PROMPT // DEC_SEQ)], x, mod_l, aw["w_out"],
                   TQ_SAMPLE, DEC_BATCH, DEC_SEQ, N_PROMPT, lambda b: 1 + b)
    new = (ckv[:N_PROMPT].reshape(BATCH, SEQ, KV_LORA),
           kpe[:N_PROMPT, MLA_NOPE:MLA_QK].reshape(BATCH, SEQ, MLA_ROPE),
           gk[:N_PROMPT].reshape(BATCH, SEQ, GQA_KV_HEADS, HEAD_PAD)[..., :GQA_DIM],
           gv[:N_PROMPT].reshape(BATCH, SEQ, GQA_KV_HEADS, GQA_DIM))
    return x, new


def _ssm_layer(x, mod_l, g1, a_re, a_im, log_dt, b_re, b_im, c_re, c_im, d_skip, w_glu, b_glu, state0):
    G, P, H = SSM_GROUPS, SSM_STATE, SSM_GROUP
    a = jnp.stack([a_re, a_im], axis=0).transpose(0, 2, 1, 3).reshape(2, G * 2, P)
    dt = jnp.broadcast_to(log_dt.transpose(1, 0)[:, :, None], (G, 2, P)).reshape(G * 2, P)
    bt = jnp.stack([b_re, b_im], axis=2).transpose(1, 0, 2, 4, 3)
    c = jnp.stack([c_re, c_im], axis=2).transpose(1, 0, 2, 3, 4)
    ops, toep, wct, lam = _ssm_ops(a, dt, bt, c)

    u = _ssm_in(x, mod_l, g1)
    perm = _step_major_perm()
    sf, sb = _ssm_ends(u, ops, perm)
    h0 = state0.transpose(2, 1, 0, 4, 3).reshape(G * 2 * DEC_BATCH, STATE_W)
    flat = lambda s: s.reshape(G * N_CHUNK_ROWS, STATE_W)
    xf, xb, fin = _ssm_carry(flat(sf), flat(sb), lam.reshape(G * 4, STATE_W), h0)
    y = _ssm_out(u, toep, xf.reshape(sf.shape), xb.reshape(sb.shape), wct, perm)
    x = _ssm_post(x, mod_l, g1, y, d_skip, w_glu.astype(BF16), b_glu)
    fin = fin.reshape(G, 2, BATCH, 2, P).transpose(2, 1, 0, 4, 3)
    return x, fin


def kernel(x_prompt, x_sample, c, cache_mla_ckv, cache_mla_krope, cache_gqa_k, cache_gqa_v, state_ssm, c_ctx,
           norm1_g, norm2_g, w_mod, b_mod,
           attn_w_in, attn_qa_norm_g, attn_kva_norm_g, attn_w_uq, attn_w_ukv,
           attn_mla_q_norm_g, attn_mla_k_norm_g, attn_gqa_q_norm_g, attn_gqa_k_norm_g, attn_w_out,
           ssm_a_re, ssm_a_im, ssm_log_dt, ssm_b_re, ssm_b_im, ssm_c_re, ssm_c_im, ssm_d, ssm_w_glu, ssm_b_glu,
           ffn_w_up, ffn_conv_w, ffn_conv_b, ffn_w_down):
    x = jnp.concatenate([x_prompt.reshape(N_PROMPT, D_MODEL), x_sample.reshape(N_SAMPLE, D_MODEL)], axis=0)
    cond8 = jnp.concatenate([c_ctx[None, :], c, jnp.zeros((8 - 1 - DEC_BATCH, D_MODEL), F32)], axis=0)
    mod = _modulation(cond8, w_mod, b_mod).reshape(DEPTH, 8, 6, D_MODEL)
    tabs = _rope_tables()

    w_up_bf, w_down_bf = ffn_w_up.astype(BF16), ffn_w_down.astype(BF16)
    new_attn, new_ssm = [], []
    for i in range(DEPTH):
        j = i // 2
        if i % 2 == 0:
            aw = _attn_weights(attn_w_in[j], attn_qa_norm_g[j], attn_kva_norm_g[j], attn_w_uq[j], attn_w_ukv[j],
                               attn_mla_q_norm_g[j], attn_mla_k_norm_g[j], attn_gqa_q_norm_g[j],
                               attn_gqa_k_norm_g[j], attn_w_out[j])
            cache = (cache_mla_ckv[:, j], cache_mla_krope[:, j], cache_gqa_k[:, j], cache_gqa_v[:, j])
            x, new = _attn_layer(x, mod[i], norm1_g[i], aw, tabs, cache)
            new_attn.append(new)
        else:
            x, fin = _ssm_layer(x, mod[i], norm1_g[i], ssm_a_re[j], ssm_a_im[j], ssm_log_dt[j], ssm_b_re[j],
                                ssm_b_im[j], ssm_c_re[j], ssm_c_im[j], ssm_d[j], ssm_w_glu[j], ssm_b_glu[j],
                                state_ssm[:, j])
            new_ssm.append(fin)
        ffn = functools.partial(_conv_ffn, x, mod[i], norm2_g[i], i, w_up_bf, ffn_conv_w, ffn_conv_b, w_down_bf)
        if i < DEPTH - 1:
            x = ffn()
        else:
            y_prompt, y_sample = ffn(0, N_PROMPT), ffn(N_PROMPT, N_SAMPLE)

    outs = [jnp.stack([n[k] for n in new_attn], axis=1) for k in range(4)]
    return (y_prompt.reshape(BATCH, SEQ, D_MODEL), y_sample.reshape(DEC_BATCH, DEC_SEQ, D_MODEL),
            outs[0], outs[1], outs[2], outs[3], jnp.stack(new_ssm, axis=1))
```

```python
import functools
import math

import jax
import jax.numpy as jnp
import numpy as np
from jax import lax
from jax.experimental import pallas as pl
from jax.experimental.pallas import tpu as pltpu

F32 = jnp.float32
BF16 = jnp.bfloat16
HIGHEST = lax.Precision.HIGHEST

LANES = 128
BF16_SUBLANES = 16
VMEM_LIMIT_BYTES = 56 * 1024 * 1024

D_MODEL = 1024
BATCH = 16
SEQ = 256
DEPTH = 4
DEC_BATCH = 2
DEC_SEQ = 2048
PAST_LEN = 256
GRID_W = 64
MLA_HEADS = 8
Q_LORA = 384
KV_LORA = 256
MLA_NOPE = 64
MLA_ROPE = 32
MLA_QK = MLA_NOPE + MLA_ROPE
MLA_V = 64
GQA_HEADS = 8
GQA_KV_HEADS = 2
GQA_DIM = 64
GQA_REP = GQA_HEADS // GQA_KV_HEADS
ROPE_THETA = 10000.0
SSM_GROUP = 16
SSM_GROUPS = D_MODEL // SSM_GROUP
SSM_STATE = 64
D_FF = 2816
EPS = 1e-6

N_PROMPT = BATCH * SEQ
N_SAMPLE = DEC_BATCH * DEC_SEQ
N_TOK = N_PROMPT + N_SAMPLE
N_HEADS = MLA_HEADS + GQA_HEADS
HEAD_PAD = LANES

IN_Q = 0
IN_CKV = IN_Q + Q_LORA
IN_KPE = IN_CKV + KV_LORA
IN_GQ = IN_KPE + HEAD_PAD
IN_GK = IN_GQ + GQA_HEADS * HEAD_PAD
IN_GV = IN_GK + GQA_KV_HEADS * HEAD_PAD
IN_EXT = IN_GV + GQA_KV_HEADS * GQA_DIM

K_ALL = (MLA_HEADS + GQA_KV_HEADS) * HEAD_PAD
V_ALL = MLA_HEADS * MLA_V + GQA_KV_HEADS * GQA_DIM

SSM_CHUNK = 16
CHUNK_W = SSM_CHUNK * SSM_GROUP
N_CHUNK_ROWS = N_TOK // SSM_CHUNK
N_CHUNK_PROMPT = N_PROMPT // SSM_CHUNK
STATE_W = 2 * SSM_STATE

TM_FFN = 512
TN_FFN = 256
FFN_HALO = BF16_SUBLANES
TM_PRE = 512
TQ_SAMPLE = 256
TM_SSM = 1024
TM_SSM_IN = 512
OUT_ROW_BLOCK = 32
GB_CARRY = 4
GB_CARRY_BLOCK = 16
TN_MOD = 1536
GB_SSM = 8


def _cparams(sem):
    return pltpu.CompilerParams(dimension_semantics=sem, vmem_limit_bytes=VMEM_LIMIT_BYTES)


def _rms(x, g):
    return x * lax.rsqrt(jnp.mean(x * x, axis=-1, keepdims=True) + EPS) * g


def _mod_row(tm):
    n_prompt_tiles = N_PROMPT // tm
    tiles_per_seq = DEC_SEQ // tm

    def row(i):
        return jnp.where(i < n_prompt_tiles, 0, 1 + (i - n_prompt_tiles) // tiles_per_seq)

    return row


def _mod_kernel(cond_ref, w_ref, b_ref, o_ref):
    a = jax.nn.silu(cond_ref[...])
    w = w_ref[0]
    a_hi, w_hi = a.astype(BF16), w.astype(BF16)
    a_lo = (a - a_hi.astype(F32)).astype(BF16)
    w_lo = (w - w_hi.astype(F32)).astype(BF16)
    dot = functools.partial(jnp.dot, preferred_element_type=F32)
    o_ref[0] = dot(a_hi, w_hi) + dot(a_lo, w_hi) + dot(a_hi, w_lo) + b_ref[0]


def _modulation(cond8, w_mod, b_mod):
    n = w_mod.shape[-1]
    return pl.pallas_call(
        _mod_kernel,
        out_shape=jax.ShapeDtypeStruct((DEPTH, 8, n), F32),
        grid=(DEPTH, n // TN_MOD),
        in_specs=[
            pl.BlockSpec((8, D_MODEL), lambda l, j: (0, 0)),
            pl.BlockSpec((1, D_MODEL, TN_MOD), lambda l, j: (l, 0, j)),
            pl.BlockSpec((1, 1, TN_MOD), lambda l, j: (l, 0, j)),
        ],
        out_specs=pl.BlockSpec((1, 8, TN_MOD), lambda l, j: (l, 0, j)),
        compiler_params=_cparams(("arbitrary", "arbitrary")),
        name="modulation",
    )(cond8, w_mod, b_mod.reshape(DEPTH, 1, n))


def _ffn_kernel(xp_ref, x_ref, xn_ref, mod_ref, g_ref, wup_ref, cw_ref, cb_ref, wd_ref, o_ref,
                h_scr, act_scr, z_scr, *, tile0):
    i = pl.program_id(0) + tile0
    tm = x_ref.shape[0]
    tn = TN_FFN
    m = mod_ref[0]
    sh, sc, gt = m[3:4], m[4:5], m[5:6]
    g = g_ref[...]

    def hmod(x):
        return (_rms(x, g) * (1.0 + sc) + sh).astype(BF16)

    h_scr[0:FFN_HALO] = hmod(xp_ref[...])
    h_scr[FFN_HALO:FFN_HALO + tm] = hmod(x_ref[...])
    h_scr[FFN_HALO + tm:] = hmod(xn_ref[...])
    seq_len = jnp.where(i * tm < N_PROMPT, SEQ, DEC_SEQ)
    pos = (i * tm + lax.broadcasted_iota(jnp.int32, (tm, 1), 0)) & (seq_len - 1)
    eg = BF16_SUBLANES
    bounds = range(0, tm + 1, SEQ)
    cuts = sorted({0, tm} | {b + o for b in bounds for o in (-eg, eg) if 0 < b + o < tm})
    segments = [(r0, r1, any(r0 <= b < r1 or r0 < b <= r1 for b in bounds)) for r0, r1 in zip(cuts[:-1], cuts[1:])]

    def conv(slot, k, col, r0, r1, masked):
        cw = cw_ref[:, col:col + tn]
        zp = z_scr[slot, k, FFN_HALO - 1 + r0:FFN_HALO - 1 + r1, :]
        zc = z_scr[slot, k, FFN_HALO + r0:FFN_HALO + r1, :]
        zn = z_scr[slot, k, FFN_HALO + 1 + r0:FFN_HALO + 1 + r1, :]
        if masked:
            zp = jnp.where(pos[r0:r1] == 0, 0.0, zp)
            zn = jnp.where(pos[r0:r1] == seq_len - 1, 0.0, zn)
        return zp * cw[0:1] + zc * cw[1:2] + zn * cw[2:3] + cb_ref[:, col:col + tn]

    h = h_scr[...]
    for jc in range(D_FF // tn):
        ca, cb = jc * tn, D_FF + jc * tn
        slot = jc % 2
        z_scr[slot, 0] = jnp.dot(h, wup_ref[:, ca:ca + tn], preferred_element_type=F32)
        z_scr[slot, 1] = jnp.dot(h, wup_ref[:, cb:cb + tn], preferred_element_type=F32)
        for r0, r1, masked in segments:
            act_scr[r0:r1, ca:ca + tn] = (jax.nn.silu(conv(slot, 0, ca, r0, r1, masked))
                                          * conv(slot, 1, cb, r0, r1, masked)).astype(BF16)
    o_ref[...] = x_ref[...] + gt * jnp.dot(act_scr[...], wd_ref[...], preferred_element_type=F32)


def _conv_ffn(x, mod_l, g, layer, w_up, conv_w, conv_b, w_down, tok0=0, n_tok=N_TOK):
    tm, halo = TM_FFN, FFN_HALO
    row = _mod_row(tm)
    n_halo_blocks = N_TOK // halo
    tile0 = tok0 // tm
    resident = lambda a: pl.BlockSpec((None,) + a.shape[1:], lambda i: (layer,) + (0,) * (a.ndim - 1),
                                      pipeline_mode=pl.Buffered(1))
    cb3 = conv_b.reshape(DEPTH, 1, -1)
    return pl.pallas_call(
        functools.partial(_ffn_kernel, tile0=tile0),
        out_shape=jax.ShapeDtypeStruct((n_tok, D_MODEL), F32),
        grid=(n_tok // tm,),
        in_specs=[
            pl.BlockSpec((halo, D_MODEL), lambda i: (jnp.maximum((i + tile0) * (tm // halo) - 1, 0), 0)),
            pl.BlockSpec((tm, D_MODEL), lambda i: (i + tile0, 0)),
            pl.BlockSpec((halo, D_MODEL),
                         lambda i: (jnp.minimum((i + tile0 + 1) * (tm // halo), n_halo_blocks - 1), 0)),
            pl.BlockSpec((1, 6, D_MODEL), lambda i: (row(i + tile0), 0, 0)),
            pl.BlockSpec((1, D_MODEL), lambda i: (0, 0)),
            resident(w_up), resident(conv_w), resident(cb3), resident(w_down),
        ],
        out_specs=pl.BlockSpec((tm, D_MODEL), lambda i: (i, 0)),
        scratch_shapes=[pltpu.VMEM((tm + 2 * halo, D_MODEL), BF16), pltpu.VMEM((tm, D_FF), BF16),
                        pltpu.VMEM((2, 2, tm + 2 * halo, TN_FFN), F32)],
        compiler_params=_cparams(("arbitrary",)),
        name="conv_ffn",
    )(x, x, x, mod_l, g.reshape(1, D_MODEL), w_up, conv_w, cb3, w_down)


def _rope(x, cos, sin, half):
    lane = lax.broadcasted_iota(jnp.int32, x.shape, 1)
    first = ((lane // half) & 1) == 0
    partner = jnp.where(first, pltpu.roll(x, LANES - half, 1), pltpu.roll(x, half, 1))
    return x * cos + partner * sin


def _head_norm(xh, g, dim):
    sq = xh * xh
    hi = sq.astype(BF16)
    lo = (sq - hi.astype(F32)).astype(BF16)
    ones = jnp.ones((2 * HEAD_PAD, HEAD_PAD), BF16)
    total = jnp.dot(jnp.concatenate([hi, lo], axis=1), ones, preferred_element_type=F32)
    return xh * lax.rsqrt(total * (1.0 / dim) + EPS) * g


def _mla_kv_heads(ckv_bf, kpe, w_ukv_ref, g_mk, cos_m, sin_m, k_ref, v_ref):
    kv = jnp.dot(ckv_bf, w_ukv_ref[...], preferred_element_type=F32)
    for h in range(MLA_HEADS):
        kh = kv[:, h * HEAD_PAD:(h + 1) * HEAD_PAD] + kpe
        kh = _head_norm(kh, g_mk, MLA_QK)
        if cos_m is not None:
            kh = _rope(kh, cos_m, sin_m, MLA_ROPE // 4)
        k_ref[:, h * HEAD_PAD:(h + 1) * HEAD_PAD] = kh.astype(BF16)
    v_ref[:, 0:MLA_HEADS * MLA_V] = kv[:, MLA_HEADS * HEAD_PAD:].astype(BF16)


def _attn_pre_kernel(x_ref, mod_ref, g1_ref, w_in_ref, g_qa_ref, g_kva_ref, w_uq_ref, w_ukv_ref,
                     g_mq_ref, g_mk_ref, g_gq_ref, g_gk_ref, cos_m_ref, sin_m_ref, cos_g_ref, sin_g_ref,
                     q_ref, k_ref, v_ref, ckv_ref, kpe_ref, gk_ref, gv_ref):
    m = mod_ref[0]
    sh, sc = m[0:1], m[1:2]
    h = (_rms(x_ref[...], g1_ref[...]) * (1.0 + sc) + sh).astype(BF16)
    p = jnp.dot(h, w_in_ref[...], preferred_element_type=F32)
    q_c = _rms(p[:, IN_Q:IN_CKV], g_qa_ref[...]).astype(BF16)
    ckv = _rms(p[:, IN_CKV:IN_KPE], g_kva_ref[...])
    kpe = p[:, IN_KPE:IN_GQ]
    ckv_ref[...] = ckv
    kpe_ref[...] = kpe
    gv = p[:, IN_GV:IN_EXT]
    gv_ref[...] = gv
    v_ref[:, MLA_HEADS * MLA_V:] = gv.astype(BF16)
    qm = jnp.dot(q_c, w_uq_ref[...], preferred_element_type=F32)

    def heads(rope):
        cos_m, sin_m = (cos_m_ref[...], sin_m_ref[...]) if rope else (None, None)
        cos_g, sin_g = (cos_g_ref[...], sin_g_ref[...]) if rope else (None, None)
        g_mq = g_mq_ref[...]
        for hd in range(MLA_HEADS):
            qh = _head_norm(qm[:, hd * HEAD_PAD:(hd + 1) * HEAD_PAD], g_mq, MLA_QK)
            if rope:
                qh = _rope(qh, cos_m, sin_m, MLA_ROPE // 4)
            q_ref[:, hd * HEAD_PAD:(hd + 1) * HEAD_PAD] = (qh * (1.0 / math.sqrt(MLA_QK))).astype(BF16)
        _mla_kv_heads(ckv.astype(BF16), kpe, w_ukv_ref, g_mk_ref[...], cos_m, sin_m, k_ref, v_ref)
        g_gq = g_gq_ref[...]
        for hd in range(GQA_HEADS):
            qh = _head_norm(p[:, IN_GQ + hd * HEAD_PAD:IN_GQ + (hd + 1) * HEAD_PAD], g_gq, GQA_DIM)
            if rope:
                qh = _rope(qh, cos_g, sin_g, GQA_DIM // 4)
            q_ref[:, (MLA_HEADS + hd) * HEAD_PAD:(MLA_HEADS + hd + 1) * HEAD_PAD] = (
                qh * (1.0 / math.sqrt(GQA_DIM))).astype(BF16)
        g_gk = g_gk_ref[...]
        for hd in range(GQA_KV_HEADS):
            kh = _head_norm(p[:, IN_GK + hd * HEAD_PAD:IN_GK + (hd + 1) * HEAD_PAD], g_gk, GQA_DIM)
            gk_ref[:, hd * HEAD_PAD:(hd + 1) * HEAD_PAD] = kh
            if rope:
                kh = _rope(kh, cos_g, sin_g, GQA_DIM // 4)
            k_ref[:, (MLA_HEADS + hd) * HEAD_PAD:(MLA_HEADS + hd + 1) * HEAD_PAD] = kh.astype(BF16)

    is_latent = pl.program_id(0) >= N_PROMPT // x_ref.shape[0]
    pl.when(is_latent)(lambda: heads(True))
    pl.when(jnp.logical_not(is_latent))(lambda: heads(False))


def _attn_pre(x, mod_l, g1, aw, tabs):
    tm = TM_PRE
    row = _mod_row(tm)
    full = lambda a: pl.BlockSpec(a.shape, lambda i: (0,) * a.ndim)
    tok = lambda w: pl.BlockSpec((tm, w), lambda i: (i, 0))
    pos_block = lambda i: (jnp.maximum(i - N_PROMPT // tm, 0) % (DEC_SEQ // tm), 0)
    consts = [g1.reshape(1, D_MODEL), aw["w_in"], aw["g_qa"], aw["g_kva"], aw["w_uq"], aw["w_ukv"],
              aw["g_mq"], aw["g_mk"], aw["g_gq"], aw["g_gk"]]
    return pl.pallas_call(
        _attn_pre_kernel,
        out_shape=[
            jax.ShapeDtypeStruct((N_TOK, N_HEADS * HEAD_PAD), BF16),
            jax.ShapeDtypeStruct((N_TOK, K_ALL), BF16),
            jax.ShapeDtypeStruct((N_TOK, V_ALL), BF16),
            jax.ShapeDtypeStruct((N_TOK, KV_LORA), F32),
            jax.ShapeDtypeStruct((N_TOK, HEAD_PAD), F32),
            jax.ShapeDtypeStruct((N_TOK, GQA_KV_HEADS * HEAD_PAD), F32),
            jax.ShapeDtypeStruct((N_TOK, GQA_KV_HEADS * GQA_DIM), F32),
        ],
        grid=(N_TOK // tm,),
        in_specs=[tok(D_MODEL), pl.BlockSpec((1, 6, D_MODEL), lambda i: (row(i), 0, 0))]
        + [full(a) for a in consts] + [pl.BlockSpec((tm, HEAD_PAD), pos_block)] * 4,
        out_specs=[tok(N_HEADS * HEAD_PAD), tok(K_ALL), tok(V_ALL), tok(KV_LORA), tok(HEAD_PAD),
                   tok(GQA_KV_HEADS * HEAD_PAD), tok(GQA_KV_HEADS * GQA_DIM)],
        compiler_params=_cparams(("arbitrary",)),
        name="attn_pre",
    )(x, mod_l, *consts, *tabs)


def _cache_kv_kernel(ckv_ref, kpe_ref, w_ukv_ref, g_mk_ref, k_ref, v_ref):
    _mla_kv_heads(ckv_ref[...].astype(BF16), kpe_ref[...], w_ukv_ref, g_mk_ref[...], None, None, k_ref, v_ref)


def _cache_kv(ckv, kpe_pad, aw):
    n = ckv.shape[0]
    return pl.pallas_call(
        _cache_kv_kernel,
        out_shape=[jax.ShapeDtypeStruct((n, MLA_HEADS * HEAD_PAD), BF16),
                   jax.ShapeDtypeStruct((n, MLA_HEADS * MLA_V), BF16)],
        name="cache_kv",
    )(ckv, kpe_pad, aw["w_ukv"], aw["g_mk"])


def _attn_kernel(*refs, n_seg):
    q_ref, kv_refs = refs[0], refs[1:1 + 2 * n_seg]
    x_ref, mod_ref, wo_ref, o_ref, oh_scr = refs[1 + 2 * n_seg:]
    for hd in range(N_HEADS):
        if hd < MLA_HEADS:
            kc, vc = hd * HEAD_PAD, hd * MLA_V
        else:
            kvh = (hd - MLA_HEADS) // GQA_REP
            kc, vc = (MLA_HEADS + kvh) * HEAD_PAD, MLA_HEADS * MLA_V + kvh * GQA_DIM
        q = q_ref[:, hd * HEAD_PAD:(hd + 1) * HEAD_PAD]
        ss = [lax.dot_general(q, kv_refs[2 * j][:, kc:kc + HEAD_PAD], (((1,), (1,)), ((), ())),
                              preferred_element_type=F32) for j in range(n_seg)]
        m = functools.reduce(jnp.maximum, [jnp.max(s, axis=-1, keepdims=True) for s in ss])
        ps = [jnp.exp(s - m) for s in ss]
        den = sum(jnp.sum(p, axis=-1, keepdims=True) for p in ps)
        o = sum(jnp.dot(p.astype(BF16), kv_refs[2 * j + 1][:, vc:vc + MLA_V], preferred_element_type=F32)
                for j, p in enumerate(ps))
        oh_scr[:, hd * MLA_V:(hd + 1) * MLA_V] = (o / den).astype(BF16)
    gt = mod_ref[0][2:3]
    o_ref[...] = x_ref[...] + gt * jnp.dot(oh_scr[...], wo_ref[...], preferred_element_type=F32)


def _attention(q, segs, x, mod_l, w_out, tq, n_batch, t_len, tok0, mod_row):
    tiles = t_len // tq
    qmap = lambda bi, qi: (tok0 // tq + bi * tiles + qi, 0)
    kv_specs, kv_args = [], []
    for k, v, rows, blk0 in segs:
        kv_specs += [pl.BlockSpec((rows, K_ALL), lambda bi, qi, blk0=blk0: (blk0 + bi, 0)),
                     pl.BlockSpec((rows, V_ALL), lambda bi, qi, blk0=blk0: (blk0 + bi, 0))]
        kv_args += [k, v]
    return pl.pallas_call(
        functools.partial(_attn_kernel, n_seg=len(segs)),
        out_shape=jax.ShapeDtypeStruct((N_TOK, D_MODEL), F32),
        grid=(n_batch, tiles),
        in_specs=[pl.BlockSpec((tq, N_HEADS * HEAD_PAD), qmap)] + kv_specs + [
            pl.BlockSpec((tq, D_MODEL), qmap),
            pl.BlockSpec((1, 6, D_MODEL), lambda bi, qi: (mod_row(bi), 0, 0)),
            pl.BlockSpec((D_MODEL, D_MODEL), lambda bi, qi: (0, 0)),
        ],
        out_specs=pl.BlockSpec((tq, D_MODEL), qmap),
        scratch_shapes=[pltpu.VMEM((tq, D_MODEL), BF16)],
        input_output_aliases={1 + len(kv_args): 0},
        compiler_params=_cparams(("arbitrary", "arbitrary")),
        name="attention",
    )(q, *kv_args, x, mod_l, w_out)


def _cmul(ar, ai, br, bi):
    return ar * br - ai * bi, ar * bi + ai * br


def _ssm_ops_kernel(a_ref, dt_ref, bt_ref, c_ref, ops_ref, toep_ref, wct_ref, lam_ref,
                    cl_re_scr, cl_im_scr, op_scr, tp_scr, wc_scr):
    L, H, P = SSM_CHUNK, SSM_GROUP, SSM_STATE
    lane = lax.broadcasted_iota(jnp.int32, (H, CHUNK_W), 1)

    a_re, a_im = a_ref[0], a_ref[1]
    dt_all = jnp.exp(dt_ref[...])
    mag = jnp.exp(a_re * dt_all)
    ang = a_im * dt_all
    ab_re, ab_im = mag * jnp.cos(ang), mag * jnp.sin(ang)
    den = a_re * a_re + a_im * a_im
    n_re, n_im = ab_re - 1.0, ab_im
    k_re_all = (n_re * a_re + n_im * a_im) / den
    k_im_all = (n_im * a_re - n_re * a_im) / den
    pw_all = [(jnp.ones_like(ab_re), jnp.zeros_like(ab_im))]
    for _ in range(L):
        pw_all.append(_cmul(pw_all[-1][0], pw_all[-1][1], ab_re, ab_im))

    for g in range(ops_ref.shape[0]):
        lag = []
        for d in range(2):
            r = 2 * g + d
            bt_re, bt_im = bt_ref[g, d, 0], bt_ref[g, d, 1]
            bb_re, bb_im = _cmul(k_re_all[r:r + 1], k_im_all[r:r + 1], bt_re, bt_im)
            c_re, c_im = c_ref[g, d, 0], c_ref[g, d, 1]
            pw = [(p_re[r:r + 1], p_im[r:r + 1]) for p_re, p_im in pw_all]

            for e in range(L + 1):
                pr, pi = pw[e] if d == 0 else pw[L - e]
                cr, ci = _cmul(c_re, c_im, pr, pi)
                cl_re_scr[r,e * H:(e + 1) * H, :] = cr
                cl_im_scr[r,e * H:(e + 1) * H, :] = ci
            lo = 0 if d == 0 else H
            lag.append(lax.dot_general(bb_re, cl_re_scr[r,lo:lo + CHUNK_W, :], (((1,), (1,)), ((), ())),
                                       preferred_element_type=F32, precision=HIGHEST)
                       - lax.dot_general(bb_im, cl_im_scr[r,lo:lo + CHUNK_W, :], (((1,), (1,)), ((), ())),
                                         preferred_element_type=F32, precision=HIGHEST))
            wlo = H if d == 0 else 0
            wc_scr[g, :, d * STATE_W:d * STATE_W + P] = cl_re_scr[r,wlo:wlo + CHUNK_W, :]
            wc_scr[g, :, d * STATE_W + P:(d + 1) * STATE_W] = -cl_im_scr[r,wlo:wlo + CHUNK_W, :]
            for j in range(L):
                pj = pw[L - 1 - j] if d == 0 else pw[j]
                sr, si = _cmul(pj[0], pj[1], bb_re, bb_im)
                op_scr[g, j * H:(j + 1) * H, d * STATE_W:d * STATE_W + P] = sr
                op_scr[g, j * H:(j + 1) * H, d * STATE_W + P:(d + 1) * STATE_W] = si
            pr, pi = pw[L]
            lam_ref[g, 2 * d:2 * d + 1, 0:P] = pr
            lam_ref[g, 2 * d:2 * d + 1, P:STATE_W] = pr
            lam_ref[g, 2 * d + 1:2 * d + 2, 0:P] = -pi
            lam_ref[g, 2 * d + 1:2 * d + 2, P:STATE_W] = pi
        for j in range(L):
            fwd = lag[0] if j == 0 else jnp.where(lane >= H * j, pltpu.roll(lag[0], H * j, 1), 0.0)
            sft = (L - 1 - j) * H
            bwd = lag[1] if sft == 0 else jnp.where(lane < H * (j + 1), pltpu.roll(lag[1], CHUNK_W - sft, 1), 0.0)
            tp_scr[g, j * H:(j + 1) * H, :] = fwd + bwd
        ops_ref[g] = op_scr[g].astype(BF16)
        toep_ref[g] = tp_scr[g].astype(BF16)
        wct_ref[g] = wc_scr[g].astype(BF16)


def _ssm_ops(a, dt, bt, c):
    gb = GB_SSM
    G = SSM_GROUPS
    blk = lambda shape: pl.BlockSpec((gb,) + shape, lambda i: (i,) + (0,) * len(shape))
    return pl.pallas_call(
        _ssm_ops_kernel,
        out_shape=[jax.ShapeDtypeStruct((G, CHUNK_W, 2 * STATE_W), BF16),
                   jax.ShapeDtypeStruct((G, CHUNK_W, CHUNK_W), BF16),
                   jax.ShapeDtypeStruct((G, CHUNK_W, 2 * STATE_W), BF16),
                   jax.ShapeDtypeStruct((G, 4, STATE_W), F32)],
        grid=(G // gb,),
        in_specs=[pl.BlockSpec((2, 2 * gb, SSM_STATE), lambda i: (0, i, 0)),
                  pl.BlockSpec((2 * gb, SSM_STATE), lambda i: (i, 0)),
                  blk((2, 2, SSM_GROUP, SSM_STATE)), blk((2, 2, SSM_GROUP, SSM_STATE))],
        out_specs=[blk((CHUNK_W, 2 * STATE_W)), blk((CHUNK_W, CHUNK_W)), blk((CHUNK_W, 2 * STATE_W)),
                   blk((4, STATE_W))],
        scratch_shapes=[pltpu.VMEM((2 * gb, (SSM_CHUNK + 1) * SSM_GROUP, SSM_STATE), F32),
                        pltpu.VMEM((2 * gb, (SSM_CHUNK + 1) * SSM_GROUP, SSM_STATE), F32),
                        pltpu.VMEM((gb, CHUNK_W, 2 * STATE_W), F32),
                        pltpu.VMEM((gb, CHUNK_W, CHUNK_W), F32),
                        pltpu.VMEM((gb, CHUNK_W, 2 * STATE_W), F32)],
        compiler_params=_cparams(("arbitrary",)),
        name="ssm_ops",
    )(a, dt, bt, c)


GRANULES = LANES // SSM_GROUP


def _granule_transpose(arrs):
    a = list(arrs)
    lane = lax.broadcasted_iota(jnp.int32, a[0].shape, 1)
    for d in (4, 2, 1):
        upper = (lane & (d * SSM_GROUP)) != 0
        for s in range(GRANULES):
            if s & d:
                continue
            lo, hi = a[s], a[s + d]
            a[s] = jnp.where(upper, pltpu.roll(hi, d * SSM_GROUP, 1), lo)
            a[s + d] = jnp.where(upper, hi, pltpu.roll(lo, LANES - d * SSM_GROUP, 1))
    return a


def _step_select(tm):
    mb = tm // SSM_CHUNK
    idx = (np.arange(mb)[None, :] * SSM_CHUNK + np.arange(SSM_CHUNK)[:, None]).reshape(-1)
    return jnp.asarray(np.eye(tm, dtype=np.float32)[idx], BF16)


def _ssm_in_kernel(x_ref, mod_ref, g_ref, sel_ref, u_ref):
    m = mod_ref[0]
    h = (_rms(x_ref[...], g_ref[...]) * (1.0 + m[1:2]) + m[0:1]).astype(BF16)
    by_step = jnp.dot(sel_ref[...], h, preferred_element_type=F32)
    mb = x_ref.shape[0] // SSM_CHUNK
    for g8 in range(SSM_GROUPS // GRANULES):
        for half in range(SSM_CHUNK // GRANULES):
            a = [by_step[(GRANULES * half + s) * mb:(GRANULES * half + s + 1) * mb, g8 * LANES:(g8 + 1) * LANES]
                 for s in range(GRANULES)]
            b = _granule_transpose(a)
            for g in range(GRANULES):
                u_ref[g8 * GRANULES + g, :, half * LANES:(half + 1) * LANES] = b[g].astype(BF16)


def _ssm_in(x, mod_l, g1):
    tm = TM_SSM_IN
    row = _mod_row(tm)
    return pl.pallas_call(
        _ssm_in_kernel,
        out_shape=jax.ShapeDtypeStruct((SSM_GROUPS, N_CHUNK_ROWS, CHUNK_W), BF16),
        grid=(N_TOK // tm,),
        in_specs=[pl.BlockSpec((tm, D_MODEL), lambda i: (i, 0)),
                  pl.BlockSpec((1, 6, D_MODEL), lambda i: (row(i), 0, 0)),
                  pl.BlockSpec((1, D_MODEL), lambda i: (0, 0)),
                  pl.BlockSpec((tm, tm), lambda i: (0, 0))],
        out_specs=pl.BlockSpec((SSM_GROUPS, tm // SSM_CHUNK, CHUNK_W), lambda i: (0, i, 0)),
        compiler_params=_cparams(("arbitrary",)),
        name="ssm_in",
    )(x, mod_l, g1.reshape(1, D_MODEL), _step_select(tm))


def _step_major_perm():
    n = SEQ // SSM_CHUNK
    assert n == BATCH
    idx = np.arange(N_CHUNK_PROMPT).reshape(BATCH, n).T.reshape(-1)
    return jnp.asarray(np.eye(N_CHUNK_PROMPT, dtype=np.float32)[idx], BF16)


def _ssm_ends_kernel(u_ref, ops_ref, perm_ref, *s_refs):
    n_p = N_CHUNK_PROMPT
    for g in range(u_ref.shape[0]):
        u_ctx = jnp.dot(perm_ref[...], u_ref[g, 0:n_p], preferred_element_type=F32).astype(BF16)
        s_ctx = jnp.dot(u_ctx, ops_ref[g], preferred_element_type=F32)
        s_lat = jnp.dot(u_ref[g, n_p:], ops_ref[g], preferred_element_type=F32)
        for k, s_ref in enumerate(s_refs):
            s_ref[g, 0:n_p] = s_ctx[:, k * STATE_W:(k + 1) * STATE_W]
            s_ref[g, n_p:] = s_lat[:, k * STATE_W:(k + 1) * STATE_W]


def _ssm_ends(u, ops, perm):
    gb = GB_SSM
    G = SSM_GROUPS
    return pl.pallas_call(
        _ssm_ends_kernel,
        out_shape=[jax.ShapeDtypeStruct((G, N_CHUNK_ROWS, STATE_W), F32)] * 2,
        grid=(G // gb,),
        in_specs=[pl.BlockSpec((gb, N_CHUNK_ROWS, CHUNK_W), lambda i: (i, 0, 0)),
                  pl.BlockSpec((gb, CHUNK_W, 2 * STATE_W), lambda i: (i, 0, 0)),
                  pl.BlockSpec(perm.shape, lambda i: (0, 0))],
        out_specs=[pl.BlockSpec((gb, N_CHUNK_ROWS, STATE_W), lambda i: (i, 0, 0))] * 2,
        compiler_params=_cparams(("arbitrary",)),
        name="ssm_chunk_ends",
    )(u, ops, perm)


def _ssm_carry_kernel(sf_ref, sb_ref, lam_ref, h0_ref, xf_ref, xb_ref, fin_ref):
    n_g = sf_ref.shape[0] // N_CHUNK_ROWS
    n_p, n_s = SEQ // SSM_CHUNK, DEC_SEQ // SSM_CHUNK

    def swap(t):
        return pltpu.roll(t, SSM_STATE, 1)

    def step(x, lam, rf, rb):
        xf, xfs, xb, xbs = x
        a1f, a2f, a1b, a2b = lam
        xf_ref[rf, :] = xf
        xb_ref[rb, :] = xb
        sf, sb = sf_ref[rf, :], sb_ref[rb, :]
        return [a1f * xf + a2f * xfs + sf, a1f * xfs - a2f * xf + swap(sf),
                a1b * xb + a2b * xbs + sb, a1b * xbs - a2b * xb + swap(sb)]

    zero = jnp.zeros((BATCH, STATE_W), F32)
    for g0 in range(0, n_g, GB_CARRY):
        groups = range(g0, g0 + GB_CARRY)
        lam = [[jnp.broadcast_to(lam_ref[4 * g + k:4 * g + k + 1, :], (BATCH, STATE_W)) for k in range(4)]
               for g in groups]

        def body_p(c, carry, groups=groups, lam=lam):
            out = []
            for j, g in enumerate(groups):
                rf = pl.ds(pl.multiple_of(g * N_CHUNK_ROWS + c * BATCH, BATCH), BATCH)
                rb = pl.ds(pl.multiple_of(g * N_CHUNK_ROWS + (n_p - 1 - c) * BATCH, BATCH), BATCH)
                out += step(carry[4 * j:4 * j + 4], lam[j], rf, rb)
            return tuple(out)

        fin = lax.fori_loop(0, n_p, body_p, (zero,) * (4 * GB_CARRY), unroll=2)
        for j, g in enumerate(groups):
            fin_ref[g, 0] = fin[4 * j]
            fin_ref[g, 1] = fin[4 * j + 2]

    packs = [(g0, b) for g0 in range(0, n_g, 8) for b in range(DEC_BATCH)]
    lam = {g0: [lam_ref[pl.ds(4 * g0 + k, 8, stride=4), :] for k in range(4)] for g0 in range(0, n_g, 8)}

    def body_s(c, carry):
        out = []
        for j, (g0, b) in enumerate(packs):
            base = g0 * N_CHUNK_ROWS + N_CHUNK_PROMPT + b * n_s
            rf = pl.ds(base + c, 8, stride=N_CHUNK_ROWS)
            rb = pl.ds(base + (n_s - 1 - c), 8, stride=N_CHUNK_ROWS)
            out += step(carry[4 * j:4 * j + 4], lam[g0], rf, rb)
        return tuple(out)

    init = []
    for g0, b in packs:
        h0f = h0_ref[pl.ds(4 * g0 + b, 8, stride=4), :]
        h0b = h0_ref[pl.ds(4 * g0 + DEC_BATCH + b, 8, stride=4), :]
        init += [h0f, swap(h0f), h0b, swap(h0b)]
    lax.fori_loop(0, n_s, body_s, tuple(init), unroll=4)


def _ssm_carry(sf, sb, lam, h0):
    gb = GB_CARRY_BLOCK
    G = SSM_GROUPS
    st = pl.BlockSpec((gb * N_CHUNK_ROWS, STATE_W), lambda i: (i, 0))
    small = pl.BlockSpec((gb * 4, STATE_W), lambda i: (i, 0))
    return pl.pallas_call(
        _ssm_carry_kernel,
        out_shape=[jax.ShapeDtypeStruct(sf.shape, F32)] * 2 + [jax.ShapeDtypeStruct((G, 2, BATCH, STATE_W), F32)],
        grid=(G // gb,),
        in_specs=[st, st, small, small],
        out_specs=[st, st, pl.BlockSpec((gb, 2, BATCH, STATE_W), lambda i: (i, 0, 0, 0))],
        compiler_params=_cparams(("arbitrary",)),
        name="ssm_carry",
    )(sf, sb, lam, h0)


def _ssm_out_kernel(u_ref, toep_ref, xf_ref, xb_ref, wct_ref, perm_ref, y_ref, y_scr):
    n_p = N_CHUNK_PROMPT
    for g in range(GRANULES):
        y = jnp.dot(u_ref[g], toep_ref[g], preferred_element_type=F32)
        for d, x_ref in enumerate((xf_ref, xb_ref)):
            x_ctx = jnp.dot(perm_ref[...], x_ref[g, 0:n_p].astype(BF16), preferred_element_type=F32).astype(BF16)
            xin = jnp.concatenate([x_ctx, x_ref[g, n_p:].astype(BF16)], axis=0)
            w = wct_ref[g, :, d * STATE_W:(d + 1) * STATE_W]
            y += lax.dot_general(xin, w, (((1,), (1,)), ((), ())), preferred_element_type=F32)
        y_scr[g] = y
    rb = OUT_ROW_BLOCK
    for r in range(N_CHUNK_ROWS // rb):
        for half in range(SSM_CHUNK // GRANULES):
            a = [y_scr[g, r * rb:(r + 1) * rb, half * LANES:(half + 1) * LANES] for g in range(GRANULES)]
            b = _granule_transpose(a)
            for s in range(GRANULES):
                y_ref[pl.ds(r * rb * SSM_CHUNK + GRANULES * half + s, rb, stride=SSM_CHUNK), :] = b[s]


def _ssm_out(u, toep, xf, xb, wct, perm):
    gb = GRANULES
    G = SSM_GROUPS
    return pl.pallas_call(
        _ssm_out_kernel,
        out_shape=jax.ShapeDtypeStruct((N_TOK, D_MODEL), F32),
        grid=(G // gb,),
        in_specs=[pl.BlockSpec((gb, N_CHUNK_ROWS, CHUNK_W), lambda i: (i, 0, 0)),
                  pl.BlockSpec((gb, CHUNK_W, CHUNK_W), lambda i: (i, 0, 0)),
                  pl.BlockSpec((gb, N_CHUNK_ROWS, STATE_W), lambda i: (i, 0, 0)),
                  pl.BlockSpec((gb, N_CHUNK_ROWS, STATE_W), lambda i: (i, 0, 0)),
                  pl.BlockSpec((gb, CHUNK_W, 2 * STATE_W), lambda i: (i, 0, 0)),
                  pl.BlockSpec(perm.shape, lambda i: (0, 0))],
        out_specs=pl.BlockSpec((N_TOK, LANES), lambda i: (0, i)),
        scratch_shapes=[pltpu.VMEM((gb, N_CHUNK_ROWS, CHUNK_W), F32)],
        compiler_params=_cparams(("arbitrary",)),
        name="ssm_chunk_out",
    )(u, toep, xf, xb, wct, perm)


def _ssm_post_kernel(x_ref, mod_ref, g_ref, y_ref, d_ref, w_ref, b_ref, o_ref):
    m = mod_ref[0]
    x = x_ref[...]
    h = _rms(x, g_ref[...]) * (1.0 + m[1:2]) + m[0:1]
    y = h * d_ref[...] + y_ref[...]
    gl = jax.nn.gelu(y).astype(BF16)
    z = jnp.dot(gl, w_ref[...], preferred_element_type=F32) + b_ref[...]
    o_ref[...] = x + m[2:3] * (z[:, :D_MODEL] * jax.nn.sigmoid(z[:, D_MODEL:]))


def _ssm_post(x, mod_l, g1, y, d_skip, w_glu, b_glu):
    tm = TM_SSM
    row = _mod_row(tm)
    full = lambda a: pl.BlockSpec(a.shape, lambda i: (0,) * a.ndim)
    d2, b2 = d_skip.reshape(1, D_MODEL), b_glu.reshape(1, 2 * D_MODEL)
    g2 = g1.reshape(1, D_MODEL)
    return pl.pallas_call(
        _ssm_post_kernel,
        out_shape=jax.ShapeDtypeStruct((N_TOK, D_MODEL), F32),
        grid=(N_TOK // tm,),
        in_specs=[pl.BlockSpec((tm, D_MODEL), lambda i: (i, 0)),
                  pl.BlockSpec((1, 6, D_MODEL), lambda i: (row(i), 0, 0)),
                  full(g2), pl.BlockSpec((tm, D_MODEL), lambda i: (i, 0)), full(d2), full(w_glu), full(b2)],
        out_specs=pl.BlockSpec((tm, D_MODEL), lambda i: (i, 0)),
        compiler_params=_cparams(("arbitrary",)),
        name="ssm_post",
    )(x, mod_l, g2, y, d2, w_glu, b2)


def _rope_tables():
    rows = DEC_SEQ // GRID_W
    t = np.arange(DEC_SEQ)
    row, col = (t // GRID_W).astype(np.float32), (t % GRID_W).astype(np.float32)

    def table(rot_dim, lane0):
        n_freq = rot_dim // 4
        inv_freq = jnp.asarray(ROPE_THETA, F32) ** (-jnp.arange(n_freq, dtype=F32) / n_freq)
        ang_row = jnp.asarray(row)[:, None] * inv_freq
        ang_col = jnp.asarray(col)[:, None] * inv_freq
        ang = jnp.concatenate([ang_row, ang_row, ang_col, ang_col], axis=1)
        sign = jnp.tile(jnp.concatenate([-jnp.ones(n_freq, F32), jnp.ones(n_freq, F32)]), 2)
        cos = jnp.ones((DEC_SEQ, HEAD_PAD), F32).at[:, lane0:lane0 + rot_dim].set(jnp.cos(ang))
        sin = jnp.zeros((DEC_SEQ, HEAD_PAD), F32).at[:, lane0:lane0 + rot_dim].set(jnp.sin(ang) * sign)
        return cos, sin

    assert rows * GRID_W == DEC_SEQ
    return table(MLA_ROPE, MLA_NOPE) + table(GQA_DIM, 0)


def _pad_heads(w, n_heads, dim):
    lead = w.shape[:-1]
    w = w.reshape(lead + (n_heads, dim))
    w = jnp.pad(w, [(0, 0)] * len(lead) + [(0, 0), (0, HEAD_PAD - dim)])
    return w.reshape(lead + (n_heads * HEAD_PAD,))


def _attn_weights(w_in, g_qa, g_kva, w_uq, w_ukv, g_mq, g_mk, g_gq, g_gk, w_out):
    o1, o2, o3 = Q_LORA, Q_LORA + KV_LORA, Q_LORA + KV_LORA + MLA_ROPE
    o4 = o3 + GQA_HEADS * GQA_DIM
    o5 = o4 + GQA_KV_HEADS * GQA_DIM
    kpe = jnp.pad(w_in[:, o2:o3], ((0, 0), (MLA_NOPE, HEAD_PAD - MLA_QK)))
    w_in_ext = jnp.concatenate([w_in[:, :o2], kpe, _pad_heads(w_in[:, o3:o4], GQA_HEADS, GQA_DIM),
                                _pad_heads(w_in[:, o4:o5], GQA_KV_HEADS, GQA_DIM), w_in[:, o5:]], axis=1)
    ukv = w_ukv.reshape(KV_LORA, MLA_HEADS, MLA_NOPE + MLA_V)
    w_ukv_perm = jnp.concatenate([_pad_heads(ukv[:, :, :MLA_NOPE].reshape(KV_LORA, -1), MLA_HEADS, MLA_NOPE),
                                  ukv[:, :, MLA_NOPE:].reshape(KV_LORA, -1)], axis=1)
    pad_g = lambda g: jnp.pad(g, (0, HEAD_PAD - g.shape[0])).reshape(1, HEAD_PAD)
    return dict(w_in=w_in_ext.astype(BF16), g_qa=g_qa.reshape(1, -1), g_kva=g_kva.reshape(1, -1),
                w_uq=_pad_heads(w_uq, MLA_HEADS, MLA_QK).astype(BF16), w_ukv=w_ukv_perm.astype(BF16),
                g_mq=pad_g(g_mq), g_mk=pad_g(g_mk), g_gq=pad_g(g_gq), g_gk=pad_g(g_gk),
                w_out=w_out.astype(BF16))


def _attn_layer(x, mod_l, g1, aw, tabs, cache):
    q, k, v, ckv, kpe, gk, gv = _attn_pre(x, mod_l, g1, aw, tabs)
    c_ckv, c_krope, c_gk, c_gv = cache
    n_c = DEC_BATCH * PAST_LEN
    kpe_c = jnp.pad(c_krope.reshape(n_c, MLA_ROPE), ((0, 0), (MLA_NOPE, HEAD_PAD - MLA_QK)))
    km_c, vm_c = _cache_kv(c_ckv.reshape(n_c, KV_LORA), kpe_c, aw)
    k_c = jnp.concatenate([km_c, _pad_heads(c_gk.reshape(n_c, -1), GQA_KV_HEADS, GQA_DIM).astype(BF16)], axis=1)
    v_c = jnp.concatenate([vm_c, c_gv.reshape(n_c, -1).astype(BF16)], axis=1)
    x = _attention(q, [(k, v, SEQ, 0)], x, mod_l, aw["w_out"], SEQ, BATCH, SEQ, 0, lambda b: 0)
    x = _attention(q, [(k_c, v_c, PAST_LEN, 0), (k, v, DEC_SEQ, N_PROMPT // DEC_SEQ)], x, mod_l, aw["w_out"],
                   TQ_SAMPLE, DEC_BATCH, DEC_SEQ, N_PROMPT, lambda b: 1 + b)
    new = (ckv[:N_PROMPT].reshape(BATCH, SEQ, KV_LORA),
           kpe[:N_PROMPT, MLA_NOPE:MLA_QK].reshape(BATCH, SEQ, MLA_ROPE),
           gk[:N_PROMPT].reshape(BATCH, SEQ, GQA_KV_HEADS, HEAD_PAD)[..., :GQA_DIM],
           gv[:N_PROMPT].reshape(BATCH, SEQ, GQA_KV_HEADS, GQA_DIM))
    return x, new


def _ssm_layer(x, mod_l, g1, a_re, a_im, log_dt, b_re, b_im, c_re, c_im, d_skip, w_glu, b_glu, state0):
    G, P, H = SSM_GROUPS, SSM_STATE, SSM_GROUP
    a = jnp.stack([a_re, a_im], axis=0).transpose(0, 2, 1, 3).reshape(2, G * 2, P)
    dt = jnp.broadcast_to(log_dt.transpose(1, 0)[:, :, None], (G, 2, P)).reshape(G * 2, P)
    bt = jnp.stack([b_re, b_im], axis=2).transpose(1, 0, 2, 4, 3)
    c = jnp.stack([c_re, c_im], axis=2).transpose(1, 0, 2, 3, 4)
    ops, toep, wct, lam = _ssm_ops(a, dt, bt, c)

    u = _ssm_in(x, mod_l, g1)
    perm = _step_major_perm()
    sf, sb = _ssm_ends(u, ops, perm)
    h0 = state0.transpose(2, 1, 0, 4, 3).reshape(G * 2 * DEC_BATCH, STATE_W)
    flat = lambda s: s.reshape(G * N_CHUNK_ROWS, STATE_W)
    xf, xb, fin = _ssm_carry(flat(sf), flat(sb), lam.reshape(G * 4, STATE_W), h0)
    y = _ssm_out(u, toep, xf.reshape(sf.shape), xb.reshape(sb.shape), wct, perm)
    x = _ssm_post(x, mod_l, g1, y, d_skip, w_glu.astype(BF16), b_glu)
    fin = fin.reshape(G, 2, BATCH, 2, P).transpose(2, 1, 0, 4, 3)
    return x, fin


def kernel(x_prompt, x_sample, c, cache_mla_ckv, cache_mla_krope, cache_gqa_k, cache_gqa_v, state_ssm, c_ctx,
           norm1_g, norm2_g, w_mod, b_mod,
           attn_w_in, attn_qa_norm_g, attn_kva_norm_g, attn_w_uq, attn_w_ukv,
           attn_mla_q_norm_g, attn_mla_k_norm_g, attn_gqa_q_norm_g, attn_gqa_k_norm_g, attn_w_out,
           ssm_a_re, ssm_a_im, ssm_log_dt, ssm_b_re, ssm_b_im, ssm_c_re, ssm_c_im, ssm_d, ssm_w_glu, ssm_b_glu,
           ffn_w_up, ffn_conv_w, ffn_conv_b, ffn_w_down):
    x = jnp.concatenate([x_prompt.reshape(N_PROMPT, D_MODEL), x_sample.reshape(N_SAMPLE, D_MODEL)], axis=0)
    cond8 = jnp.concatenate([c_ctx[None, :], c, jnp.zeros((8 - 1 - DEC_BATCH, D_MODEL), F32)], axis=0)
    mod = _modulation(cond8, w_mod, b_mod).reshape(DEPTH, 8, 6, D_MODEL)
    tabs = _rope_tables()

    w_up_bf, w_down_bf = ffn_w_up.astype(BF16), ffn_w_down.astype(BF16)
    new_attn, new_ssm = [], []
    for i in range(DEPTH):
        j = i // 2
        if i % 2 == 0:
            aw = _attn_weights(attn_w_in[j], attn_qa_norm_g[j], attn_kva_norm_g[j], attn_w_uq[j], attn_w_ukv[j],
                               attn_mla_q_norm_g[j], attn_mla_k_norm_g[j], attn_gqa_q_norm_g[j],
                               attn_gqa_k_norm_g[j], attn_w_out[j])
            cache = (cache_mla_ckv[:, j], cache_mla_krope[:, j], cache_gqa_k[:, j], cache_gqa_v[:, j])
            x, new = _attn_layer(x, mod[i], norm1_g[i], aw, tabs, cache)
            new_attn.append(new)
        else:
            x, fin = _ssm_layer(x, mod[i], norm1_g[i], ssm_a_re[j], ssm_a_im[j], ssm_log_dt[j], ssm_b_re[j],
                                ssm_b_im[j], ssm_c_re[j], ssm_c_im[j], ssm_d[j], ssm_w_glu[j], ssm_b_glu[j],
                                state_ssm[:, j])
            new_ssm.append(fin)
        ffn = functools.partial(_conv_ffn, x, mod[i], norm2_g[i], i, w_up_bf, ffn_conv_w, ffn_conv_b, w_down_bf)
        if i < DEPTH - 1:
            x = ffn()
        else:
            y_prompt, y_sample = ffn(0, N_PROMPT), ffn(N_PROMPT, N_SAMPLE)

    outs = [jnp.stack([n[k] for n in new_attn], axis=1) for k in range(4)]
    return (y_prompt.reshape(BATCH, SEQ, D_MODEL), y_sample.reshape(DEC_BATCH, DEC_SEQ, D_MODEL),
            outs[0], outs[1], outs[2], outs[3], jnp.stack(new_ssm, axis=1))
```

```python
import functools
import math

import jax
import jax.numpy as jnp
import numpy as np
from jax import lax
from jax.experimental import pallas as pl
from jax.experimental.pallas import tpu as pltpu

F32 = jnp.float32
BF16 = jnp.bfloat16
HIGHEST = lax.Precision.HIGHEST

LANES = 128
BF16_SUBLANES = 16
VMEM_LIMIT_BYTES = 56 * 1024 * 1024

D_MODEL = 1024
BATCH = 16
SEQ = 256
DEPTH = 4
DEC_BATCH = 2
DEC_SEQ = 2048
PAST_LEN = 256
GRID_W = 64
MLA_HEADS = 8
Q_LORA = 384
KV_LORA = 256
MLA_NOPE = 64
MLA_ROPE = 32
MLA_QK = MLA_NOPE + MLA_ROPE
MLA_V = 64
GQA_HEADS = 8
GQA_KV_HEADS = 2
GQA_DIM = 64
GQA_REP = GQA_HEADS // GQA_KV_HEADS
ROPE_THETA = 10000.0
SSM_GROUP = 16
SSM_GROUPS = D_MODEL // SSM_GROUP
SSM_STATE = 64
D_FF = 2816
EPS = 1e-6

N_PROMPT = BATCH * SEQ
N_SAMPLE = DEC_BATCH * DEC_SEQ
N_TOK = N_PROMPT + N_SAMPLE
N_HEADS = MLA_HEADS + GQA_HEADS
HEAD_PAD = LANES

IN_Q = 0
IN_CKV = IN_Q + Q_LORA
IN_KPE = IN_CKV + KV_LORA
IN_GQ = IN_KPE + HEAD_PAD
IN_GK = IN_GQ + GQA_HEADS * HEAD_PAD
IN_GV = IN_GK + GQA_KV_HEADS * HEAD_PAD
IN_EXT = IN_GV + GQA_KV_HEADS * GQA_DIM

K_ALL = (MLA_HEADS + GQA_KV_HEADS) * HEAD_PAD
V_ALL = MLA_HEADS * MLA_V + GQA_KV_HEADS * GQA_DIM

SSM_CHUNK = 16
CHUNK_W = SSM_CHUNK * SSM_GROUP
N_CHUNK_ROWS = N_TOK // SSM_CHUNK
N_CHUNK_PROMPT = N_PROMPT // SSM_CHUNK
STATE_W = 2 * SSM_STATE

TM_FFN = 512
TN_FFN = 256
FFN_HALO = BF16_SUBLANES
TM_PRE = 512
TQ_SAMPLE = 256
TM_SSM = 1024
TM_SSM_IN = 512
OUT_ROW_BLOCK = 32
GB_CARRY = 4
GB_CARRY_BLOCK = 16
TN_MOD = 1536
GB_SSM = 8


def _cparams(sem):
    return pltpu.CompilerParams(dimension_semantics=sem, vmem_limit_bytes=VMEM_LIMIT_BYTES)


def _rms(x, g):
    return x * lax.rsqrt(jnp.mean(x * x, axis=-1, keepdims=True) + EPS) * g


def _mod_row(tm):
    n_prompt_tiles = N_PROMPT // tm
    tiles_per_seq = DEC_SEQ // tm

    def row(i):
        return jnp.where(i < n_prompt_tiles, 0, 1 + (i - n_prompt_tiles) // tiles_per_seq)

    return row


def _mod_kernel(cond_ref, w_ref, b_ref, o_ref):
    a = jax.nn.silu(cond_ref[...])
    w = w_ref[0]
    a_hi, w_hi = a.astype(BF16), w.astype(BF16)
    a_lo = (a - a_hi.astype(F32)).astype(BF16)
    w_lo = (w - w_hi.astype(F32)).astype(BF16)
    dot = functools.partial(jnp.dot, preferred_element_type=F32)
    o_ref[0] = dot(a_hi, w_hi) + dot(a_lo, w_hi) + dot(a_hi, w_lo) + b_ref[0]


def _modulation(cond8, w_mod, b_mod):
    n = w_mod.shape[-1]
    return pl.pallas_call(
        _mod_kernel,
        out_shape=jax.ShapeDtypeStruct((DEPTH, 8, n), F32),
        grid=(DEPTH, n // TN_MOD),
        in_specs=[
            pl.BlockSpec((8, D_MODEL), lambda l, j: (0, 0)),
            pl.BlockSpec((1, D_MODEL, TN_MOD), lambda l, j: (l, 0, j)),
            pl.BlockSpec((1, 1, TN_MOD), lambda l, j: (l, 0, j)),
        ],
        out_specs=pl.BlockSpec((1, 8, TN_MOD), lambda l, j: (l, 0, j)),
        compiler_params=_cparams(("arbitrary", "arbitrary")),
        name="modulation",
    )(cond8, w_mod, b_mod.reshape(DEPTH, 1, n))


def _ffn_kernel(xp_ref, x_ref, xn_ref, mod_ref, g_ref, wup_ref, cw_ref, cb_ref, wd_ref, o_ref,
                h_scr, act_scr, z_scr, *, tile0):
    i = pl.program_id(0) + tile0
    tm = x_ref.shape[0]
    tn = TN_FFN
    m = mod_ref[0]
    sh, sc, gt = m[3:4], m[4:5], m[5:6]
    g = g_ref[...]

    def hmod(x):
        return (_rms(x, g) * (1.0 + sc) + sh).astype(BF16)

    h_scr[0:FFN_HALO] = hmod(xp_ref[...])
    h_scr[FFN_HALO:FFN_HALO + tm] = hmod(x_ref[...])
    h_scr[FFN_HALO + tm:] = hmod(xn_ref[...])
    seq_len = jnp.where(i * tm < N_PROMPT, SEQ, DEC_SEQ)
    pos = (i * tm + lax.broadcasted_iota(jnp.int32, (tm, 1), 0)) & (seq_len - 1)
    eg = BF16_SUBLANES
    bounds = range(0, tm + 1, SEQ)
    cuts = sorted({0, tm} | {b + o for b in bounds for o in (-eg, eg) if 0 < b + o < tm})
    segments = [(r0, r1, any(r0 <= b < r1 or r0 < b <= r1 for b in bounds)) for r0, r1 in zip(cuts[:-1], cuts[1:])]

    def conv(slot, k, col, r0, r1, masked):
        cw = cw_ref[:, col:col + tn]
        zp = z_scr[slot, k, FFN_HALO - 1 + r0:FFN_HALO - 1 + r1, :]
        zc = z_scr[slot, k, FFN_HALO + r0:FFN_HALO + r1, :]
        zn = z_scr[slot, k, FFN_HALO + 1 + r0:FFN_HALO + 1 + r1, :]
        if masked:
            zp = jnp.where(pos[r0:r1] == 0, 0.0, zp)
            zn = jnp.where(pos[r0:r1] == seq_len - 1, 0.0, zn)
        return zp * cw[0:1] + zc * cw[1:2] + zn * cw[2:3] + cb_ref[:, col:col + tn]

    h = h_scr[...]
    for jc in range(D_FF // tn):
        ca, cb = jc * tn, D_FF + jc * tn
        slot = jc % 2
        z_scr[slot, 0] = jnp.dot(h, wup_ref[:, ca:ca + tn], preferred_element_type=F32)
        z_scr[slot, 1] = jnp.dot(h, wup_ref[:, cb:cb + tn], preferred_element_type=F32)
        for r0, r1, masked in segments:
            act_scr[r0:r1, ca:ca + tn] = (jax.nn.silu(conv(slot, 0, ca, r0, r1, masked))
                                          * conv(slot, 1, cb, r0, r1, masked)).astype(BF16)
    o_ref[...] = x_ref[...] + gt * jnp.dot(act_scr[...], wd_ref[...], preferred_element_type=F32)


def _conv_ffn(x, mod_l, g, layer, w_up, conv_w, conv_b, w_down, tok0=0, n_tok=N_TOK):
    tm, halo = TM_FFN, FFN_HALO
    row = _mod_row(tm)
    n_halo_blocks = N_TOK // halo
    tile0 = tok0 // tm
    resident = lambda a: pl.BlockSpec((None,) + a.shape[1:], lambda i: (layer,) + (0,) * (a.ndim - 1),
                                      pipeline_mode=pl.Buffered(1))
    cb3 = conv_b.reshape(DEPTH, 1, -1)
    return pl.pallas_call(
        functools.partial(_ffn_kernel, tile0=tile0),
        out_shape=jax.ShapeDtypeStruct((n_tok, D_MODEL), F32),
        grid=(n_tok // tm,),
        in_specs=[
            pl.BlockSpec((halo, D_MODEL), lambda i: (jnp.maximum((i + tile0) * (tm // halo) - 1, 0), 0)),
            pl.BlockSpec((tm, D_MODEL), lambda i: (i + tile0, 0)),
            pl.BlockSpec((halo, D_MODEL),
                         lambda i: (jnp.minimum((i + tile0 + 1) * (tm // halo), n_halo_blocks - 1), 0)),
            pl.BlockSpec((1, 6, D_MODEL), lambda i: (row(i + tile0), 0, 0)),
            pl.BlockSpec((1, D_MODEL), lambda i: (0, 0)),
            resident(w_up), resident(conv_w), resident(cb3), resident(w_down),
        ],
        out_specs=pl.BlockSpec((tm, D_MODEL), lambda i: (i, 0)),
        scratch_shapes=[pltpu.VMEM((tm + 2 * halo, D_MODEL), BF16), pltpu.VMEM((tm, D_FF), BF16),
                        pltpu.VMEM((2, 2, tm + 2 * halo, TN_FFN), F32)],
        compiler_params=_cparams(("arbitrary",)),
        name="conv_ffn",
    )(x, x, x, mod_l, g.reshape(1, D_MODEL), w_up, conv_w, cb3, w_down)


def _rope(x, cos, sin, half):
    lane = lax.broadcasted_iota(jnp.int32, x.shape, 1)
    first = ((lane // half) & 1) == 0
    partner = jnp.where(first, pltpu.roll(x, LANES - half, 1), pltpu.roll(x, half, 1))
    return x * cos + partner * sin


def _head_norm(xh, g, dim):
    sq = xh * xh
    hi = sq.astype(BF16)
    lo = (sq - hi.astype(F32)).astype(BF16)
    ones = jnp.ones((2 * HEAD_PAD, HEAD_PAD), BF16)
    total = jnp.dot(jnp.concatenate([hi, lo], axis=1), ones, preferred_element_type=F32)
    return xh * lax.rsqrt(total * (1.0 / dim) + EPS) * g


def _mla_kv_heads(ckv_bf, kpe, w_ukv_ref, g_mk, cos_m, sin_m, k_ref, v_ref):
    kv = jnp.dot(ckv_bf, w_ukv_ref[...], preferred_element_type=F32)
    for h in range(MLA_HEADS):
        kh = kv[:, h * HEAD_PAD:(h + 1) * HEAD_PAD] + kpe
        kh = _head_norm(kh, g_mk, MLA_QK)
        if cos_m is not None:
            kh = _rope(kh, cos_m, sin_m, MLA_ROPE // 4)
        k_ref[:, h * HEAD_PAD:(h + 1) * HEAD_PAD] = kh.astype(BF16)
    v_ref[:, 0:MLA_HEADS * MLA_V] = kv[:, MLA_HEADS * HEAD_PAD:].astype(BF16)


def _attn_pre_kernel(x_ref, mod_ref, g1_ref, w_in_ref, g_qa_ref, g_kva_ref, w_uq_ref, w_ukv_ref,
                     g_mq_ref, g_mk_ref, g_gq_ref, g_gk_ref, cos_m_ref, sin_m_ref, cos_g_ref, sin_g_ref,
                     q_ref, k_ref, v_ref, ckv_ref, kpe_ref, gk_ref, gv_ref):
    m = mod_ref[0]
    sh, sc = m[0:1], m[1:2]
    h = (_rms(x_ref[...], g1_ref[...]) * (1.0 + sc) + sh).astype(BF16)
    p = jnp.dot(h, w_in_ref[...], preferred_element_type=F32)
    q_c = _rms(p[:, IN_Q:IN_CKV], g_qa_ref[...]).astype(BF16)
    ckv = _rms(p[:, IN_CKV:IN_KPE], g_kva_ref[...])
    kpe = p[:, IN_KPE:IN_GQ]
    ckv_ref[...] = ckv
    kpe_ref[...] = kpe
    gv = p[:, IN_GV:IN_EXT]
    gv_ref[...] = gv
    v_ref[:, MLA_HEADS * MLA_V:] = gv.astype(BF16)
    qm = jnp.dot(q_c, w_uq_ref[...], preferred_element_type=F32)

    def heads(rope):
        cos_m, sin_m = (cos_m_ref[...], sin_m_ref[...]) if rope else (None, None)
        cos_g, sin_g = (cos_g_ref[...], sin_g_ref[...]) if rope else (None, None)
        g_mq = g_mq_ref[...]
        for hd in range(MLA_HEADS):
            qh = _head_norm(qm[:, hd * HEAD_PAD:(hd + 1) * HEAD_PAD], g_mq, MLA_QK)
            if rope:
                qh = _rope(qh, cos_m, sin_m, MLA_ROPE // 4)
            q_ref[:, hd * HEAD_PAD:(hd + 1) * HEAD_PAD] = (qh * (1.0 / math.sqrt(MLA_QK))).astype(BF16)
        _mla_kv_heads(ckv.astype(BF16), kpe, w_ukv_ref, g_mk_ref[...], cos_m, sin_m, k_ref, v_ref)
        g_gq = g_gq_ref[...]
        for hd in range(GQA_HEADS):
            qh = _head_norm(p[:, IN_GQ + hd * HEAD_PAD:IN_GQ + (hd + 1) * HEAD_PAD], g_gq, GQA_DIM)
            if rope:
                qh = _rope(qh, cos_g, sin_g, GQA_DIM // 4)
            q_ref[:, (MLA_HEADS + hd) * HEAD_PAD:(MLA_HEADS + hd + 1) * HEAD_PAD] = (
                qh * (1.0 / math.sqrt(GQA_DIM))).astype(BF16)
        g_gk = g_gk_ref[...]
        for hd in range(GQA_KV_HEADS):
            kh = _head_norm(p[:, IN_GK + hd * HEAD_PAD:IN_GK + (hd + 1) * HEAD_PAD], g_gk, GQA_DIM)
            gk_ref[:, hd * HEAD_PAD:(hd + 1) * HEAD_PAD] = kh
            if rope:
                kh = _rope(kh, cos_g, sin_g, GQA_DIM // 4)
            k_ref[:, (MLA_HEADS + hd) * HEAD_PAD:(MLA_HEADS + hd + 1) * HEAD_PAD] = kh.astype(BF16)

    is_latent = pl.program_id(0) >= N_PROMPT // x_ref.shape[0]
    pl.when(is_latent)(lambda: heads(True))
    pl.when(jnp.logical_not(is_latent))(lambda: heads(False))


def _attn_pre(x, mod_l, g1, aw, tabs):
    tm = TM_PRE
    row = _mod_row(tm)
    full = lambda a: pl.BlockSpec(a.shape, lambda i: (0,) * a.ndim)
    tok = lambda w: pl.BlockSpec((tm, w), lambda i: (i, 0))
    pos_block = lambda i: (jnp.maximum(i - N_PROMPT // tm, 0) % (DEC_SEQ // tm), 0)
    consts = [g1.reshape(1, D_MODEL), aw["w_in"], aw["g_qa"], aw["g_kva"], aw["w_uq"], aw["w_ukv"],
              aw["g_mq"], aw["g_mk"], aw["g_gq"], aw["g_gk"]]
    return pl.pallas_call(
        _attn_pre_kernel,
        out_shape=[
            jax.ShapeDtypeStruct((N_TOK, N_HEADS * HEAD_PAD), BF16),
            jax.ShapeDtypeStruct((N_TOK, K_ALL), BF16),
            jax.ShapeDtypeStruct((N_TOK, V_ALL), BF16),
            jax.ShapeDtypeStruct((N_TOK, KV_LORA), F32),
            jax.ShapeDtypeStruct((N_TOK, HEAD_PAD), F32),
            jax.ShapeDtypeStruct((N_TOK, GQA_KV_HEADS * HEAD_PAD), F32),
            jax.ShapeDtypeStruct((N_TOK, GQA_KV_HEADS * GQA_DIM), F32),
        ],
        grid=(N_TOK // tm,),
        in_specs=[tok(D_MODEL), pl.BlockSpec((1, 6, D_MODEL), lambda i: (row(i), 0, 0))]
        + [full(a) for a in consts] + [pl.BlockSpec((tm, HEAD_PAD), pos_block)] * 4,
        out_specs=[tok(N_HEADS * HEAD_PAD), tok(K_ALL), tok(V_ALL), tok(KV_LORA), tok(HEAD_PAD),
                   tok(GQA_KV_HEADS * HEAD_PAD), tok(GQA_KV_HEADS * GQA_DIM)],
        compiler_params=_cparams(("arbitrary",)),
        name="attn_pre",
    )(x, mod_l, *consts, *tabs)


def _cache_kv_kernel(ckv_ref, kpe_ref, w_ukv_ref, g_mk_ref, k_ref, v_ref):
    _mla_kv_heads(ckv_ref[...].astype(BF16), kpe_ref[...], w_ukv_ref, g_mk_ref[...], None, None, k_ref, v_ref)


def _cache_kv(ckv, kpe_pad, aw):
    n = ckv.shape[0]
    return pl.pallas_call(
        _cache_kv_kernel,
        out_shape=[jax.ShapeDtypeStruct((n, MLA_HEADS * HEAD_PAD), BF16),
                   jax.ShapeDtypeStruct((n, MLA_HEADS * MLA_V), BF16)],
        name="cache_kv",
    )(ckv, kpe_pad, aw["w_ukv"], aw["g_mk"])


def _attn_kernel(*refs, n_seg):
    q_ref, kv_refs = refs[0], refs[1:1 + 2 * n_seg]
    x_ref, mod_ref, wo_ref, o_ref, oh_scr = refs[1 + 2 * n_seg:]
    for hd in range(N_HEADS):
        if hd < MLA_HEADS:
            kc, vc = hd * HEAD_PAD, hd * MLA_V
        else:
            kvh = (hd - MLA_HEADS) // GQA_REP
            kc, vc = (MLA_HEADS + kvh) * HEAD_PAD, MLA_HEADS * MLA_V + kvh * GQA_DIM
        q = q_ref[:, hd * HEAD_PAD:(hd + 1) * HEAD_PAD]
        ss = [lax.dot_general(q, kv_refs[2 * j][:, kc:kc + HEAD_PAD], (((1,), (1,)), ((), ())),
                              preferred_element_type=F32) for j in range(n_seg)]
        m = functools.reduce(jnp.maximum, [jnp.max(s, axis=-1, keepdims=True) for s in ss])
        ps = [jnp.exp(s - m) for s in ss]
        den = sum(jnp.sum(p, axis=-1, keepdims=True) for p in ps)
        o = sum(jnp.dot(p.astype(BF16), kv_refs[2 * j + 1][:, vc:vc + MLA_V], preferred_element_type=F32)
                for j, p in enumerate(ps))
        oh_scr[:, hd * MLA_V:(hd + 1) * MLA_V] = (o / den).astype(BF16)
    gt = mod_ref[0][2:3]
    o_ref[...] = x_ref[...] + gt * jnp.dot(oh_scr[...], wo_ref[...], preferred_element_type=F32)


def _attention(q, segs, x, mod_l, w_out, tq, n_batch, t_len, tok0, mod_row):
    tiles = t_len // tq
    qmap = lambda bi, qi: (tok0 // tq + bi * tiles + qi, 0)
    kv_specs, kv_args = [], []
    for k, v, rows, blk0 in segs:
        kv_specs += [pl.BlockSpec((rows, K_ALL), lambda bi, qi, blk0=blk0: (blk0 + bi, 0)),
                     pl.BlockSpec((rows, V_ALL), lambda bi, qi, blk0=blk0: (blk0 + bi, 0))]
        kv_args += [k, v]
    return pl.pallas_call(
        functools.partial(_attn_kernel, n_seg=len(segs)),
        out_shape=jax.ShapeDtypeStruct((N_TOK, D_MODEL), F32),
        grid=(n_batch, tiles),
        in_specs=[pl.BlockSpec((tq, N_HEADS * HEAD_PAD), qmap)] + kv_specs + [
            pl.BlockSpec((tq, D_MODEL), qmap),
            pl.BlockSpec((1, 6, D_MODEL), lambda bi, qi: (mod_row(bi), 0, 0)),
            pl.BlockSpec((D_MODEL, D_MODEL), lambda bi, qi: (0, 0)),
        ],
        out_specs=pl.BlockSpec((tq, D_MODEL), qmap),
        scratch_shapes=[pltpu.VMEM((tq, D_MODEL), BF16)],
        input_output_aliases={1 + len(kv_args): 0},
        compiler_params=_cparams(("arbitrary", "arbitrary")),
        name="attention",
    )(q, *kv_args, x, mod_l, w_out)


def _cmul(ar, ai, br, bi):
    return ar * br - ai * bi, ar * bi + ai * br


def _ssm_ops_kernel(a_ref, dt_ref, bt_ref, c_ref, ops_ref, toep_ref, wct_ref, lam_ref,
                    cl_re_scr, cl_im_scr, op_scr, tp_scr, wc_scr):
    L, H, P = SSM_CHUNK, SSM_GROUP, SSM_STATE
    lane = lax.broadcasted_iota(jnp.int32, (H, CHUNK_W), 1)

    a_re, a_im = a_ref[0], a_ref[1]
    dt_all = jnp.exp(dt_ref[...])
    mag = jnp.exp(a_re * dt_all)
    ang = a_im * dt_all
    ab_re, ab_im = mag * jnp.cos(ang), mag * jnp.sin(ang)
    den = a_re * a_re + a_im * a_im
    n_re, n_im = ab_re - 1.0, ab_im
    k_re_all = (n_re * a_re + n_im * a_im) / den
    k_im_all = (n_im * a_re - n_re * a_im) / den
    pw_all = [(jnp.ones_like(ab_re), jnp.zeros_like(ab_im))]
    for _ in range(L):
        pw_all.append(_cmul(pw_all[-1][0], pw_all[-1][1], ab_re, ab_im))

    for g in range(ops_ref.shape[0]):
        lag = []
        for d in range(2):
            r = 2 * g + d
            bt_re, bt_im = bt_ref[g, d, 0], bt_ref[g, d, 1]
            bb_re, bb_im = _cmul(k_re_all[r:r + 1], k_im_all[r:r + 1], bt_re, bt_im)
            c_re, c_im = c_ref[g, d, 0], c_ref[g, d, 1]
            pw = [(p_re[r:r + 1], p_im[r:r + 1]) for p_re, p_im in pw_all]

            for e in range(L + 1):
                pr, pi = pw[e] if d == 0 else pw[L - e]
                cr, ci = _cmul(c_re, c_im, pr, pi)
                cl_re_scr[r,e * H:(e + 1) * H, :] = cr
                cl_im_scr[r,e * H:(e + 1) * H, :] = ci
            lo = 0 if d == 0 else H
            lag.append(lax.dot_general(bb_re, cl_re_scr[r,lo:lo + CHUNK_W, :], (((1,), (1,)), ((), ())),
                                       preferred_element_type=F32, precision=HIGHEST)
                       - lax.dot_general(bb_im, cl_im_scr[r,lo:lo + CHUNK_W, :], (((1,), (1,)), ((), ())),
                                         preferred_element_type=F32, precision=HIGHEST))
            wlo = H if d == 0 else 0
            wc_scr[g, :, d * STATE_W:d * STATE_W + P] = cl_re_scr[r,wlo:wlo + CHUNK_W, :]
            wc_scr[g, :, d * STATE_W + P:(d + 1) * STATE_W] = -cl_im_scr[r,wlo:wlo + CHUNK_W, :]
            for j in range(L):
                pj = pw[L - 1 - j] if d == 0 else pw[j]
                sr, si = _cmul(pj[0], pj[1], bb_re, bb_im)
                op_scr[g, j * H:(j + 1) * H, d * STATE_W:d * STATE_W + P] = sr
                op_scr[g, j * H:(j + 1) * H, d * STATE_W + P:(d + 1) * STATE_W] = si
            pr, pi = pw[L]
            lam_ref[g, 2 * d:2 * d + 1, 0:P] = pr
            lam_ref[g, 2 * d:2 * d + 1, P:STATE_W] = pr
            lam_ref[g, 2 * d + 1:2 * d + 2, 0:P] = -pi
            lam_ref[g, 2 * d + 1:2 * d + 2, P:STATE_W] = pi
        for j in range(L):
            fwd = lag[0] if j == 0 else jnp.where(lane >= H * j, pltpu.roll(lag[0], H * j, 1), 0.0)
            sft = (L - 1 - j) * H
            bwd = lag[1] if sft == 0 else jnp.where(lane < H * (j + 1), pltpu.roll(lag[1], CHUNK_W - sft, 1), 0.0)
            tp_scr[g, j * H:(j + 1) * H, :] = fwd + bwd
        ops_ref[g] = op_scr[g].astype(BF16)
        toep_ref[g] = tp_scr[g].astype(BF16)
        wct_ref[g] = wc_scr[g].astype(BF16)


def _ssm_ops(a, dt, bt, c):
    gb = GB_SSM
    G = SSM_GROUPS
    blk = lambda shape: pl.BlockSpec((gb,) + shape, lambda i: (i,) + (0,) * len(shape))
    return pl.pallas_call(
        _ssm_ops_kernel,
        out_shape=[jax.ShapeDtypeStruct((G, CHUNK_W, 2 * STATE_W), BF16),
                   jax.ShapeDtypeStruct((G, CHUNK_W, CHUNK_W), BF16),
                   jax.ShapeDtypeStruct((G, CHUNK_W, 2 * STATE_W), BF16),
                   jax.ShapeDtypeStruct((G, 4, STATE_W), F32)],
        grid=(G // gb,),
        in_specs=[pl.BlockSpec((2, 2 * gb, SSM_STATE), lambda i: (0, i, 0)),
                  pl.BlockSpec((2 * gb, SSM_STATE), lambda i: (i, 0)),
                  blk((2, 2, SSM_GROUP, SSM_STATE)), blk((2, 2, SSM_GROUP, SSM_STATE))],
        out_specs=[blk((CHUNK_W, 2 * STATE_W)), blk((CHUNK_W, CHUNK_W)), blk((CHUNK_W, 2 * STATE_W)),
                   blk((4, STATE_W))],
        scratch_shapes=[pltpu.VMEM((2 * gb, (SSM_CHUNK + 1) * SSM_GROUP, SSM_STATE), F32),
                        pltpu.VMEM((2 * gb, (SSM_CHUNK + 1) * SSM_GROUP, SSM_STATE), F32),
                        pltpu.VMEM((gb, CHUNK_W, 2 * STATE_W), F32),
                        pltpu.VMEM((gb, CHUNK_W, CHUNK_W), F32),
                        pltpu.VMEM((gb, CHUNK_W, 2 * STATE_W), F32)],
        compiler_params=_cparams(("arbitrary",)),
        name="ssm_ops",
    )(a, dt, bt, c)


GRANULES = LANES // SSM_GROUP


def _granule_transpose(arrs):
    a = list(arrs)
    lane = lax.broadcasted_iota(jnp.int32, a[0].shape, 1)
    for d in (4, 2, 1):
        upper = (lane & (d * SSM_GROUP)) != 0
        for s in range(GRANULES):
            if s & d:
                continue
            lo, hi = a[s], a[s + d]
            a[s] = jnp.where(upper, pltpu.roll(hi, d * SSM_GROUP, 1), lo)
            a[s + d] = jnp.where(upper, hi, pltpu.roll(lo, LANES - d * SSM_GROUP, 1))
    return a


def _step_select(tm):
    mb = tm // SSM_CHUNK
    idx = (np.arange(mb)[None, :] * SSM_CHUNK + np.arange(SSM_CHUNK)[:, None]).reshape(-1)
    return jnp.asarray(np.eye(tm, dtype=np.float32)[idx], BF16)


def _ssm_in_kernel(x_ref, mod_ref, g_ref, sel_ref, u_ref):
    m = mod_ref[0]
    h = (_rms(x_ref[...], g_ref[...]) * (1.0 + m[1:2]) + m[0:1]).astype(BF16)
    by_step = jnp.dot(sel_ref[...], h, preferred_element_type=F32)
    mb = x_ref.shape[0] // SSM_CHUNK
    for g8 in range(SSM_GROUPS // GRANULES):
        for half in range(SSM_CHUNK // GRANULES):
            a = [by_step[(GRANULES * half + s) * mb:(GRANULES * half + s + 1) * mb, g8 * LANES:(g8 + 1) * LANES]
                 for s in range(GRANULES)]
            b = _granule_transpose(a)
            for g in range(GRANULES):
                u_ref[g8 * GRANULES + g, :, half * LANES:(half + 1) * LANES] = b[g].astype(BF16)


def _ssm_in(x, mod_l, g1):
    tm = TM_SSM_IN
    row = _mod_row(tm)
    return pl.pallas_call(
        _ssm_in_kernel,
        out_shape=jax.ShapeDtypeStruct((SSM_GROUPS, N_CHUNK_ROWS, CHUNK_W), BF16),
        grid=(N_TOK // tm,),
        in_specs=[pl.BlockSpec((tm, D_MODEL), lambda i: (i, 0)),
                  pl.BlockSpec((1, 6, D_MODEL), lambda i: (row(i), 0, 0)),
                  pl.BlockSpec((1, D_MODEL), lambda i: (0, 0)),
                  pl.BlockSpec((tm, tm), lambda i: (0, 0))],
        out_specs=pl.BlockSpec((SSM_GROUPS, tm // SSM_CHUNK, CHUNK_W), lambda i: (0, i, 0)),
        compiler_params=_cparams(("arbitrary",)),
        name="ssm_in",
    )(x, mod_l, g1.reshape(1, D_MODEL), _step_select(tm))


def _step_major_perm():
    n = SEQ // SSM_CHUNK
    assert n == BATCH
    idx = np.arange(N_CHUNK_PROMPT).reshape(BATCH, n).T.reshape(-1)
    return jnp.asarray(np.eye(N_CHUNK_PROMPT, dtype=np.float32)[idx], BF16)


def _ssm_ends_kernel(u_ref, ops_ref, perm_ref, *s_refs):
    n_p = N_CHUNK_PROMPT
    for g in range(u_ref.shape[0]):
        u_ctx = jnp.dot(perm_ref[...], u_ref[g, 0:n_p], preferred_element_type=F32).astype(BF16)
        s_ctx = jnp.dot(u_ctx, ops_ref[g], preferred_element_type=F32)
        s_lat = jnp.dot(u_ref[g, n_p:], ops_ref[g], preferred_element_type=F32)
        for k, s_ref in enumerate(s_refs):
            s_ref[g, 0:n_p] = s_ctx[:, k * STATE_W:(k + 1) * STATE_W]
            s_ref[g, n_p:] = s_lat[:, k * STATE_W:(k + 1) * STATE_W]


def _ssm_ends(u, ops, perm):
    gb = GB_SSM
    G = SSM_GROUPS
    return pl.pallas_call(
        _ssm_ends_kernel,
        out_shape=[jax.ShapeDtypeStruct((G, N_CHUNK_ROWS, STATE_W), F32)] * 2,
        grid=(G // gb,),
        in_specs=[pl.BlockSpec((gb, N_CHUNK_ROWS, CHUNK_W), lambda i: (i, 0, 0)),
                  pl.BlockSpec((gb, CHUNK_W, 2 * STATE_W), lambda i: (i, 0, 0)),
                  pl.BlockSpec(perm.shape, lambda i: (0, 0))],
        out_specs=[pl.BlockSpec((gb, N_CHUNK_ROWS, STATE_W), lambda i: (i, 0, 0))] * 2,
        compiler_params=_cparams(("arbitrary",)),
        name="ssm_chunk_ends",
    )(u, ops, perm)


def _ssm_carry_kernel(sf_ref, sb_ref, lam_ref, h0_ref, xf_ref, xb_ref, fin_ref):
    n_g = sf_ref.shape[0] // N_CHUNK_ROWS
    n_p, n_s = SEQ // SSM_CHUNK, DEC_SEQ // SSM_CHUNK

    def swap(t):
        return pltpu.roll(t, SSM_STATE, 1)

    def step(x, lam, rf, rb):
        xf, xfs, xb, xbs = x
        a1f, a2f, a1b, a2b = lam
        xf_ref[rf, :] = xf
        xb_ref[rb, :] = xb
        sf, sb = sf_ref[rf, :], sb_ref[rb, :]
        return [a1f * xf + a2f * xfs + sf, a1f * xfs - a2f * xf + swap(sf),
                a1b * xb + a2b * xbs + sb, a1b * xbs - a2b * xb + swap(sb)]

    zero = jnp.zeros((BATCH, STATE_W), F32)
    for g0 in range(0, n_g, GB_CARRY):
        groups = range(g0, g0 + GB_CARRY)
        lam = [[jnp.broadcast_to(lam_ref[4 * g + k:4 * g + k + 1, :], (BATCH, STATE_W)) for k in range(4)]
               for g in groups]

        def body_p(c, carry, groups=groups, lam=lam):
            out = []
            for j, g in enumerate(groups):
                rf = pl.ds(pl.multiple_of(g * N_CHUNK_ROWS + c * BATCH, BATCH), BATCH)
                rb = pl.ds(pl.multiple_of(g * N_CHUNK_ROWS + (n_p - 1 - c) * BATCH, BATCH), BATCH)
                out += step(carry[4 * j:4 * j + 4], lam[j], rf, rb)
            return tuple(out)

        fin = lax.fori_loop(0, n_p, body_p, (zero,) * (4 * GB_CARRY), unroll=4)
        for j, g in enumerate(groups):
            fin_ref[g, 0] = fin[4 * j]
            fin_ref[g, 1] = fin[4 * j + 2]

    packs = [(g0, b) for g0 in range(0, n_g, 8) for b in range(DEC_BATCH)]
    lam = {g0: [lam_ref[pl.ds(4 * g0 + k, 8, stride=4), :] for k in range(4)] for g0 in range(0, n_g, 8)}

    def body_s(c, carry):
        out = []
        for j, (g0, b) in enumerate(packs):
            base = g0 * N_CHUNK_ROWS + N_CHUNK_PROMPT + b * n_s
            rf = pl.ds(base + c, 8, stride=N_CHUNK_ROWS)
            rb = pl.ds(base + (n_s - 1 - c), 8, stride=N_CHUNK_ROWS)
            out += step(carry[4 * j:4 * j + 4], lam[g0], rf, rb)
        return tuple(out)

    init = []
    for g0, b in packs:
        h0f = h0_ref[pl.ds(4 * g0 + b, 8, stride=4), :]
        h0b = h0_ref[pl.ds(4 * g0 + DEC_BATCH + b, 8, stride=4), :]
        init += [h0f, swap(h0f), h0b, swap(h0b)]
    lax.fori_loop(0, n_s, body_s, tuple(init), unroll=8)


def _ssm_carry(sf, sb, lam, h0):
    gb = GB_CARRY_BLOCK
    G = SSM_GROUPS
    st = pl.BlockSpec((gb * N_CHUNK_ROWS, STATE_W), lambda i: (i, 0))
    small = pl.BlockSpec((gb * 4, STATE_W), lambda i: (i, 0))
    return pl.pallas_call(
        _ssm_carry_kernel,
        out_shape=[jax.ShapeDtypeStruct(sf.shape, F32)] * 2 + [jax.ShapeDtypeStruct((G, 2, BATCH, STATE_W), F32)],
        grid=(G // gb,),
        in_specs=[st, st, small, small],
        out_specs=[st, st, pl.BlockSpec((gb, 2, BATCH, STATE_W), lambda i: (i, 0, 0, 0))],
        compiler_params=_cparams(("arbitrary",)),
        name="ssm_carry",
    )(sf, sb, lam, h0)


def _ssm_out_kernel(u_ref, toep_ref, xf_ref, xb_ref, wct_ref, perm_ref, y_ref, y_scr):
    n_p = N_CHUNK_PROMPT
    for g in range(GRANULES):
        y = jnp.dot(u_ref[g], toep_ref[g], preferred_element_type=F32)
        for d, x_ref in enumerate((xf_ref, xb_ref)):
            x_ctx = jnp.dot(perm_ref[...], x_ref[g, 0:n_p].astype(BF16), preferred_element_type=F32).astype(BF16)
            xin = jnp.concatenate([x_ctx, x_ref[g, n_p:].astype(BF16)], axis=0)
            w = wct_ref[g, :, d * STATE_W:(d + 1) * STATE_W]
            y += lax.dot_general(xin, w, (((1,), (1,)), ((), ())), preferred_element_type=F32)
        y_scr[g] = y
    rb = OUT_ROW_BLOCK
    for r in range(N_CHUNK_ROWS // rb):
        for half in range(SSM_CHUNK // GRANULES):
            a = [y_scr[g, r * rb:(r + 1) * rb, half * LANES:(half + 1) * LANES] for g in range(GRANULES)]
            b = _granule_transpose(a)
            for s in range(GRANULES):
                y_ref[pl.ds(r * rb * SSM_CHUNK + GRANULES * half + s, rb, stride=SSM_CHUNK), :] = b[s]


def _ssm_out(u, toep, xf, xb, wct, perm):
    gb = GRANULES
    G = SSM_GROUPS
    return pl.pallas_call(
        _ssm_out_kernel,
        out_shape=jax.ShapeDtypeStruct((N_TOK, D_MODEL), F32),
        grid=(G // gb,),
        in_specs=[pl.BlockSpec((gb, N_CHUNK_ROWS, CHUNK_W), lambda i: (i, 0, 0)),
                  pl.BlockSpec((gb, CHUNK_W, CHUNK_W), lambda i: (i, 0, 0)),
                  pl.BlockSpec((gb, N_CHUNK_ROWS, STATE_W), lambda i: (i, 0, 0)),
                  pl.BlockSpec((gb, N_CHUNK_ROWS, STATE_W), lambda i: (i, 0, 0)),
                  pl.BlockSpec((gb, CHUNK_W, 2 * STATE_W), lambda i: (i, 0, 0)),
                  pl.BlockSpec(perm.shape, lambda i: (0, 0))],
        out_specs=pl.BlockSpec((N_TOK, LANES), lambda i: (0, i)),
        scratch_shapes=[pltpu.VMEM((gb, N_CHUNK_ROWS, CHUNK_W), F32)],
        compiler_params=_cparams(("arbitrary",)),
        name="ssm_chunk_out",
    )(u, toep, xf, xb, wct, perm)


def _ssm_post_kernel(x_ref, mod_ref, g_ref, y_ref, d_ref, w_ref, b_ref, o_ref):
    m = mod_ref[0]
    x = x_ref[...]
    h = _rms(x, g_ref[...]) * (1.0 + m[1:2]) + m[0:1]
    y = h * d_ref[...] + y_ref[...]
    gl = jax.nn.gelu(y).astype(BF16)
    z = jnp.dot(gl, w_ref[...], preferred_element_type=F32) + b_ref[...]
    o_ref[...] = x + m[2:3] * (z[:, :D_MODEL] * jax.nn.sigmoid(z[:, D_MODEL:]))


def _ssm_post(x, mod_l, g1, y, d_skip, w_glu, b_glu):
    tm = TM_SSM
    row = _mod_row(tm)
    full = lambda a: pl.BlockSpec(a.shape, lambda i: (0,) * a.ndim)
    d2, b2 = d_skip.reshape(1, D_MODEL), b_glu.reshape(1, 2 * D_MODEL)
    g2 = g1.reshape(1, D_MODEL)
    return pl.pallas_call(
        _ssm_post_kernel,
        out_shape=jax.ShapeDtypeStruct((N_TOK, D_MODEL), F32),
        grid=(N_TOK // tm,),
        in_specs=[pl.BlockSpec((tm, D_MODEL), lambda i: (i, 0)),
                  pl.BlockSpec((1, 6, D_MODEL), lambda i: (row(i), 0, 0)),
                  full(g2), pl.BlockSpec((tm, D_MODEL), lambda i: (i, 0)), full(d2), full(w_glu), full(b2)],
        out_specs=pl.BlockSpec((tm, D_MODEL), lambda i: (i, 0)),
        compiler_params=_cparams(("arbitrary",)),
        name="ssm_post",
    )(x, mod_l, g2, y, d2, w_glu, b2)


def _rope_tables():
    rows = DEC_SEQ // GRID_W
    t = np.arange(DEC_SEQ)
    row, col = (t // GRID_W).astype(np.float32), (t % GRID_W).astype(np.float32)

    def table(rot_dim, lane0):
        n_freq = rot_dim // 4
        inv_freq = jnp.asarray(ROPE_THETA, F32) ** (-jnp.arange(n_freq, dtype=F32) / n_freq)
        ang_row = jnp.asarray(row)[:, None] * inv_freq
        ang_col = jnp.asarray(col)[:, None] * inv_freq
        ang = jnp.concatenate([ang_row, ang_row, ang_col, ang_col], axis=1)
        sign = jnp.tile(jnp.concatenate([-jnp.ones(n_freq, F32), jnp.ones(n_freq, F32)]), 2)
        cos = jnp.ones((DEC_SEQ, HEAD_PAD), F32).at[:, lane0:lane0 + rot_dim].set(jnp.cos(ang))
        sin = jnp.zeros((DEC_SEQ, HEAD_PAD), F32).at[:, lane0:lane0 + rot_dim].set(jnp.sin(ang) * sign)
        return cos, sin

    assert rows * GRID_W == DEC_SEQ
    return table(MLA_ROPE, MLA_NOPE) + table(GQA_DIM, 0)


def _pad_heads(w, n_heads, dim):
    lead = w.shape[:-1]
    w = w.reshape(lead + (n_heads, dim))
    w = jnp.pad(w, [(0, 0)] * len(lead) + [(0, 0), (0, HEAD_PAD - dim)])
    return w.reshape(lead + (n_heads * HEAD_PAD,))


def _attn_weights(w_in, g_qa, g_kva, w_uq, w_ukv, g_mq, g_mk, g_gq, g_gk, w_out):
    o1, o2, o3 = Q_LORA, Q_LORA + KV_LORA, Q_LORA + KV_LORA + MLA_ROPE
    o4 = o3 + GQA_HEADS * GQA_DIM
    o5 = o4 + GQA_KV_HEADS * GQA_DIM
    kpe = jnp.pad(w_in[:, o2:o3], ((0, 0), (MLA_NOPE, HEAD_PAD - MLA_QK)))
    w_in_ext = jnp.concatenate([w_in[:, :o2], kpe, _pad_heads(w_in[:, o3:o4], GQA_HEADS, GQA_DIM),
                                _pad_heads(w_in[:, o4:o5], GQA_KV_HEADS, GQA_DIM), w_in[:, o5:]], axis=1)
    ukv = w_ukv.reshape(KV_LORA, MLA_HEADS, MLA_NOPE + MLA_V)
    w_ukv_perm = jnp.concatenate([_pad_heads(ukv[:, :, :MLA_NOPE].reshape(KV_LORA, -1), MLA_HEADS, MLA_NOPE),
                                  ukv[:, :, MLA_NOPE:].reshape(KV_LORA, -1)], axis=1)
    pad_g = lambda g: jnp.pad(g, (0, HEAD_PAD - g.shape[0])).reshape(1, HEAD_PAD)
    return dict(w_in=w_in_ext.astype(BF16), g_qa=g_qa.reshape(1, -1), g_kva=g_kva.reshape(1, -1),
                w_uq=_pad_heads(w_uq, MLA_HEADS, MLA_QK).astype(BF16), w_ukv=w_ukv_perm.astype(BF16),
                g_mq=pad_g(g_mq), g_mk=pad_g(g_mk), g_gq=pad_g(g_gq), g_gk=pad_g(g_gk),
                w_out=w_out.astype(BF16))


def _attn_layer(x, mod_l, g1, aw, tabs, cache):
    q, k, v, ckv, kpe, gk, gv = _attn_pre(x, mod_l, g1, aw, tabs)
    c_ckv, c_krope, c_gk, c_gv = cache
    n_c = DEC_BATCH * PAST_LEN
    kpe_c = jnp.pad(c_krope.reshape(n_c, MLA_ROPE), ((0, 0), (MLA_NOPE, HEAD_PAD - MLA_QK)))
    km_c, vm_c = _cache_kv(c_ckv.reshape(n_c, KV_LORA), kpe_c, aw)
    k_c = jnp.concatenate([km_c, _pad_heads(c_gk.reshape(n_c, -1), GQA_KV_HEADS, GQA_DIM).astype(BF16)], axis=1)
    v_c = jnp.concatenate([vm_c, c_gv.reshape(n_c, -1).astype(BF16)], axis=1)
    x = _attention(q, [(k, v, SEQ, 0)], x, mod_l, aw["w_out"], SEQ, BATCH, SEQ, 0, lambda b: 0)
    x = _attention(q, [(k_c, v_c, PAST_LEN, 0), (k, v, DEC_SEQ, N_PROMPT // DEC_SEQ)], x, mod_l, aw["w_out"],
                   TQ_SAMPLE, DEC_BATCH, DEC_SEQ, N_PROMPT, lambda b: 1 + b)
    new = (ckv[:N_PROMPT].reshape(BATCH, SEQ, KV_LORA),
           kpe[:N_PROMPT, MLA_NOPE:MLA_QK].reshape(BATCH, SEQ, MLA_ROPE),
           gk[:N_PROMPT].reshape(BATCH, SEQ, GQA_KV_HEADS, HEAD_PAD)[..., :GQA_DIM],
           gv[:N_PROMPT].reshape(BATCH, SEQ, GQA_KV_HEADS, GQA_DIM))
    return x, new


def _ssm_layer(x, mod_l, g1, a_re, a_im, log_dt, b_re, b_im, c_re, c_im, d_skip, w_glu, b_glu, state0):
    G, P, H = SSM_GROUPS, SSM_STATE, SSM_GROUP
    a = jnp.stack([a_re, a_im], axis=0).transpose(0, 2, 1, 3).reshape(2, G * 2, P)
    dt = jnp.broadcast_to(log_dt.transpose(1, 0)[:, :, None], (G, 2, P)).reshape(G * 2, P)
    bt = jnp.stack([b_re, b_im], axis=2).transpose(1, 0, 2, 4, 3)
    c = jnp.stack([c_re, c_im], axis=2).transpose(1, 0, 2, 3, 4)
    ops, toep, wct, lam = _ssm_ops(a, dt, bt, c)

    u = _ssm_in(x, mod_l, g1)
    perm = _step_major_perm()
    sf, sb = _ssm_ends(u, ops, perm)
    h0 = state0.transpose(2, 1, 0, 4, 3).reshape(G * 2 * DEC_BATCH, STATE_W)
    flat = lambda s: s.reshape(G * N_CHUNK_ROWS, STATE_W)
    xf, xb, fin = _ssm_carry(flat(sf), flat(sb), lam.reshape(G * 4, STATE_W), h0)
    y = _ssm_out(u, toep, xf.reshape(sf.shape), xb.reshape(sb.shape), wct, perm)
    x = _ssm_post(x, mod_l, g1, y, d_skip, w_glu.astype(BF16), b_glu)
    fin = fin.reshape(G, 2, BATCH, 2, P).transpose(2, 1, 0, 4, 3)
    return x, fin


def kernel(x_prompt, x_sample, c, cache_mla_ckv, cache_mla_krope, cache_gqa_k, cache_gqa_v, state_ssm, c_ctx,
           norm1_g, norm2_g, w_mod, b_mod,
           attn_w_in, attn_qa_norm_g, attn_kva_norm_g, attn_w_uq, attn_w_ukv,
           attn_mla_q_norm_g, attn_mla_k_norm_g, attn_gqa_q_norm_g, attn_gqa_k_norm_g, attn_w_out,
           ssm_a_re, ssm_a_im, ssm_log_dt, ssm_b_re, ssm_b_im, ssm_c_re, ssm_c_im, ssm_d, ssm_w_glu, ssm_b_glu,
           ffn_w_up, ffn_conv_w, ffn_conv_b, ffn_w_down):
    x = jnp.concatenate([x_prompt.reshape(N_PROMPT, D_MODEL), x_sample.reshape(N_SAMPLE, D_MODEL)], axis=0)
    cond8 = jnp.concatenate([c_ctx[None, :], c, jnp.zeros((8 - 1 - DEC_BATCH, D_MODEL), F32)], axis=0)
    mod = _modulation(cond8, w_mod, b_mod).reshape(DEPTH, 8, 6, D_MODEL)
    tabs = _rope_tables()

    w_up_bf, w_down_bf = ffn_w_up.astype(BF16), ffn_w_down.astype(BF16)
    new_attn, new_ssm = [], []
    for i in range(DEPTH):
        j = i // 2
        if i % 2 == 0:
            aw = _attn_weights(attn_w_in[j], attn_qa_norm_g[j], attn_kva_norm_g[j], attn_w_uq[j], attn_w_ukv[j],
                               attn_mla_q_norm_g[j], attn_mla_k_norm_g[j], attn_gqa_q_norm_g[j],
                               attn_gqa_k_norm_g[j], attn_w_out[j])
            cache = (cache_mla_ckv[:, j], cache_mla_krope[:, j], cache_gqa_k[:, j], cache_gqa_v[:, j])
            x, new = _attn_layer(x, mod[i], norm1_g[i], aw, tabs, cache)
            new_attn.append(new)
        else:
            x, fin = _ssm_layer(x, mod[i], norm1_g[i], ssm_a_re[j], ssm_a_im[j], ssm_log_dt[j], ssm_b_re[j],
                                ssm_b_im[j], ssm_c_re[j], ssm_c_im[j], ssm_d[j], ssm_w_glu[j], ssm_b_glu[j],
                                state_ssm[:, j])
            new_ssm.append(fin)
        ffn = functools.partial(_conv_ffn, x, mod[i], norm2_g[i], i, w_up_bf, ffn_conv_w, ffn_conv_b, w_down_bf)
        if i < DEPTH - 1:
            x = ffn()
        else:
            y_prompt, y_sample = ffn(0, N_PROMPT), ffn(N_PROMPT, N_SAMPLE)

    outs = [jnp.stack([n[k] for n in new_attn], axis=1) for k in range(4)]
    return (y_prompt.reshape(BATCH, SEQ, D_MODEL), y_sample.reshape(DEC_BATCH, DEC_SEQ, D_MODEL),
            outs[0], outs[1], outs[2], outs[3], jnp.stack(new_ssm, axis=1))
```

```python
import functools
import math

import jax
import jax.numpy as jnp
import numpy as np
from jax import lax
from jax.experimental import pallas as pl
from jax.experimental.pallas import tpu as pltpu

F32 = jnp.float32
BF16 = jnp.bfloat16
HIGHEST = lax.Precision.HIGHEST

LANES = 128
BF16_SUBLANES = 16
VMEM_LIMIT_BYTES = 56 * 1024 * 1024

D_MODEL = 1024
BATCH = 16
SEQ = 256
DEPTH = 4
DEC_BATCH = 2
DEC_SEQ = 2048
PAST_LEN = 256
GRID_W = 64
MLA_HEADS = 8
Q_LORA = 384
KV_LORA = 256
MLA_NOPE = 64
MLA_ROPE = 32
MLA_QK = MLA_NOPE + MLA_ROPE
MLA_V = 64
GQA_HEADS = 8
GQA_KV_HEADS = 2
GQA_DIM = 64
GQA_REP = GQA_HEADS // GQA_KV_HEADS
ROPE_THETA = 10000.0
SSM_GROUP = 16
SSM_GROUPS = D_MODEL // SSM_GROUP
SSM_STATE = 64
D_FF = 2816
EPS = 1e-6

N_PROMPT = BATCH * SEQ
N_SAMPLE = DEC_BATCH * DEC_SEQ
N_TOK = N_PROMPT + N_SAMPLE
N_HEADS = MLA_HEADS + GQA_HEADS
HEAD_PAD = LANES

IN_Q = 0
IN_CKV = IN_Q + Q_LORA
IN_KPE = IN_CKV + KV_LORA
IN_GQ = IN_KPE + HEAD_PAD
IN_GK = IN_GQ + GQA_HEADS * HEAD_PAD
IN_GV = IN_GK + GQA_KV_HEADS * HEAD_PAD
IN_EXT = IN_GV + GQA_KV_HEADS * GQA_DIM

K_ALL = (MLA_HEADS + GQA_KV_HEADS) * HEAD_PAD
V_ALL = MLA_HEADS * MLA_V + GQA_KV_HEADS * GQA_DIM

SSM_CHUNK = 16
CHUNK_W = SSM_CHUNK * SSM_GROUP
N_CHUNK_ROWS = N_TOK // SSM_CHUNK
N_CHUNK_PROMPT = N_PROMPT // SSM_CHUNK
STATE_W = 2 * SSM_STATE

TM_FFN = 512
TN_FFN = 256
FFN_HALO = BF16_SUBLANES
TM_PRE = 512
TQ_SAMPLE = 256
TM_SSM = 1024
TM_SSM_IN = 512
OUT_ROW_BLOCK = 32
GB_CARRY = 4
GB_CARRY_BLOCK = 16
TN_MOD = 1536
GB_SSM = 8


def _cparams(sem):
    return pltpu.CompilerParams(dimension_semantics=sem, vmem_limit_bytes=VMEM_LIMIT_BYTES)


def _rms(x, g):
    return x * lax.rsqrt(jnp.mean(x * x, axis=-1, keepdims=True) + EPS) * g


def _mod_row(tm):
    n_prompt_tiles = N_PROMPT // tm
    tiles_per_seq = DEC_SEQ // tm

    def row(i):
        return jnp.where(i < n_prompt_tiles, 0, 1 + (i - n_prompt_tiles) // tiles_per_seq)

    return row


def _mod_kernel(cond_ref, w_ref, b_ref, o_ref):
    a = jax.nn.silu(cond_ref[...])
    w = w_ref[0]
    a_hi, w_hi = a.astype(BF16), w.astype(BF16)
    a_lo = (a - a_hi.astype(F32)).astype(BF16)
    w_lo = (w - w_hi.astype(F32)).astype(BF16)
    dot = functools.partial(jnp.dot, preferred_element_type=F32)
    o_ref[0] = dot(a_hi, w_hi) + dot(a_lo, w_hi) + dot(a_hi, w_lo) + b_ref[0]


def _modulation(cond8, w_mod, b_mod):
    n = w_mod.shape[-1]
    return pl.pallas_call(
        _mod_kernel,
        out_shape=jax.ShapeDtypeStruct((DEPTH, 8, n), F32),
        grid=(DEPTH, n // TN_MOD),
        in_specs=[
            pl.BlockSpec((8, D_MODEL), lambda l, j: (0, 0)),
            pl.BlockSpec((1, D_MODEL, TN_MOD), lambda l, j: (l, 0, j)),
            pl.BlockSpec((1, 1, TN_MOD), lambda l, j: (l, 0, j)),
        ],
        out_specs=pl.BlockSpec((1, 8, TN_MOD), lambda l, j: (l, 0, j)),
        compiler_params=_cparams(("arbitrary", "arbitrary")),
        name="modulation",
    )(cond8, w_mod, b_mod.reshape(DEPTH, 1, n))


def _ffn_kernel(xp_ref, x_ref, xn_ref, mod_ref, g_ref, wup_ref, cw_ref, cb_ref, wd_ref, o_ref,
                h_scr, act_scr, z_scr, *, tile0):
    i = pl.program_id(0) + tile0
    tm = x_ref.shape[0]
    tn = TN_FFN
    m = mod_ref[0]
    sh, sc, gt = m[3:4], m[4:5], m[5:6]
    g = g_ref[...]

    def hmod(x):
        return (_rms(x, g) * (1.0 + sc) + sh).astype(BF16)

    h_scr[0:FFN_HALO] = hmod(xp_ref[...])
    h_scr[FFN_HALO:FFN_HALO + tm] = hmod(x_ref[...])
    h_scr[FFN_HALO + tm:] = hmod(xn_ref[...])
    seq_len = jnp.where(i * tm < N_PROMPT, SEQ, DEC_SEQ)
    pos = (i * tm + lax.broadcasted_iota(jnp.int32, (tm, 1), 0)) & (seq_len - 1)
    eg = BF16_SUBLANES
    bounds = range(0, tm + 1, SEQ)
    cuts = sorted({0, tm} | {b + o for b in bounds for o in (-eg, eg) if 0 < b + o < tm})
    segments = [(r0, r1, any(r0 <= b < r1 or r0 < b <= r1 for b in bounds)) for r0, r1 in zip(cuts[:-1], cuts[1:])]

    def conv(slot, k, col, r0, r1, masked):
        cw = cw_ref[:, col:col + tn]
        zp = z_scr[slot, k, FFN_HALO - 1 + r0:FFN_HALO - 1 + r1, :]
        zc = z_scr[slot, k, FFN_HALO + r0:FFN_HALO + r1, :]
        zn = z_scr[slot, k, FFN_HALO + 1 + r0:FFN_HALO + 1 + r1, :]
        if masked:
            zp = jnp.where(pos[r0:r1] == 0, 0.0, zp)
            zn = jnp.where(pos[r0:r1] == seq_len - 1, 0.0, zn)
        return zp * cw[0:1] + zc * cw[1:2] + zn * cw[2:3] + cb_ref[:, col:col + tn]

    h = h_scr[...]
    for jc in range(D_FF // tn):
        ca, cb = jc * tn, D_FF + jc * tn
        slot = jc % 2
        z_scr[slot, 0] = jnp.dot(h, wup_ref[:, ca:ca + tn], preferred_element_type=F32)
        z_scr[slot, 1] = jnp.dot(h, wup_ref[:, cb:cb + tn], preferred_element_type=F32)
        for r0, r1, masked in segments:
            act_scr[r0:r1, ca:ca + tn] = (jax.nn.silu(conv(slot, 0, ca, r0, r1, masked))
                                          * conv(slot, 1, cb, r0, r1, masked)).astype(BF16)
    o_ref[...] = x_ref[...] + gt * jnp.dot(act_scr[...], wd_ref[...], preferred_element_type=F32)


def _conv_ffn(x, mod_l, g, layer, w_up, conv_w, conv_b, w_down, tok0=0, n_tok=N_TOK):
    tm, halo = TM_FFN, FFN_HALO
    row = _mod_row(tm)
    n_halo_blocks = N_TOK // halo
    tile0 = tok0 // tm
    resident = lambda a: pl.BlockSpec((None,) + a.shape[1:], lambda i: (layer,) + (0,) * (a.ndim - 1),
                                      pipeline_mode=pl.Buffered(1))
    cb3 = conv_b.reshape(DEPTH, 1, -1)
    return pl.pallas_call(
        functools.partial(_ffn_kernel, tile0=tile0),
        out_shape=jax.ShapeDtypeStruct((n_tok, D_MODEL), F32),
        grid=(n_tok // tm,),
        in_specs=[
            pl.BlockSpec((halo, D_MODEL), lambda i: (jnp.maximum((i + tile0) * (tm // halo) - 1, 0), 0)),
            pl.BlockSpec((tm, D_MODEL), lambda i: (i + tile0, 0)),
            pl.BlockSpec((halo, D_MODEL),
                         lambda i: (jnp.minimum((i + tile0 + 1) * (tm // halo), n_halo_blocks - 1), 0)),
            pl.BlockSpec((1, 6, D_MODEL), lambda i: (row(i + tile0), 0, 0)),
            pl.BlockSpec((1, D_MODEL), lambda i: (0, 0)),
            resident(w_up), resident(conv_w), resident(cb3), resident(w_down),
        ],
        out_specs=pl.BlockSpec((tm, D_MODEL), lambda i: (i, 0)),
        scratch_shapes=[pltpu.VMEM((tm + 2 * halo, D_MODEL), BF16), pltpu.VMEM((tm, D_FF), BF16),
                        pltpu.VMEM((2, 2, tm + 2 * halo, TN_FFN), F32)],
        compiler_params=_cparams(("arbitrary",)),
        name="conv_ffn",
    )(x, x, x, mod_l, g.reshape(1, D_MODEL), w_up, conv_w, cb3, w_down)


def _rope(x, cos, sin, half):
    lane = lax.broadcasted_iota(jnp.int32, x.shape, 1)
    first = ((lane // half) & 1) == 0
    partner = jnp.where(first, pltpu.roll(x, LANES - half, 1), pltpu.roll(x, half, 1))
    return x * cos + partner * sin


def _head_norm(xh, g, dim):
    sq = xh * xh
    hi = sq.astype(BF16)
    lo = (sq - hi.astype(F32)).astype(BF16)
    ones = jnp.ones((2 * HEAD_PAD, HEAD_PAD), BF16)
    total = jnp.dot(jnp.concatenate([hi, lo], axis=1), ones, preferred_element_type=F32)
    return xh * lax.rsqrt(total * (1.0 / dim) + EPS) * g


def _mla_kv_heads(ckv_bf, kpe, w_ukv_ref, g_mk, cos_m, sin_m, k_ref, v_ref):
    kv = jnp.dot(ckv_bf, w_ukv_ref[...], preferred_element_type=F32)
    for h in range(MLA_HEADS):
        kh = kv[:, h * HEAD_PAD:(h + 1) * HEAD_PAD] + kpe
        kh = _head_norm(kh, g_mk, MLA_QK)
        if cos_m is not None:
            kh = _rope(kh, cos_m, sin_m, MLA_ROPE // 4)
        k_ref[:, h * HEAD_PAD:(h + 1) * HEAD_PAD] = kh.astype(BF16)
    v_ref[:, 0:MLA_HEADS * MLA_V] = kv[:, MLA_HEADS * HEAD_PAD:].astype(BF16)


def _attn_pre_kernel(x_ref, mod_ref, g1_ref, w_in_ref, g_qa_ref, g_kva_ref, w_uq_ref, w_ukv_ref,
                     g_mq_ref, g_mk_ref, g_gq_ref, g_gk_ref, cos_m_ref, sin_m_ref, cos_g_ref, sin_g_ref,
                     q_ref, k_ref, v_ref, ckv_ref, kpe_ref, gk_ref, gv_ref):
    m = mod_ref[0]
    sh, sc = m[0:1], m[1:2]
    h = (_rms(x_ref[...], g1_ref[...]) * (1.0 + sc) + sh).astype(BF16)
    p = jnp.dot(h, w_in_ref[...], preferred_element_type=F32)
    q_c = _rms(p[:, IN_Q:IN_CKV], g_qa_ref[...]).astype(BF16)
    ckv = _rms(p[:, IN_CKV:IN_KPE], g_kva_ref[...])
    kpe = p[:, IN_KPE:IN_GQ]
    ckv_ref[...] = ckv
    kpe_ref[...] = kpe
    gv = p[:, IN_GV:IN_EXT]
    gv_ref[...] = gv
    v_ref[:, MLA_HEADS * MLA_V:] = gv.astype(BF16)
    qm = jnp.dot(q_c, w_uq_ref[...], preferred_element_type=F32)

    def heads(rope):
        cos_m, sin_m = (cos_m_ref[...], sin_m_ref[...]) if rope else (None, None)
        cos_g, sin_g = (cos_g_ref[...], sin_g_ref[...]) if rope else (None, None)
        g_mq = g_mq_ref[...]
        for hd in range(MLA_HEADS):
            qh = _head_norm(qm[:, hd * HEAD_PAD:(hd + 1) * HEAD_PAD], g_mq, MLA_QK)
            if rope:
                qh = _rope(qh, cos_m, sin_m, MLA_ROPE // 4)
            q_ref[:, hd * HEAD_PAD:(hd + 1) * HEAD_PAD] = (qh * (1.0 / math.sqrt(MLA_QK))).astype(BF16)
        _mla_kv_heads(ckv.astype(BF16), kpe, w_ukv_ref, g_mk_ref[...], cos_m, sin_m, k_ref, v_ref)
        g_gq = g_gq_ref[...]
        for hd in range(GQA_HEADS):
            qh = _head_norm(p[:, IN_GQ + hd * HEAD_PAD:IN_GQ + (hd + 1) * HEAD_PAD], g_gq, GQA_DIM)
            if rope:
                qh = _rope(qh, cos_g, sin_g, GQA_DIM // 4)
            q_ref[:, (MLA_HEADS + hd) * HEAD_PAD:(MLA_HEADS + hd + 1) * HEAD_PAD] = (
                qh * (1.0 / math.sqrt(GQA_DIM))).astype(BF16)
        g_gk = g_gk_ref[...]
        for hd in range(GQA_KV_HEADS):
            kh = _head_norm(p[:, IN_GK + hd * HEAD_PAD:IN_GK + (hd + 1) * HEAD_PAD], g_gk, GQA_DIM)
            gk_ref[:, hd * HEAD_PAD:(hd + 1) * HEAD_PAD] = kh
            if rope:
                kh = _rope(kh, cos_g, sin_g, GQA_DIM // 4)
            k_ref[:, (MLA_HEADS + hd) * HEAD_PAD:(MLA_HEADS + hd + 1) * HEAD_PAD] = kh.astype(BF16)

    is_latent = pl.program_id(0) >= N_PROMPT // x_ref.shape[0]
    pl.when(is_latent)(lambda: heads(True))
    pl.when(jnp.logical_not(is_latent))(lambda: heads(False))


def _attn_pre(x, mod_l, g1, aw, tabs):
    tm = TM_PRE
    row = _mod_row(tm)
    full = lambda a: pl.BlockSpec(a.shape, lambda i: (0,) * a.ndim)
    tok = lambda w: pl.BlockSpec((tm, w), lambda i: (i, 0))
    pos_block = lambda i: (jnp.maximum(i - N_PROMPT // tm, 0) % (DEC_SEQ // tm), 0)
    consts = [g1.reshape(1, D_MODEL), aw["w_in"], aw["g_qa"], aw["g_kva"], aw["w_uq"], aw["w_ukv"],
              aw["g_mq"], aw["g_mk"], aw["g_gq"], aw["g_gk"]]
    return pl.pallas_call(
        _attn_pre_kernel,
        out_shape=[
            jax.ShapeDtypeStruct((N_TOK, N_HEADS * HEAD_PAD), BF16),
            jax.ShapeDtypeStruct((N_TOK, K_ALL), BF16),
            jax.ShapeDtypeStruct((N_TOK, V_ALL), BF16),
            jax.ShapeDtypeStruct((N_TOK, KV_LORA), F32),
            jax.ShapeDtypeStruct((N_TOK, HEAD_PAD), F32),
            jax.ShapeDtypeStruct((N_TOK, GQA_KV_HEADS * HEAD_PAD), F32),
            jax.ShapeDtypeStruct((N_TOK, GQA_KV_HEADS * GQA_DIM), F32),
        ],
        grid=(N_TOK // tm,),
        in_specs=[tok(D_MODEL), pl.BlockSpec((1, 6, D_MODEL), lambda i: (row(i), 0, 0))]
        + [full(a) for a in consts] + [pl.BlockSpec((tm, HEAD_PAD), pos_block)] * 4,
        out_specs=[tok(N_HEADS * HEAD_PAD), tok(K_ALL), tok(V_ALL), tok(KV_LORA), tok(HEAD_PAD),
                   tok(GQA_KV_HEADS * HEAD_PAD), tok(GQA_KV_HEADS * GQA_DIM)],
        compiler_params=_cparams(("arbitrary",)),
        name="attn_pre",
    )(x, mod_l, *consts, *tabs)


def _cache_kv_kernel(ckv_ref, kpe_ref, w_ukv_ref, g_mk_ref, k_ref, v_ref):
    _mla_kv_heads(ckv_ref[...].astype(BF16), kpe_ref[...], w_ukv_ref, g_mk_ref[...], None, None, k_ref, v_ref)


def _cache_kv(ckv, kpe_pad, aw):
    n = ckv.shape[0]
    return pl.pallas_call(
        _cache_kv_kernel,
        out_shape=[jax.ShapeDtypeStruct((n, MLA_HEADS * HEAD_PAD), BF16),
                   jax.ShapeDtypeStruct((n, MLA_HEADS * MLA_V), BF16)],
        name="cache_kv",
    )(ckv, kpe_pad, aw["w_ukv"], aw["g_mk"])


def _attn_kernel(*refs, n_seg, n_sub):
    q_ref, kv_refs = refs[0], refs[1:1 + 2 * n_seg]
    x_ref, mod_ref, wo_ref, o_ref, oh_scr = refs[1 + 2 * n_seg:]
    tq = q_ref.shape[0] // n_sub
    for hd in range(N_HEADS):
        if hd < MLA_HEADS:
            kc, vc = hd * HEAD_PAD, hd * MLA_V
        else:
            kvh = (hd - MLA_HEADS) // GQA_REP
            kc, vc = (MLA_HEADS + kvh) * HEAD_PAD, MLA_HEADS * MLA_V + kvh * GQA_DIM
        for u in range(n_sub):
            rows = lambda ref: slice(u * (ref.shape[0] // n_sub), (u + 1) * (ref.shape[0] // n_sub))
            q = q_ref[u * tq:(u + 1) * tq, hd * HEAD_PAD:(hd + 1) * HEAD_PAD]
            ss = [lax.dot_general(q, kv_refs[2 * j][rows(kv_refs[2 * j]), kc:kc + HEAD_PAD],
                                  (((1,), (1,)), ((), ())), preferred_element_type=F32) for j in range(n_seg)]
            m = functools.reduce(jnp.maximum, [jnp.max(s, axis=-1, keepdims=True) for s in ss])
            ps = [jnp.exp(s - m) for s in ss]
            den = sum(jnp.sum(p, axis=-1, keepdims=True) for p in ps)
            o = sum(jnp.dot(p.astype(BF16), kv_refs[2 * j + 1][rows(kv_refs[2 * j + 1]), vc:vc + MLA_V],
                            preferred_element_type=F32) for j, p in enumerate(ps))
            oh_scr[u * tq:(u + 1) * tq, hd * MLA_V:(hd + 1) * MLA_V] = (o / den).astype(BF16)
    gt = mod_ref[0][2:3]
    o_ref[...] = x_ref[...] + gt * jnp.dot(oh_scr[...], wo_ref[...], preferred_element_type=F32)


def _attention(q, segs, x, mod_l, w_out, tq, n_batch, t_len, tok0, mod_row, n_sub=1):
    tiles = t_len // tq
    qmap = lambda bi, qi: (tok0 // tq + bi * tiles + qi, 0)
    kv_specs, kv_args = [], []
    for k, v, rows, blk0 in segs:
        kv_specs += [pl.BlockSpec((rows, K_ALL), lambda bi, qi, blk0=blk0: (blk0 + bi, 0)),
                     pl.BlockSpec((rows, V_ALL), lambda bi, qi, blk0=blk0: (blk0 + bi, 0))]
        kv_args += [k, v]
    return pl.pallas_call(
        functools.partial(_attn_kernel, n_seg=len(segs), n_sub=n_sub),
        out_shape=jax.ShapeDtypeStruct((N_TOK, D_MODEL), F32),
        grid=(n_batch, tiles),
        in_specs=[pl.BlockSpec((tq, N_HEADS * HEAD_PAD), qmap)] + kv_specs + [
            pl.BlockSpec((tq, D_MODEL), qmap),
            pl.BlockSpec((1, 6, D_MODEL), lambda bi, qi: (mod_row(bi), 0, 0)),
            pl.BlockSpec((D_MODEL, D_MODEL), lambda bi, qi: (0, 0)),
        ],
        out_specs=pl.BlockSpec((tq, D_MODEL), qmap),
        scratch_shapes=[pltpu.VMEM((tq, D_MODEL), BF16)],
        input_output_aliases={1 + len(kv_args): 0},
        compiler_params=_cparams(("arbitrary", "arbitrary")),
        name="attention",
    )(q, *kv_args, x, mod_l, w_out)


def _cmul(ar, ai, br, bi):
    return ar * br - ai * bi, ar * bi + ai * br


def _ssm_ops_kernel(a_ref, dt_ref, bt_ref, c_ref, ops_ref, toep_ref, wct_ref, lam_ref,
                    cl_re_scr, cl_im_scr, op_scr, tp_scr, wc_scr):
    L, H, P = SSM_CHUNK, SSM_GROUP, SSM_STATE
    lane = lax.broadcasted_iota(jnp.int32, (H, CHUNK_W), 1)

    a_re, a_im = a_ref[0], a_ref[1]
    dt_all = jnp.exp(dt_ref[...])
    mag = jnp.exp(a_re * dt_all)
    ang = a_im * dt_all
    ab_re, ab_im = mag * jnp.cos(ang), mag * jnp.sin(ang)
    den = a_re * a_re + a_im * a_im
    n_re, n_im = ab_re - 1.0, ab_im
    k_re_all = (n_re * a_re + n_im * a_im) / den
    k_im_all = (n_im * a_re - n_re * a_im) / den
    pw_all = [(jnp.ones_like(ab_re), jnp.zeros_like(ab_im))]
    for _ in range(L):
        pw_all.append(_cmul(pw_all[-1][0], pw_all[-1][1], ab_re, ab_im))

    for g in range(ops_ref.shape[0]):
        lag = []
        for d in range(2):
            r = 2 * g + d
            bt_re, bt_im = bt_ref[g, d, 0], bt_ref[g, d, 1]
            bb_re, bb_im = _cmul(k_re_all[r:r + 1], k_im_all[r:r + 1], bt_re, bt_im)
            c_re, c_im = c_ref[g, d, 0], c_ref[g, d, 1]
            pw = [(p_re[r:r + 1], p_im[r:r + 1]) for p_re, p_im in pw_all]

            for e in range(L + 1):
                pr, pi = pw[e] if d == 0 else pw[L - e]
                cr, ci = _cmul(c_re, c_im, pr, pi)
                cl_re_scr[r,e * H:(e + 1) * H, :] = cr
                cl_im_scr[r,e * H:(e + 1) * H, :] = ci
            lo = 0 if d == 0 else H
            lag.append(lax.dot_general(bb_re, cl_re_scr[r,lo:lo + CHUNK_W, :], (((1,), (1,)), ((), ())),
                                       preferred_element_type=F32, precision=HIGHEST)
                       - lax.dot_general(bb_im, cl_im_scr[r,lo:lo + CHUNK_W, :], (((1,), (1,)), ((), ())),
                                         preferred_element_type=F32, precision=HIGHEST))
            wlo = H if d == 0 else 0
            wc_scr[g, :, d * STATE_W:d * STATE_W + P] = cl_re_scr[r,wlo:wlo + CHUNK_W, :]
            wc_scr[g, :, d * STATE_W + P:(d + 1) * STATE_W] = -cl_im_scr[r,wlo:wlo + CHUNK_W, :]
            for j in range(L):
                pj = pw[L - 1 - j] if d == 0 else pw[j]
                sr, si = _cmul(pj[0], pj[1], bb_re, bb_im)
                op_scr[g, j * H:(j + 1) * H, d * STATE_W:d * STATE_W + P] = sr
                op_scr[g, j * H:(j + 1) * H, d * STATE_W + P:(d + 1) * STATE_W] = si
            pr, pi = pw[L]
            lam_ref[g, 2 * d:2 * d + 1, 0:P] = pr
            lam_ref[g, 2 * d:2 * d + 1, P:STATE_W] = pr
            lam_ref[g, 2 * d + 1:2 * d + 2, 0:P] = -pi
            lam_ref[g, 2 * d + 1:2 * d + 2, P:STATE_W] = pi
        for j in range(L):
            fwd = lag[0] if j == 0 else jnp.where(lane >= H * j, pltpu.roll(lag[0], H * j, 1), 0.0)
            sft = (L - 1 - j) * H
            bwd = lag[1] if sft == 0 else jnp.where(lane < H * (j + 1), pltpu.roll(lag[1], CHUNK_W - sft, 1), 0.0)
            tp_scr[g, j * H:(j + 1) * H, :] = fwd + bwd
        ops_ref[g] = op_scr[g].astype(BF16)
        toep_ref[g] = tp_scr[g].astype(BF16)
        wct_ref[g] = wc_scr[g].astype(BF16)


def _ssm_ops(a, dt, bt, c):
    gb = GB_SSM
    G = SSM_GROUPS
    blk = lambda shape: pl.BlockSpec((gb,) + shape, lambda i: (i,) + (0,) * len(shape))
    return pl.pallas_call(
        _ssm_ops_kernel,
        out_shape=[jax.ShapeDtypeStruct((G, CHUNK_W, 2 * STATE_W), BF16),
                   jax.ShapeDtypeStruct((G, CHUNK_W, CHUNK_W), BF16),
                   jax.ShapeDtypeStruct((G, CHUNK_W, 2 * STATE_W), BF16),
                   jax.ShapeDtypeStruct((G, 4, STATE_W), F32)],
        grid=(G // gb,),
        in_specs=[pl.BlockSpec((2, 2 * gb, SSM_STATE), lambda i: (0, i, 0)),
                  pl.BlockSpec((2 * gb, SSM_STATE), lambda i: (i, 0)),
                  blk((2, 2, SSM_GROUP, SSM_STATE)), blk((2, 2, SSM_GROUP, SSM_STATE))],
        out_specs=[blk((CHUNK_W, 2 * STATE_W)), blk((CHUNK_W, CHUNK_W)), blk((CHUNK_W, 2 * STATE_W)),
                   blk((4, STATE_W))],
        scratch_shapes=[pltpu.VMEM((2 * gb, (SSM_CHUNK + 1) * SSM_GROUP, SSM_STATE), F32),
                        pltpu.VMEM((2 * gb, (SSM_CHUNK + 1) * SSM_GROUP, SSM_STATE), F32),
                        pltpu.VMEM((gb, CHUNK_W, 2 * STATE_W), F32),
                        pltpu.VMEM((gb, CHUNK_W, CHUNK_W), F32),
                        pltpu.VMEM((gb, CHUNK_W, 2 * STATE_W), F32)],
        compiler_params=_cparams(("arbitrary",)),
        name="ssm_ops",
    )(a, dt, bt, c)


GRANULES = LANES // SSM_GROUP


def _granule_transpose(arrs):
    a = list(arrs)
    lane = lax.broadcasted_iota(jnp.int32, a[0].shape, 1)
    for d in (4, 2, 1):
        upper = (lane & (d * SSM_GROUP)) != 0
        for s in range(GRANULES):
            if s & d:
                continue
            lo, hi = a[s], a[s + d]
            a[s] = jnp.where(upper, pltpu.roll(hi, d * SSM_GROUP, 1), lo)
            a[s + d] = jnp.where(upper, hi, pltpu.roll(lo, LANES - d * SSM_GROUP, 1))
    return a


def _step_select(tm):
    mb = tm // SSM_CHUNK
    idx = (np.arange(mb)[None, :] * SSM_CHUNK + np.arange(SSM_CHUNK)[:, None]).reshape(-1)
    return jnp.asarray(np.eye(tm, dtype=np.float32)[idx], BF16)


def _ssm_in_kernel(x_ref, mod_ref, g_ref, sel_ref, u_ref):
    m = mod_ref[0]
    h = (_rms(x_ref[...], g_ref[...]) * (1.0 + m[1:2]) + m[0:1]).astype(BF16)
    by_step = jnp.dot(sel_ref[...], h, preferred_element_type=F32)
    mb = x_ref.shape[0] // SSM_CHUNK
    for g8 in range(SSM_GROUPS // GRANULES):
        for half in range(SSM_CHUNK // GRANULES):
            a = [by_step[(GRANULES * half + s) * mb:(GRANULES * half + s + 1) * mb, g8 * LANES:(g8 + 1) * LANES]
                 for s in range(GRANULES)]
            b = _granule_transpose(a)
            for g in range(GRANULES):
                u_ref[g8 * GRANULES + g, :, half * LANES:(half + 1) * LANES] = b[g].astype(BF16)


def _ssm_in(x, mod_l, g1):
    tm = TM_SSM_IN
    row = _mod_row(tm)
    return pl.pallas_call(
        _ssm_in_kernel,
        out_shape=jax.ShapeDtypeStruct((SSM_GROUPS, N_CHUNK_ROWS, CHUNK_W), BF16),
        grid=(N_TOK // tm,),
        in_specs=[pl.BlockSpec((tm, D_MODEL), lambda i: (i, 0)),
                  pl.BlockSpec((1, 6, D_MODEL), lambda i: (row(i), 0, 0)),
                  pl.BlockSpec((1, D_MODEL), lambda i: (0, 0)),
                  pl.BlockSpec((tm, tm), lambda i: (0, 0))],
        out_specs=pl.BlockSpec((SSM_GROUPS, tm // SSM_CHUNK, CHUNK_W), lambda i: (0, i, 0)),
        compiler_params=_cparams(("arbitrary",)),
        name="ssm_in",
    )(x, mod_l, g1.reshape(1, D_MODEL), _step_select(tm))


def _step_major_perm():
    n = SEQ // SSM_CHUNK
    assert n == BATCH
    idx = np.arange(N_CHUNK_PROMPT).reshape(BATCH, n).T.reshape(-1)
    return jnp.asarray(np.eye(N_CHUNK_PROMPT, dtype=np.float32)[idx], BF16)


def _ssm_ends_kernel(u_ref, ops_ref, perm_ref, *s_refs):
    n_p = N_CHUNK_PROMPT
    for g in range(u_ref.shape[0]):
        u_ctx = jnp.dot(perm_ref[...], u_ref[g, 0:n_p], preferred_element_type=F32).astype(BF16)
        s_ctx = jnp.dot(u_ctx, ops_ref[g], preferred_element_type=F32)
        s_lat = jnp.dot(u_ref[g, n_p:], ops_ref[g], preferred_element_type=F32)
        for k, s_ref in enumerate(s_refs):
            s_ref[g, 0:n_p] = s_ctx[:, k * STATE_W:(k + 1) * STATE_W]
            s_ref[g, n_p:] = s_lat[:, k * STATE_W:(k + 1) * STATE_W]


def _ssm_ends(u, ops, perm):
    gb = GB_SSM
    G = SSM_GROUPS
    return pl.pallas_call(
        _ssm_ends_kernel,
        out_shape=[jax.ShapeDtypeStruct((G, N_CHUNK_ROWS, STATE_W), F32)] * 2,
        grid=(G // gb,),
        in_specs=[pl.BlockSpec((gb, N_CHUNK_ROWS, CHUNK_W), lambda i: (i, 0, 0)),
                  pl.BlockSpec((gb, CHUNK_W, 2 * STATE_W), lambda i: (i, 0, 0)),
                  pl.BlockSpec(perm.shape, lambda i: (0, 0))],
        out_specs=[pl.BlockSpec((gb, N_CHUNK_ROWS, STATE_W), lambda i: (i, 0, 0))] * 2,
        compiler_params=_cparams(("arbitrary",)),
        name="ssm_chunk_ends",
    )(u, ops, perm)


def _ssm_carry_kernel(sf_ref, sb_ref, lam_ref, h0_ref, xf_ref, xb_ref, fin_ref):
    n_g = sf_ref.shape[0] // N_CHUNK_ROWS
    n_p, n_s = SEQ // SSM_CHUNK, DEC_SEQ // SSM_CHUNK

    def swap(t):
        return pltpu.roll(t, SSM_STATE, 1)

    def step(x, lam, rf, rb):
        xf, xfs, xb, xbs = x
        a1f, a2f, a1b, a2b = lam
        xf_ref[rf, :] = xf
        xb_ref[rb, :] = xb
        sf, sb = sf_ref[rf, :], sb_ref[rb, :]
        return [a1f * xf + a2f * xfs + sf, a1f * xfs - a2f * xf + swap(sf),
                a1b * xb + a2b * xbs + sb, a1b * xbs - a2b * xb + swap(sb)]

    zero = jnp.zeros((BATCH, STATE_W), F32)
    for g0 in range(0, n_g, GB_CARRY):
        groups = range(g0, g0 + GB_CARRY)
        lam = [[jnp.broadcast_to(lam_ref[4 * g + k:4 * g + k + 1, :], (BATCH, STATE_W)) for k in range(4)]
               for g in groups]

        def body_p(c, carry, groups=groups, lam=lam):
            out = []
            for j, g in enumerate(groups):
                rf = pl.ds(pl.multiple_of(g * N_CHUNK_ROWS + c * BATCH, BATCH), BATCH)
                rb = pl.ds(pl.multiple_of(g * N_CHUNK_ROWS + (n_p - 1 - c) * BATCH, BATCH), BATCH)
                out += step(carry[4 * j:4 * j + 4], lam[j], rf, rb)
            return tuple(out)

        fin = lax.fori_loop(0, n_p, body_p, (zero,) * (4 * GB_CARRY), unroll=4)
        for j, g in enumerate(groups):
            fin_ref[g, 0] = fin[4 * j]
            fin_ref[g, 1] = fin[4 * j + 2]

    packs = [(g0, b) for g0 in range(0, n_g, 8) for b in range(DEC_BATCH)]
    lam = {g0: [lam_ref[pl.ds(4 * g0 + k, 8, stride=4), :] for k in range(4)] for g0 in range(0, n_g, 8)}

    def body_s(c, carry):
        out = []
        for j, (g0, b) in enumerate(packs):
            base = g0 * N_CHUNK_ROWS + N_CHUNK_PROMPT + b * n_s
            rf = pl.ds(base + c, 8, stride=N_CHUNK_ROWS)
            rb = pl.ds(base + (n_s - 1 - c), 8, stride=N_CHUNK_ROWS)
            out += step(carry[4 * j:4 * j + 4], lam[g0], rf, rb)
        return tuple(out)

    init = []
    for g0, b in packs:
        h0f = h0_ref[pl.ds(4 * g0 + b, 8, stride=4), :]
        h0b = h0_ref[pl.ds(4 * g0 + DEC_BATCH + b, 8, stride=4), :]
        init += [h0f, swap(h0f), h0b, swap(h0b)]
    lax.fori_loop(0, n_s, body_s, tuple(init), unroll=8)


def _ssm_carry(sf, sb, lam, h0):
    gb = GB_CARRY_BLOCK
    G = SSM_GROUPS
    st = pl.BlockSpec((gb * N_CHUNK_ROWS, STATE_W), lambda i: (i, 0))
    small = pl.BlockSpec((gb * 4, STATE_W), lambda i: (i, 0))
    return pl.pallas_call(
        _ssm_carry_kernel,
        out_shape=[jax.ShapeDtypeStruct(sf.shape, F32)] * 2 + [jax.ShapeDtypeStruct((G, 2, BATCH, STATE_W), F32)],
        grid=(G // gb,),
        in_specs=[st, st, small, small],
        out_specs=[st, st, pl.BlockSpec((gb, 2, BATCH, STATE_W), lambda i: (i, 0, 0, 0))],
        compiler_params=_cparams(("arbitrary",)),
        name="ssm_carry",
    )(sf, sb, lam, h0)


def _ssm_out_kernel(u_ref, toep_ref, xf_ref, xb_ref, wct_ref, perm_ref, y_ref, y_scr):
    n_p = N_CHUNK_PROMPT
    for g in range(GRANULES):
        y = jnp.dot(u_ref[g], toep_ref[g], preferred_element_type=F32)
        for d, x_ref in enumerate((xf_ref, xb_ref)):
            x_ctx = jnp.dot(perm_ref[...], x_ref[g, 0:n_p].astype(BF16), preferred_element_type=F32).astype(BF16)
            xin = jnp.concatenate([x_ctx, x_ref[g, n_p:].astype(BF16)], axis=0)
            w = wct_ref[g, :, d * STATE_W:(d + 1) * STATE_W]
            y += lax.dot_general(xin, w, (((1,), (1,)), ((), ())), preferred_element_type=F32)
        y_scr[g] = y
    rb = OUT_ROW_BLOCK
    for r in range(N_CHUNK_ROWS // rb):
        for half in range(SSM_CHUNK // GRANULES):
            a = [y_scr[g, r * rb:(r + 1) * rb, half * LANES:(half + 1) * LANES] for g in range(GRANULES)]
            b = _granule_transpose(a)
            for s in range(GRANULES):
                y_ref[pl.ds(r * rb * SSM_CHUNK + GRANULES * half + s, rb, stride=SSM_CHUNK), :] = b[s]


def _ssm_out(u, toep, xf, xb, wct, perm):
    gb = GRANULES
    G = SSM_GROUPS
    return pl.pallas_call(
        _ssm_out_kernel,
        out_shape=jax.ShapeDtypeStruct((N_TOK, D_MODEL), F32),
        grid=(G // gb,),
        in_specs=[pl.BlockSpec((gb, N_CHUNK_ROWS, CHUNK_W), lambda i: (i, 0, 0)),
                  pl.BlockSpec((gb, CHUNK_W, CHUNK_W), lambda i: (i, 0, 0)),
                  pl.BlockSpec((gb, N_CHUNK_ROWS, STATE_W), lambda i: (i, 0, 0)),
                  pl.BlockSpec((gb, N_CHUNK_ROWS, STATE_W), lambda i: (i, 0, 0)),
                  pl.BlockSpec((gb, CHUNK_W, 2 * STATE_W), lambda i: (i, 0, 0)),
                  pl.BlockSpec(perm.shape, lambda i: (0, 0))],
        out_specs=pl.BlockSpec((N_TOK, LANES), lambda i: (0, i)),
        scratch_shapes=[pltpu.VMEM((gb, N_CHUNK_ROWS, CHUNK_W), F32)],
        compiler_params=_cparams(("arbitrary",)),
        name="ssm_chunk_out",
    )(u, toep, xf, xb, wct, perm)


def _ssm_post_kernel(x_ref, mod_ref, g_ref, y_ref, d_ref, w_ref, b_ref, o_ref):
    m = mod_ref[0]
    x = x_ref[...]
    h = _rms(x, g_ref[...]) * (1.0 + m[1:2]) + m[0:1]
    y = h * d_ref[...] + y_ref[...]
    gl = jax.nn.gelu(y).astype(BF16)
    z = jnp.dot(gl, w_ref[...], preferred_element_type=F32) + b_ref[...]
    o_ref[...] = x + m[2:3] * (z[:, :D_MODEL] * jax.nn.sigmoid(z[:, D_MODEL:]))


def _ssm_post(x, mod_l, g1, y, d_skip, w_glu, b_glu):
    tm = TM_SSM
    row = _mod_row(tm)
    full = lambda a: pl.BlockSpec(a.shape, lambda i: (0,) * a.ndim)
    d2, b2 = d_skip.reshape(1, D_MODEL), b_glu.reshape(1, 2 * D_MODEL)
    g2 = g1.reshape(1, D_MODEL)
    return pl.pallas_call(
        _ssm_post_kernel,
        out_shape=jax.ShapeDtypeStruct((N_TOK, D_MODEL), F32),
        grid=(N_TOK // tm,),
        in_specs=[pl.BlockSpec((tm, D_MODEL), lambda i: (i, 0)),
                  pl.BlockSpec((1, 6, D_MODEL), lambda i: (row(i), 0, 0)),
                  full(g2), pl.BlockSpec((tm, D_MODEL), lambda i: (i, 0)), full(d2), full(w_glu), full(b2)],
        out_specs=pl.BlockSpec((tm, D_MODEL), lambda i: (i, 0)),
        compiler_params=_cparams(("arbitrary",)),
        name="ssm_post",
    )(x, mod_l, g2, y, d2, w_glu, b2)


def _rope_tables():
    rows = DEC_SEQ // GRID_W
    t = np.arange(DEC_SEQ)
    row, col = (t // GRID_W).astype(np.float32), (t % GRID_W).astype(np.float32)

    def table(rot_dim, lane0):
        n_freq = rot_dim // 4
        inv_freq = jnp.asarray(ROPE_THETA, F32) ** (-jnp.arange(n_freq, dtype=F32) / n_freq)
        ang_row = jnp.asarray(row)[:, None] * inv_freq
        ang_col = jnp.asarray(col)[:, None] * inv_freq
        ang = jnp.concatenate([ang_row, ang_row, ang_col, ang_col], axis=1)
        sign = jnp.tile(jnp.concatenate([-jnp.ones(n_freq, F32), jnp.ones(n_freq, F32)]), 2)
        cos = jnp.ones((DEC_SEQ, HEAD_PAD), F32).at[:, lane0:lane0 + rot_dim].set(jnp.cos(ang))
        sin = jnp.zeros((DEC_SEQ, HEAD_PAD), F32).at[:, lane0:lane0 + rot_dim].set(jnp.sin(ang) * sign)
        return cos, sin

    assert rows * GRID_W == DEC_SEQ
    return table(MLA_ROPE, MLA_NOPE) + table(GQA_DIM, 0)


def _pad_heads(w, n_heads, dim):
    lead = w.shape[:-1]
    w = w.reshape(lead + (n_heads, dim))
    w = jnp.pad(w, [(0, 0)] * len(lead) + [(0, 0), (0, HEAD_PAD - dim)])
    return w.reshape(lead + (n_heads * HEAD_PAD,))


def _attn_weights(w_in, g_qa, g_kva, w_uq, w_ukv, g_mq, g_mk, g_gq, g_gk, w_out):
    o1, o2, o3 = Q_LORA, Q_LORA + KV_LORA, Q_LORA + KV_LORA + MLA_ROPE
    o4 = o3 + GQA_HEADS * GQA_DIM
    o5 = o4 + GQA_KV_HEADS * GQA_DIM
    kpe = jnp.pad(w_in[:, o2:o3], ((0, 0), (MLA_NOPE, HEAD_PAD - MLA_QK)))
    w_in_ext = jnp.concatenate([w_in[:, :o2], kpe, _pad_heads(w_in[:, o3:o4], GQA_HEADS, GQA_DIM),
                                _pad_heads(w_in[:, o4:o5], GQA_KV_HEADS, GQA_DIM), w_in[:, o5:]], axis=1)
    ukv = w_ukv.reshape(KV_LORA, MLA_HEADS, MLA_NOPE + MLA_V)
    w_ukv_perm = jnp.concatenate([_pad_heads(ukv[:, :, :MLA_NOPE].reshape(KV_LORA, -1), MLA_HEADS, MLA_NOPE),
                                  ukv[:, :, MLA_NOPE:].reshape(KV_LORA, -1)], axis=1)
    pad_g = lambda g: jnp.pad(g, (0, HEAD_PAD - g.shape[0])).reshape(1, HEAD_PAD)
    return dict(w_in=w_in_ext.astype(BF16), g_qa=g_qa.reshape(1, -1), g_kva=g_kva.reshape(1, -1),
                w_uq=_pad_heads(w_uq, MLA_HEADS, MLA_QK).astype(BF16), w_ukv=w_ukv_perm.astype(BF16),
                g_mq=pad_g(g_mq), g_mk=pad_g(g_mk), g_gq=pad_g(g_gq), g_gk=pad_g(g_gk),
                w_out=w_out.astype(BF16))


def _attn_layer(x, mod_l, g1, aw, tabs, cache):
    q, k, v, ckv, kpe, gk, gv = _attn_pre(x, mod_l, g1, aw, tabs)
    c_ckv, c_krope, c_gk, c_gv = cache
    n_c = DEC_BATCH * PAST_LEN
    kpe_c = jnp.pad(c_krope.reshape(n_c, MLA_ROPE), ((0, 0), (MLA_NOPE, HEAD_PAD - MLA_QK)))
    km_c, vm_c = _cache_kv(c_ckv.reshape(n_c, KV_LORA), kpe_c, aw)
    k_c = jnp.concatenate([km_c, _pad_heads(c_gk.reshape(n_c, -1), GQA_KV_HEADS, GQA_DIM).astype(BF16)], axis=1)
    v_c = jnp.concatenate([vm_c, c_gv.reshape(n_c, -1).astype(BF16)], axis=1)
    x = _attention(q, [(k, v, 2 * SEQ, 0)], x, mod_l, aw["w_out"], 2 * SEQ, BATCH // 2, 2 * SEQ, 0, lambda b: 0,
                   n_sub=2)
    x = _attention(q, [(k_c, v_c, PAST_LEN, 0), (k, v, DEC_SEQ, N_PROMPT // DEC_SEQ)], x, mod_l, aw["w_out"],
                   TQ_SAMPLE, DEC_BATCH, DEC_SEQ, N_PROMPT, lambda b: 1 + b)
    new = (ckv[:N_PROMPT].reshape(BATCH, SEQ, KV_LORA),
           kpe[:N_PROMPT, MLA_NOPE:MLA_QK].reshape(BATCH, SEQ, MLA_ROPE),
           gk[:N_PROMPT].reshape(BATCH, SEQ, GQA_KV_HEADS, HEAD_PAD)[..., :GQA_DIM],
           gv[:N_PROMPT].reshape(BATCH, SEQ, GQA_KV_HEADS, GQA_DIM))
    return x, new


def _ssm_layer(x, mod_l, g1, a_re, a_im, log_dt, b_re, b_im, c_re, c_im, d_skip, w_glu, b_glu, state0):
    G, P, H = SSM_GROUPS, SSM_STATE, SSM_GROUP
    a = jnp.stack([a_re, a_im], axis=0).transpose(0, 2, 1, 3).reshape(2, G * 2, P)
    dt = jnp.broadcast_to(log_dt.transpose(1, 0)[:, :, None], (G, 2, P)).reshape(G * 2, P)
    bt = jnp.stack([b_re, b_im], axis=2).transpose(1, 0, 2, 4, 3)
    c = jnp.stack([c_re, c_im], axis=2).transpose(1, 0, 2, 3, 4)
    ops, toep, wct, lam = _ssm_ops(a, dt, bt, c)

    u = _ssm_in(x, mod_l, g1)
    perm = _step_major_perm()
    sf, sb = _ssm_ends(u, ops, perm)
    h0 = state0.transpose(2, 1, 0, 4, 3).reshape(G * 2 * DEC_BATCH, STATE_W)
    flat = lambda s: s.reshape(G * N_CHUNK_ROWS, STATE_W)
    xf, xb, fin = _ssm_carry(flat(sf), flat(sb), lam.reshape(G * 4, STATE_W), h0)
    y = _ssm_out(u, toep, xf.reshape(sf.shape), xb.reshape(sb.shape), wct, perm)
    x = _ssm_post(x, mod_l, g1, y, d_skip, w_glu.astype(BF16), b_glu)
    fin = fin.reshape(G, 2, BATCH, 2, P).transpose(2, 1, 0, 4, 3)
    return x, fin


def kernel(x_prompt, x_sample, c, cache_mla_ckv, cache_mla_krope, cache_gqa_k, cache_gqa_v, state_ssm, c_ctx,
           norm1_g, norm2_g, w_mod, b_mod,
           attn_w_in, attn_qa_norm_g, attn_kva_norm_g, attn_w_uq, attn_w_ukv,
           attn_mla_q_norm_g, attn_mla_k_norm_g, attn_gqa_q_norm_g, attn_gqa_k_norm_g, attn_w_out,
           ssm_a_re, ssm_a_im, ssm_log_dt, ssm_b_re, ssm_b_im, ssm_c_re, ssm_c_im, ssm_d, ssm_w_glu, ssm_b_glu,
           ffn_w_up, ffn_conv_w, ffn_conv_b, ffn_w_down):
    x = jnp.concatenate([x_prompt.reshape(N_PROMPT, D_MODEL), x_sample.reshape(N_SAMPLE, D_MODEL)], axis=0)
    cond8 = jnp.concatenate([c_ctx[None, :], c, jnp.zeros((8 - 1 - DEC_BATCH, D_MODEL), F32)], axis=0)
    mod = _modulation(cond8, w_mod, b_mod).reshape(DEPTH, 8, 6, D_MODEL)
    tabs = _rope_tables()

    w_up_bf, w_down_bf = ffn_w_up.astype(BF16), ffn_w_down.astype(BF16)
    new_attn, new_ssm = [], []
    for i in range(DEPTH):
        j = i // 2
        if i % 2 == 0:
            aw = _attn_weights(attn_w_in[j], attn_qa_norm_g[j], attn_kva_norm_g[j], attn_w_uq[j], attn_w_ukv[j],
                               attn_mla_q_norm_g[j], attn_mla_k_norm_g[j], attn_gqa_q_norm_g[j],
                               attn_gqa_k_norm_g[j], attn_w_out[j])
            cache = (cache_mla_ckv[:, j], cache_mla_krope[:, j], cache_gqa_k[:, j], cache_gqa_v[:, j])
            x, new = _attn_layer(x, mod[i], norm1_g[i], aw, tabs, cache)
            new_attn.append(new)
        else:
            x, fin = _ssm_layer(x, mod[i], norm1_g[i], ssm_a_re[j], ssm_a_im[j], ssm_log_dt[j], ssm_b_re[j],
                                ssm_b_im[j], ssm_c_re[j], ssm_c_im[j], ssm_d[j], ssm_w_glu[j], ssm_b_glu[j],
                                state_ssm[:, j])
            new_ssm.append(fin)
        ffn = functools.partial(_conv_ffn, x, mod[i], norm2_g[i], i, w_up_bf, ffn_conv_w, ffn_conv_b, w_down_bf)
        if i < DEPTH - 1:
            x = ffn()
        else:
            y_prompt, y_sample = ffn(0, N_PROMPT), ffn(N_PROMPT, N_SAMPLE)

    outs = [jnp.stack([n[k] for n in new_attn], axis=1) for k in range(4)]
    return (y_prompt.reshape(BATCH, SEQ, D_MODEL), y_sample.reshape(DEC_BATCH, DEC_SEQ, D_MODEL),
            outs[0], outs[1], outs[2], outs[3], jnp.stack(new_ssm, axis=1))
```
